```python
import jax, jax.numpy as jnp
from jax import lax
import numpy as np

D_MODEL = 2048
BATCH = 2
SEQ = 4096
DEPTH = 2
DEC_BATCH = 32
DEC_SEQ = 8
PAST_LEN = 16384
PAGE_SIZE = 128

N_MIXERS = 2
N_CONV_LAYERS = (DEPTH + N_MIXERS - 1) // N_MIXERS
N_ATTN_LAYERS = DEPTH // N_MIXERS
CONV_WIDTH = 3
HEAD_DIM = 64
N_HEADS = D_MODEL // HEAD_DIM
N_KV_HEADS = N_HEADS // 8
GROUP = N_HEADS // N_KV_HEADS
WINDOW = 128
BLOCK = WINDOW
D_FF = 4 * D_MODEL
ROPE_THETA = 10000.0
EPS = 1e-6

kernel_name = "hybrid_shortconv_swa_sink_step"


def rms_norm(x, g):
    xf = x.astype(jnp.float32)
    y = xf * lax.rsqrt(jnp.mean(xf * xf, axis=-1, keepdims=True) + EPS)
    return (y * g.astype(jnp.float32)).astype(x.dtype)


def rope(x, pos):
    half = HEAD_DIM // 2
    inv = ROPE_THETA ** (-jnp.arange(half, dtype=jnp.float32) / half)
    ang = pos.astype(jnp.float32)[:, None] * inv[None, :]
    cos = jnp.cos(ang)[None, :, None, :]
    sin = jnp.sin(ang)[None, :, None, :]
    xf = x.astype(jnp.float32)
    x1, x2 = xf[..., :half], xf[..., half:]
    return jnp.concatenate([x1 * cos - x2 * sin, x2 * cos + x1 * sin], axis=-1).astype(x.dtype)


def short_conv_mixer(h, conv_past, w_in, w_conv, w_out):
    T = h.shape[1]
    b_gate, c_gate, v = jnp.split(h @ w_in, 3, axis=-1)
    u = c_gate * v
    up = jnp.concatenate([conv_past.astype(u.dtype), u], axis=1)
    conv = w_conv[0] * up[:, 0:T]
    for kk in range(1, CONV_WIDTH):
        conv = conv + w_conv[kk] * up[:, kk:kk + T]
    y = (b_gate * conv) @ w_out
    return y, up[:, -(CONV_WIDTH - 1):]


def attn_project(h, pos, w_qkv, q_gain, k_gain):
    B, T, _ = h.shape
    qkv = h @ w_qkv
    q, k, v = jnp.split(qkv, [N_HEADS * HEAD_DIM, (N_HEADS + N_KV_HEADS) * HEAD_DIM], axis=-1)
    q = q.reshape(B, T, N_HEADS, HEAD_DIM)
    k = k.reshape(B, T, N_KV_HEADS, HEAD_DIM)
    v = v.reshape(B, T, N_KV_HEADS, HEAD_DIM)
    q = rope(rms_norm(q, q_gain), pos)
    k = rope(rms_norm(k, k_gain), pos)
    return q, k, v


def sink_softmax(s, sinks):
    sk = sinks.astype(jnp.float32).reshape(N_KV_HEADS, GROUP, 1, 1)
    m = jnp.maximum(jnp.max(s, axis=-1, keepdims=True), sk)
    p = jnp.exp(s - m)
    return p / (jnp.sum(p, axis=-1, keepdims=True) + jnp.exp(sk - m))


def swa_prompt(q, k, v, sinks):
    B, T = q.shape[:2]
    nb = T // BLOCK
    scale = HEAD_DIM ** -0.5
    qb = q.reshape(B, nb, BLOCK, N_KV_HEADS, GROUP, HEAD_DIM)
    kb = k.reshape(B, nb, BLOCK, N_KV_HEADS, HEAD_DIM)
    vb = v.reshape(B, nb, BLOCK, N_KV_HEADS, HEAD_DIM)
    kpad = jnp.zeros_like(kb[:, :1])
    vpad = jnp.zeros_like(vb[:, :1])
    k2 = jnp.concatenate([jnp.concatenate([kpad, kb[:, :-1]], axis=1), kb], axis=2)
    v2 = jnp.concatenate([jnp.concatenate([vpad, vb[:, :-1]], axis=1), vb], axis=2)
    s = jnp.einsum('bnqkgd,bnskd->bnkgqs', qb, k2, preferred_element_type=jnp.float32) * scale
    blk = jnp.arange(nb, dtype=jnp.int32)[:, None]
    qpos = blk * BLOCK + jnp.arange(BLOCK, dtype=jnp.int32)[None, :]
    kpos = (blk - 1) * BLOCK + jnp.arange(2 * BLOCK, dtype=jnp.int32)[None, :]
    diff = qpos[:, :, None] - kpos[:, None, :]
    mask = (diff >= 0) & (diff < WINDOW) & (kpos[:, None, :] >= 0)
    s = jnp.where(mask[None, :, None, None], s, -jnp.inf)
    p = sink_softmax(s, sinks)
    o = jnp.einsum('bnkgqs,bnskd->bnqkgd', p.astype(v.dtype), v2)
    return o.reshape(B, T, N_HEADS * HEAD_DIM)


def swa_sample(q, k_new, v_new, k_buf, v_buf, sinks, past_len):
    B, T = q.shape[:2]
    W = k_buf.shape[1]
    scale = HEAD_DIM ** -0.5
    k_all = jnp.concatenate([k_buf.astype(k_new.dtype), k_new], axis=1)
    v_all = jnp.concatenate([v_buf.astype(v_new.dtype), v_new], axis=1)
    qg = q.reshape(B, T, N_KV_HEADS, GROUP, HEAD_DIM)
    s = jnp.einsum('btkgd,bskd->bkgts', qg, k_all, preferred_element_type=jnp.float32) * scale
    qpos = past_len + jnp.arange(T, dtype=jnp.int32)
    kpos = past_len - W + jnp.arange(W + T, dtype=jnp.int32)
    diff = qpos[:, None] - kpos[None, :]
    mask = (diff >= 0) & (diff < WINDOW)
    s = jnp.where(mask[None, None, None], s, -jnp.inf)
    p = sink_softmax(s, sinks)
    o = jnp.einsum('bkgts,bskd->btkgd', p.astype(v_all.dtype), v_all)
    return o.reshape(B, T, N_HEADS * HEAD_DIM), k_all[:, -W:], v_all[:, -W:]


def sq_relu_mlp(h, w_up, w_down):
    a = jax.nn.relu(h @ w_up)
    return (a * a) @ w_down


def setup_inputs(seed: int = 0) -> dict:
    key = jax.random.key(seed)
    ks = jax.random.split(key, 20)
    f32 = jnp.float32
    D = D_MODEL
    W = min(WINDOW, PAST_LEN)
    qkv_cols = (N_HEADS + 2 * N_KV_HEADS) * HEAD_DIM
    nrm = lambda k, shape, s: jax.random.normal(k, shape, f32) * s
    return {
        "x_prompt": nrm(ks[0], (BATCH, SEQ, D), 1.0),
        "x_sample": nrm(ks[1], (DEC_BATCH, DEC_SEQ, D), 1.0),
        "state_conv": nrm(ks[2], (N_CONV_LAYERS, DEC_BATCH, CONV_WIDTH - 1, D), 1.0),
        "cache_k_win": nrm(ks[3], (N_ATTN_LAYERS, DEC_BATCH, W, N_KV_HEADS, HEAD_DIM), 1.0),
        "cache_v_win": nrm(ks[4], (N_ATTN_LAYERS, DEC_BATCH, W, N_KV_HEADS, HEAD_DIM), 1.0),
        "ln_mix": 1.0 + nrm(ks[5], (DEPTH, D), 0.02),
        "ln_mlp": 1.0 + nrm(ks[6], (DEPTH, D), 0.02),
        "w_conv_in": nrm(ks[7], (N_CONV_LAYERS, D, 3 * D), D ** -0.5),
        "w_conv": nrm(ks[8], (N_CONV_LAYERS, CONV_WIDTH, D), CONV_WIDTH ** -0.5),
        "w_conv_out": nrm(ks[9], (N_CONV_LAYERS, D, D), D ** -0.5),
        "w_qkv": nrm(ks[10], (N_ATTN_LAYERS, D, qkv_cols), D ** -0.5),
        "w_attn_out": nrm(ks[11], (N_ATTN_LAYERS, N_HEADS * HEAD_DIM, D), (N_HEADS * HEAD_DIM) ** -0.5),
        "q_norm": 1.0 + nrm(ks[12], (N_ATTN_LAYERS, HEAD_DIM), 0.02),
        "k_norm": 1.0 + nrm(ks[13], (N_ATTN_LAYERS, HEAD_DIM), 0.02),
        "sinks": nrm(ks[14], (N_ATTN_LAYERS, N_HEADS), 1.0),
        "w_up": nrm(ks[15], (DEPTH, D, D_FF), D ** -0.5),
        "w_down": nrm(ks[16], (DEPTH, D_FF, D), D_FF ** -0.5),
    }


def reference(x_prompt, x_sample, state_conv, cache_k_win, cache_v_win, ln_mix, ln_mlp,
              w_conv_in, w_conv, w_conv_out, w_qkv, w_attn_out, q_norm, k_norm, sinks,
              w_up, w_down):
    yp, ys = x_prompt, x_sample
    Bp, Tp, D = x_prompt.shape
    Ts = x_sample.shape[1]
    pos_p = jnp.arange(Tp, dtype=jnp.int32)
    pos_s = PAST_LEN + jnp.arange(Ts, dtype=jnp.int32)
    conv_p, conv_s, kp, vp, kss, vss = [], [], [], [], [], []
    for i in range(DEPTH):
        j = i // N_MIXERS
        hp = rms_norm(yp, ln_mix[i])
        hs = rms_norm(ys, ln_mix[i])
        if i % N_MIXERS == 0:
            zero_past = jnp.zeros((Bp, CONV_WIDTH - 1, D), hp.dtype)
            op, sp = short_conv_mixer(hp, zero_past, w_conv_in[j], w_conv[j], w_conv_out[j])
            os_, ss = short_conv_mixer(hs, state_conv[j], w_conv_in[j], w_conv[j], w_conv_out[j])
            conv_p.append(sp)
            conv_s.append(ss)
        else:
            q, k, v = attn_project(hp, pos_p, w_qkv[j], q_norm[j], k_norm[j])
            op = swa_prompt(q, k, v, sinks[j]) @ w_attn_out[j]
            kp.append(k[:, -WINDOW:])
            vp.append(v[:, -WINDOW:])
            q2, k2, v2 = attn_project(hs, pos_s, w_qkv[j], q_norm[j], k_norm[j])
            o2, kb, vb = swa_sample(q2, k2, v2, cache_k_win[j], cache_v_win[j], sinks[j], PAST_LEN)
            os_ = o2 @ w_attn_out[j]
            kss.append(kb)
            vss.append(vb)
        yp = yp + op
        ys = ys + os_
        yp = yp + sq_relu_mlp(rms_norm(yp, ln_mlp[i]), w_up[i], w_down[i])
        ys = ys + sq_relu_mlp(rms_norm(ys, ln_mlp[i]), w_up[i], w_down[i])
    return (yp, ys, jnp.stack(conv_p), jnp.stack(conv_s), jnp.stack(kp), jnp.stack(vp), jnp.stack(kss), jnp.stack(vss))
```

```python
import functools

import jax
import jax.numpy as jnp
from jax import lax
from jax.experimental import pallas as pl
from jax.experimental.pallas import tpu as pltpu

F32 = jnp.float32
BF16 = jnp.bfloat16

HEAD_DIM = 64
N_HEADS = 32
N_KV_HEADS = 4
GROUP = N_HEADS // N_KV_HEADS
WINDOW = 128
PAST_LEN = 16384
ROPE_THETA = 10000.0
EPS = 1e-6
CONV_WIDTH = 3

SUBLANES = 8
LANES = 128
MIB = 1024 * 1024


def _params(n_grid_dims, vmem_mib):
    return pltpu.CompilerParams(
        dimension_semantics=("arbitrary",) * n_grid_dims,
        vmem_limit_bytes=vmem_mib * MIB,
    )


def _rms_norm(x, gain):
    return x * lax.rsqrt(jnp.mean(x * x, axis=-1, keepdims=True) + EPS) * gain


def _dot(a, b):
    return jnp.dot(a, b, preferred_element_type=F32)


def _mlp_body(x_ref, g_ref, wu_ref, wd_ref, o_ref, h_ref):
    f = pl.program_id(1)

    @pl.when(f == 0)
    def _():
        h_ref[...] = _rms_norm(x_ref[...], g_ref[...]).astype(BF16)

    a = jnp.maximum(_dot(h_ref[...], wu_ref[...]), 0.0)
    y = _dot((a * a).astype(BF16), wd_ref[...])

    @pl.when(f == 0)
    def _():
        o_ref[...] = x_ref[...] + y

    @pl.when(f != 0)
    def _():
        o_ref[...] += y


def _mlp(x, gain, w_up, w_down, *, tm, tf):
    m, d = x.shape
    ff = w_up.shape[1]
    return pl.pallas_call(
        _mlp_body,
        grid=(m // tm, ff // tf),
        in_specs=[
            pl.BlockSpec((tm, d), lambda i, f: (i, 0)),
            pl.BlockSpec((1, d), lambda i, f: (0, 0)),
            pl.BlockSpec((d, tf), lambda i, f: (0, f)),
            pl.BlockSpec((tf, d), lambda i, f: (f, 0)),
        ],
        out_specs=pl.BlockSpec((tm, d), lambda i, f: (i, 0)),
        out_shape=jax.ShapeDtypeStruct((m, d), F32),
        scratch_shapes=[pltpu.VMEM((tm, d), BF16)],
        compiler_params=_params(2, 52),
        name="mlp",
    )(x, gain, w_up, w_down)


def _proj_body(x_ref, a_ref, w_ref, o_ref):
    o_ref[...] = x_ref[...] + _dot(a_ref[...], w_ref[...])


def _proj_residual(x, a, w, *, tm):
    m, d = x.shape
    k = a.shape[1]
    return pl.pallas_call(
        _proj_body,
        grid=(m // tm,),
        in_specs=[
            pl.BlockSpec((tm, d), lambda i: (i, 0)),
            pl.BlockSpec((tm, k), lambda i: (i, 0)),
            pl.BlockSpec((k, d), lambda i: (0, 0)),
        ],
        out_specs=pl.BlockSpec((tm, d), lambda i: (i, 0)),
        out_shape=jax.ShapeDtypeStruct((m, d), F32),
        compiler_params=_params(1, 48),
        name="proj_residual",
    )(x, a, w)


def _conv_taps(u, u1, u2, w):
    return w[0:1, :] * u2 + w[1:2, :] * u1 + w[2:3, :] * u


def _conv_in_prompt_body(x_ref, g_ref, wb_ref, wc_ref, wv_ref, wconv_ref,
                         gate_ref, utail_ref, h_ref, carry_ref, *, tiles_per_seq):
    i = pl.program_id(0)
    j = pl.program_id(1)

    @pl.when(j == 0)
    def _():
        h_ref[...] = _rms_norm(x_ref[...], g_ref[...]).astype(BF16)

    @pl.when(i % tiles_per_seq == 0)
    def _():
        carry_ref[j] = jnp.zeros(carry_ref.shape[1:], F32)

    h = h_ref[...]
    u = _dot(h, wc_ref[...]) * _dot(h, wv_ref[...])
    tm = u.shape[0]
    prev = carry_ref[j]
    row8 = lax.broadcasted_iota(jnp.int32, (SUBLANES, 1), 0)
    u1 = pltpu.roll(u, 1, axis=0)
    u2 = pltpu.roll(u, 2, axis=0)
    top1 = jnp.where(row8 < 1, pltpu.roll(prev, 1, axis=0), u1[0:SUBLANES])
    top2 = jnp.where(row8 < 2, pltpu.roll(prev, 2, axis=0), u2[0:SUBLANES])
    u1 = jnp.concatenate([top1, u1[SUBLANES:]], axis=0)
    u2 = jnp.concatenate([top2, u2[SUBLANES:]], axis=0)
    conv = _conv_taps(u, u1, u2, wconv_ref[...])
    tail = u[tm - SUBLANES:]
    carry_ref[j] = tail
    utail_ref[0] = tail
    gate_ref[...] = (_dot(h, wb_ref[...]) * conv).astype(BF16)


def _conv_in_sample_body(x_ref, g_ref, wb_ref, wc_ref, wv_ref, wconv_ref, p1_ref, p2_ref,
                         gate_ref, u_ref, h_ref, *, seq_len):
    j = pl.program_id(1)

    @pl.when(j == 0)
    def _():
        h_ref[...] = _rms_norm(x_ref[...], g_ref[...]).astype(BF16)

    h = h_ref[...]
    u = _dot(h, wc_ref[...]) * _dot(h, wv_ref[...])
    step = lax.broadcasted_iota(jnp.int32, (u.shape[0], 1), 0) % seq_len
    u1 = jnp.where(step < 1, p1_ref[...], pltpu.roll(u, 1, axis=0))
    u2 = jnp.where(step < 2, p2_ref[...], pltpu.roll(u, 2, axis=0))
    conv = _conv_taps(u, u1, u2, wconv_ref[...])
    u_ref[...] = u
    gate_ref[...] = (_dot(h, wb_ref[...]) * conv).astype(BF16)


def _conv_in_specs(tm, d, tn):
    nj = d // tn
    return [
        pl.BlockSpec((tm, d), lambda i, j: (i, 0)),
        pl.BlockSpec((1, d), lambda i, j: (0, 0)),
        pl.BlockSpec((d, tn), lambda i, j: (0, j)),
        pl.BlockSpec((d, tn), lambda i, j: (0, nj + j)),
        pl.BlockSpec((d, tn), lambda i, j: (0, 2 * nj + j)),
        pl.BlockSpec((CONV_WIDTH, tn), lambda i, j: (0, j)),
    ]


def _conv_in_prompt(x, gain, w_in, w_conv, *, seq, tm, tn):
    m, d = x.shape
    body = functools.partial(_conv_in_prompt_body, tiles_per_seq=seq // tm)
    return pl.pallas_call(
        body,
        grid=(m // tm, d // tn),
        in_specs=_conv_in_specs(tm, d, tn),
        out_specs=[
            pl.BlockSpec((tm, tn), lambda i, j: (i, j)),
            pl.BlockSpec((1, SUBLANES, tn), lambda i, j: (i, 0, j)),
        ],
        out_shape=[
            jax.ShapeDtypeStruct((m, d), BF16),
            jax.ShapeDtypeStruct((m // tm, SUBLANES, d), F32),
        ],
        scratch_shapes=[
            pltpu.VMEM((tm, d), BF16),
            pltpu.VMEM((d // tn, SUBLANES, tn), F32),
        ],
        compiler_params=_params(2, 48),
        name="conv_in_prompt",
    )(x, gain, w_in, w_in, w_in, w_conv)


def _conv_in_sample(x, gain, w_in, w_conv, p1, p2, *, seq_len, tn):
    m, d = x.shape
    body = functools.partial(_conv_in_sample_body, seq_len=seq_len)
    return pl.pallas_call(
        body,
        grid=(1, d // tn),
        in_specs=_conv_in_specs(m, d, tn) + [
            pl.BlockSpec((m, tn), lambda i, j: (0, j)),
            pl.BlockSpec((m, tn), lambda i, j: (0, j)),
        ],
        out_specs=[
            pl.BlockSpec((m, tn), lambda i, j: (0, j)),
            pl.BlockSpec((m, tn), lambda i, j: (0, j)),
        ],
        out_shape=[
            jax.ShapeDtypeStruct((m, d), BF16),
            jax.ShapeDtypeStruct((m, d), F32),
        ],
        scratch_shapes=[pltpu.VMEM((m, d), BF16)],
        compiler_params=_params(2, 48),
        name="conv_in_sample",
    )(x, gain, w_in, w_in, w_in, w_conv, p1, p2)


def _head_norm_rope(z, gain, ones_blockdiag, cos, sin_signed):
    w = z.shape[1]
    zz = z * z
    hi = zz.astype(BF16)
    lo = (zz - hi.astype(F32)).astype(BF16)
    ssq = _dot(hi, ones_blockdiag) + _dot(lo, ones_blockdiag)
    zn = z * lax.rsqrt(ssq * (1.0 / HEAD_DIM) + EPS) * gain
    reps = w // cos.shape[1]
    cos = jnp.concatenate([cos] * reps, axis=1)
    sin_signed = jnp.concatenate([sin_signed] * reps, axis=1)
    lane = lax.broadcasted_iota(jnp.int32, (1, w), 1)
    first_half = (lane % HEAD_DIM) < (HEAD_DIM // 2)
    partner = jnp.where(first_half,
                        pltpu.roll(zn, w - HEAD_DIM // 2, axis=1),
                        pltpu.roll(zn, HEAD_DIM // 2, axis=1))
    return zn * cos + partner * sin_signed


def _qkv_body(x_ref, g_ref, w_ref, ones_ref, qgain_ref, kgain_ref, cos_ref, sin_ref,
              q_ref, kv_ref, h_ref, *, n_q_tiles):
    j = pl.program_id(1)

    @pl.when(j == 0)
    def _():
        h_ref[...] = _rms_norm(x_ref[...], g_ref[...]).astype(BF16)

    z = _dot(h_ref[...], w_ref[...])

    @pl.when(j < n_q_tiles)
    def _():
        q = _head_norm_rope(z, qgain_ref[...], ones_ref[...], cos_ref[...], sin_ref[...])
        q_ref[...] = (q * HEAD_DIM ** -0.5).astype(BF16)

    @pl.when(j == n_q_tiles)
    def _():
        kw = N_KV_HEADS * HEAD_DIM
        k = _head_norm_rope(z[:, :kw], kgain_ref[...], ones_ref[0:kw, 0:kw],
                            cos_ref[...], sin_ref[...])
        kv_ref[...] = jnp.concatenate([k, z[:, kw:]], axis=1)


def _qkv(x, gain, w_qkv, ones_blockdiag, q_gain, k_gain, cos, sin_signed, *, tm, rope_tiles):
    m, d = x.shape
    qw = N_HEADS * HEAD_DIM
    kvw = 2 * N_KV_HEADS * HEAD_DIM
    tn = kvw
    n_q_tiles = qw // tn
    body = functools.partial(_qkv_body, n_q_tiles=n_q_tiles)
    return pl.pallas_call(
        body,
        grid=(m // tm, n_q_tiles + 1),
        in_specs=[
            pl.BlockSpec((tm, d), lambda i, j: (i, 0)),
            pl.BlockSpec((1, d), lambda i, j: (0, 0)),
            pl.BlockSpec((d, tn), lambda i, j: (0, j)),
            pl.BlockSpec((tn, tn), lambda i, j: (0, 0)),
            pl.BlockSpec((1, tn), lambda i, j: (0, 0)),
            pl.BlockSpec((1, tn // 2), lambda i, j: (0, 0)),
            pl.BlockSpec((tm, LANES), lambda i, j: (i % rope_tiles, 0)),
            pl.BlockSpec((tm, LANES), lambda i, j: (i % rope_tiles, 0)),
        ],
        out_specs=[
            pl.BlockSpec((tm, tn), lambda i, j: (i, jnp.minimum(j, n_q_tiles - 1))),
            pl.BlockSpec((tm, tn), lambda i, j: (i, 0)),
        ],
        out_shape=[
            jax.ShapeDtypeStruct((m, qw), BF16),
            jax.ShapeDtypeStruct((m, kvw), F32),
        ],
        scratch_shapes=[pltpu.VMEM((tm, d), BF16)],
        compiler_params=_params(2, 48),
        name="qkv",
    )(x, gain, w_qkv, ones_blockdiag, q_gain, k_gain, cos, sin_signed)


def _sink_softmax_pv(s, sink_col, v):
    m = jnp.maximum(jnp.max(s, axis=-1, keepdims=True), sink_col)
    p = jnp.exp(s - m)
    den = jnp.sum(p, axis=-1, keepdims=True) + jnp.exp(sink_col - m)
    return _dot(p.astype(BF16), v) / den


MASKED = -1e30


def _attn_prompt_body(sinks_ref, q_ref, kv_prev_ref, kv_cur_ref, o_ref):
    n = pl.program_id(1)
    blk = q_ref.shape[0]
    kv = jnp.concatenate([kv_prev_ref[...], kv_cur_ref[...]], axis=0)
    kw = N_KV_HEADS * HEAD_DIM
    r = lax.broadcasted_iota(jnp.int32, (blk, 2 * blk), 0)
    c = lax.broadcasted_iota(jnp.int32, (blk, 2 * blk), 1)
    visible = (c > r + (blk - WINDOW)) & (c <= r + blk) & ((c >= blk) | (n > 0))
    visible = jnp.concatenate([visible] * GROUP, axis=0)
    q = q_ref[...].astype(F32)
    pieces = []
    for g in range(N_KV_HEADS):
        k = kv[:, g * HEAD_DIM:(g + 1) * HEAD_DIM].astype(BF16)
        v = kv[:, kw + g * HEAD_DIM:kw + (g + 1) * HEAD_DIM].astype(BF16)
        heads = range(g * GROUP, (g + 1) * GROUP)
        qg = jnp.concatenate([q[:, h * HEAD_DIM:(h + 1) * HEAD_DIM] for h in heads], axis=0)
        s = lax.dot_general(qg.astype(BF16), k, (((1,), (1,)), ((), ())),
                            preferred_element_type=F32)
        s = jnp.where(visible, s, MASKED)
        sink_col = jnp.concatenate(
            [jnp.full((blk, 1), sinks_ref[h], F32) for h in heads], axis=0)
        o = _sink_softmax_pv(s, sink_col, v)
        pieces += [o[t * blk:(t + 1) * blk] for t in range(GROUP)]
    o_ref[...] = jnp.concatenate(pieces, axis=1).astype(BF16)


def _attn_prompt(sinks, q, kv, *, batch, seq):
    m, qw = q.shape
    kvw = kv.shape[1]
    blk = WINDOW
    nb = seq // blk
    return pl.pallas_call(
        _attn_prompt_body,
        grid_spec=pltpu.PrefetchScalarGridSpec(
            num_scalar_prefetch=1,
            grid=(batch, nb),
            in_specs=[
                pl.BlockSpec((blk, qw), lambda b, n, s: (b * nb + n, 0)),
                pl.BlockSpec((blk, kvw), lambda b, n, s: (b * nb + jnp.maximum(n - 1, 0), 0)),
                pl.BlockSpec((blk, kvw), lambda b, n, s: (b * nb + n, 0)),
            ],
            out_specs=pl.BlockSpec((blk, qw), lambda b, n, s: (b * nb + n, 0)),
        ),
        out_shape=jax.ShapeDtypeStruct((m, qw), BF16),
        compiler_params=_params(2, 32),
        name="attn_prompt",
    )(sinks, q, kv, kv)


def _attn_sample_body(sinks_ref, q_ref, kv_ref, ck_ref, cv_ref, o_ref, *, seq_len):
    nseq, win, kw = ck_ref.shape
    rows = GROUP * seq_len
    q = q_ref[...].astype(F32)
    kv_new = kv_ref[...]
    pad = jnp.zeros((nseq, win - seq_len, HEAD_DIM), F32)
    step = lax.broadcasted_iota(jnp.int32, (1, rows, 2 * win), 1) % seq_len
    c = lax.broadcasted_iota(jnp.int32, (1, rows, 2 * win), 2)
    visible = ((c < win) & (c > step + (win - WINDOW))) | ((c >= win) & (c - win <= step))
    pieces = []
    for g in range(N_KV_HEADS):
        lo, hi = g * HEAD_DIM, (g + 1) * HEAD_DIM
        heads = range(g * GROUP, (g + 1) * GROUP)
        k_new = kv_new[:, lo:hi].reshape(nseq, seq_len, HEAD_DIM)
        v_new = kv_new[:, kw + lo:kw + hi].reshape(nseq, seq_len, HEAD_DIM)
        k = jnp.concatenate([ck_ref[:, :, lo:hi], k_new, pad], axis=1).astype(BF16)
        v = jnp.concatenate([cv_ref[:, :, lo:hi], v_new, pad], axis=1).astype(BF16)
        qg = jnp.concatenate(
            [q[:, h * HEAD_DIM:(h + 1) * HEAD_DIM].reshape(nseq, seq_len, HEAD_DIM)
             for h in heads], axis=1).astype(BF16)
        s = jnp.einsum("bqd,bkd->bqk", qg, k, preferred_element_type=F32)
        s = jnp.where(visible, s, MASKED)
        sink_col = jnp.concatenate(
            [jnp.full((1, seq_len, 1), sinks_ref[h], F32) for h in heads], axis=1)
        m = jnp.maximum(jnp.max(s, axis=-1, keepdims=True), sink_col)
        p = jnp.exp(s - m)
        den = jnp.sum(p, axis=-1, keepdims=True) + jnp.exp(sink_col - m)
        o = jnp.einsum("bqk,bkd->bqd", p.astype(BF16), v, preferred_element_type=F32) / den
        pieces += [o[:, t * seq_len:(t + 1) * seq_len].reshape(nseq * seq_len, HEAD_DIM)
                   for t in range(GROUP)]
    o_ref[...] = jnp.concatenate(pieces, axis=1).astype(BF16)


def _attn_sample(sinks, q, kv, cache_k, cache_v, *, seq_len):
    m, qw = q.shape
    body = functools.partial(_attn_sample_body, seq_len=seq_len)
    full = lambda shape: pl.BlockSpec(shape, lambda i, s: (0,) * len(shape))
    return pl.pallas_call(
        body,
        grid_spec=pltpu.PrefetchScalarGridSpec(
            num_scalar_prefetch=1,
            grid=(1,),
            in_specs=[full(q.shape), full(kv.shape), full(cache_k.shape), full(cache_v.shape)],
            out_specs=full((m, qw)),
        ),
        out_shape=jax.ShapeDtypeStruct((m, qw), BF16),
        compiler_params=_params(1, 48),
        name="attn_sample",
    )(sinks, q, kv, cache_k, cache_v)


def _rope_tables(pos):
    half = HEAD_DIM // 2
    inv = ROPE_THETA ** (-jnp.arange(half, dtype=F32) / half)
    ang = pos.astype(F32)[:, None] * inv[None, :]
    cos, sin = jnp.cos(ang), jnp.sin(ang)
    reps = LANES // HEAD_DIM
    cos = jnp.tile(jnp.concatenate([cos, cos], axis=1), (1, reps))
    sin_signed = jnp.tile(jnp.concatenate([-sin, sin], axis=1), (1, reps))
    return cos, sin_signed


def kernel(x_prompt, x_sample, state_conv, cache_k_win, cache_v_win, ln_mix, ln_mlp,
           w_conv_in, w_conv, w_conv_out, w_qkv, w_attn_out, q_norm, k_norm, sinks,
           w_up, w_down):
    bp, tp, d = x_prompt.shape
    bs, ts, _ = x_sample.shape
    win = cache_k_win.shape[2]
    kw = N_KV_HEADS * HEAD_DIM
    assert ts == SUBLANES and win == WINDOW and tp % WINDOW == 0

    tm_p = 512
    m_s = bs * ts

    xp = x_prompt.reshape(bp * tp, d)
    xs = x_sample.reshape(m_s, d)
    wb = lambda w: w.astype(BF16)

    w_in, w_out = wb(w_conv_in[0]), wb(w_conv_out[0])
    gain = ln_mix[0][None]
    gate_p, u_tail = _conv_in_prompt(xp, gain, w_in, w_conv[0], seq=tp, tm=tm_p, tn=512)
    past = state_conv[0]
    zeros = lambda n: jnp.zeros((bs, n, d), F32)
    p1 = jnp.concatenate([past[:, 1:2], zeros(ts - 1)], axis=1).reshape(m_s, d)
    p2 = jnp.concatenate([past, zeros(ts - 2)], axis=1).reshape(m_s, d)
    gate_s, u_s = _conv_in_sample(xs, gain, w_in, w_conv[0], p1, p2, seq_len=ts, tn=512)
    xp = _proj_residual(xp, gate_p, w_out, tm=tm_p)
    xs = _proj_residual(xs, gate_s, w_out, tm=m_s)
    tiles_per_seq = tp // tm_p
    new_conv_prompt = u_tail[tiles_per_seq - 1::tiles_per_seq, SUBLANES - (CONV_WIDTH - 1):][None]
    new_conv_sample = u_s.reshape(bs, ts, d)[:, ts - (CONV_WIDTH - 1):][None]

    wu, wd = wb(w_up[0]), wb(w_down[0])
    xp = _mlp(xp, ln_mlp[0][None], wu, wd, tm=tm_p, tf=1024)
    xs = _mlp(xs, ln_mlp[0][None], wu, wd, tm=m_s, tf=1024)

    wqkv, wo = wb(w_qkv[0]), wb(w_attn_out[0])
    gain = ln_mix[1][None]
    tn = 2 * kw
    head_id = jnp.arange(tn) // HEAD_DIM
    ones_blockdiag = (head_id[:, None] == head_id[None, :]).astype(BF16)
    q_gain = jnp.tile(q_norm[0], tn // HEAD_DIM)[None]
    k_gain = jnp.tile(k_norm[0], kw // HEAD_DIM)[None]
    cos_p, sin_p = _rope_tables(jnp.arange(tp, dtype=jnp.int32))
    cos_s, sin_s = _rope_tables(PAST_LEN + jnp.arange(ts, dtype=jnp.int32))
    cos_s, sin_s = jnp.tile(cos_s, (bs, 1)), jnp.tile(sin_s, (bs, 1))
    q_p, kv_p = _qkv(xp, gain, wqkv, ones_blockdiag, q_gain, k_gain, cos_p, sin_p,
                     tm=tm_p, rope_tiles=tp // tm_p)
    q_s, kv_s = _qkv(xs, gain, wqkv, ones_blockdiag, q_gain, k_gain, cos_s, sin_s,
                     tm=m_s, rope_tiles=1)
    o_p = _attn_prompt(sinks[0], q_p, kv_p, batch=bp, seq=tp)
    ck = cache_k_win[0].reshape(bs, win, kw)
    cv = cache_v_win[0].reshape(bs, win, kw)
    o_s = _attn_sample(sinks[0], q_s, kv_s, ck, cv, seq_len=ts)
    xp = _proj_residual(xp, o_p, wo, tm=tm_p)
    xs = _proj_residual(xs, o_s, wo, tm=m_s)

    kv_p3 = kv_p.reshape(bp, tp, 2 * kw)[:, tp - WINDOW:]
    new_k_prompt = kv_p3[:, :, :kw].reshape(1, bp, WINDOW, N_KV_HEADS, HEAD_DIM)
    new_v_prompt = kv_p3[:, :, kw:].reshape(1, bp, WINDOW, N_KV_HEADS, HEAD_DIM)
    kv_s3 = kv_s.reshape(bs, ts, 2 * kw)
    new_k_sample = jnp.concatenate([ck[:, ts:], kv_s3[:, :, :kw]], axis=1)
    new_v_sample = jnp.concatenate([cv[:, ts:], kv_s3[:, :, kw:]], axis=1)
    new_k_sample = new_k_sample.reshape(1, bs, win, N_KV_HEADS, HEAD_DIM)
    new_v_sample = new_v_sample.reshape(1, bs, win, N_KV_HEADS, HEAD_DIM)

    wu, wd = wb(w_up[1]), wb(w_down[1])
    xp = _mlp(xp, ln_mlp[1][None], wu, wd, tm=tm_p, tf=1024)
    xs = _mlp(xs, ln_mlp[1][None], wu, wd, tm=m_s, tf=1024)

    return (xp.reshape(bp, tp, d), xs.reshape(bs, ts, d), new_conv_prompt, new_conv_sample,
            new_k_prompt, new_v_prompt, new_k_sample, new_v_sample)
```

```python
import functools

import jax
import jax.numpy as jnp
from jax import lax
from jax.experimental import pallas as pl
from jax.experimental.pallas import tpu as pltpu

F32 = jnp.float32
BF16 = jnp.bfloat16

HEAD_DIM = 64
N_HEADS = 32
N_KV_HEADS = 4
GROUP = N_HEADS // N_KV_HEADS
WINDOW = 128
PAST_LEN = 16384
ROPE_THETA = 10000.0
EPS = 1e-6
CONV_WIDTH = 3

SUBLANES = 8
LANES = 128
MIB = 1024 * 1024
QKV_ROW_CHUNK = 256


def _params(n_grid_dims, vmem_mib):
    return pltpu.CompilerParams(
        dimension_semantics=("arbitrary",) * n_grid_dims,
        vmem_limit_bytes=vmem_mib * MIB,
    )


def _rms_norm(x, gain):
    return x * lax.rsqrt(jnp.mean(x * x, axis=-1, keepdims=True) + EPS) * gain


def _dot(a, b):
    return jnp.dot(a, b, preferred_element_type=F32)


def _mlp_body(x_ref, g_ref, wu_ref, wd_ref, o_ref, h_ref):
    @pl.when(pl.program_id(1) == 0)
    def _():
        x = x_ref[...]
        h_ref[...] = _rms_norm(x, g_ref[...]).astype(BF16)
        o_ref[...] = x

    a = jnp.maximum(_dot(h_ref[...], wu_ref[...]), 0.0)
    o_ref[...] += _dot((a * a).astype(BF16), wd_ref[...])


def _mlp(x, gain, w_up, w_down, layer, *, tm, tf):
    m, d = x.shape
    ff = w_up.shape[2]
    return pl.pallas_call(
        _mlp_body,
        grid=(m // tm, ff // tf),
        in_specs=[
            pl.BlockSpec((tm, d), lambda i, f: (i, 0)),
            pl.BlockSpec((1, d), lambda i, f: (0, 0)),
            pl.BlockSpec((None, d, tf), lambda i, f: (layer, 0, f)),
            pl.BlockSpec((None, tf, d), lambda i, f: (layer, f, 0)),
        ],
        out_specs=pl.BlockSpec((tm, d), lambda i, f: (i, 0)),
        out_shape=jax.ShapeDtypeStruct((m, d), F32),
        scratch_shapes=[pltpu.VMEM((tm, d), BF16)],
        compiler_params=_params(2, 52),
        name="mlp",
    )(x, gain, w_up, w_down)


def _proj_body(x_ref, a_ref, w_ref, o_ref):
    o_ref[...] = x_ref[...] + _dot(a_ref[...], w_ref[...])


def _proj_residual(x, a, w, *, tm):
    m, d = x.shape
    k = a.shape[1]
    return pl.pallas_call(
        _proj_body,
        grid=(m // tm,),
        in_specs=[
            pl.BlockSpec((tm, d), lambda i: (i, 0)),
            pl.BlockSpec((tm, k), lambda i: (i, 0)),
            pl.BlockSpec((k, d), lambda i: (0, 0)),
        ],
        out_specs=pl.BlockSpec((tm, d), lambda i: (i, 0)),
        out_shape=jax.ShapeDtypeStruct((m, d), F32),
        compiler_params=_params(1, 48),
        name="proj_residual",
    )(x, a, w)


def _conv_taps(u, u1, u2, w):
    return w[0:1, :] * u2 + w[1:2, :] * u1 + w[2:3, :] * u


def _conv_in_prompt_body(x_ref, g_ref, wb_ref, wc_ref, wv_ref, wconv_ref,
                         gate_ref, utail_ref, h_ref, carry_ref, *, tiles_per_seq):
    i = pl.program_id(0)
    j = pl.program_id(1)

    @pl.when(j == 0)
    def _():
        h_ref[...] = _rms_norm(x_ref[...], g_ref[...]).astype(BF16)

    @pl.when(i % tiles_per_seq == 0)
    def _():
        carry_ref[j] = jnp.zeros(carry_ref.shape[1:], F32)

    h = h_ref[...]
    u = _dot(h, wc_ref[...]) * _dot(h, wv_ref[...])
    tm = u.shape[0]
    prev = carry_ref[j]
    row8 = lax.broadcasted_iota(jnp.int32, (SUBLANES, 1), 0)
    u1 = pltpu.roll(u, 1, axis=0)
    u2 = pltpu.roll(u, 2, axis=0)
    top1 = jnp.where(row8 < 1, pltpu.roll(prev, 1, axis=0), u1[0:SUBLANES])
    top2 = jnp.where(row8 < 2, pltpu.roll(prev, 2, axis=0), u2[0:SUBLANES])
    u1 = jnp.concatenate([top1, u1[SUBLANES:]], axis=0)
    u2 = jnp.concatenate([top2, u2[SUBLANES:]], axis=0)
    conv = _conv_taps(u, u1, u2, wconv_ref[...])
    tail = u[tm - SUBLANES:]
    carry_ref[j] = tail
    utail_ref[0] = tail
    gate_ref[...] = (_dot(h, wb_ref[...]) * conv).astype(BF16)


def _conv_in_sample_body(x_ref, g_ref, wb_ref, wc_ref, wv_ref, wconv_ref, p1_ref, p2_ref,
                         gate_ref, u_ref, h_ref, *, seq_len):
    j = pl.program_id(1)

    @pl.when(j == 0)
    def _():
        h_ref[...] = _rms_norm(x_ref[...], g_ref[...]).astype(BF16)

    h = h_ref[...]
    u = _dot(h, wc_ref[...]) * _dot(h, wv_ref[...])
    step = lax.broadcasted_iota(jnp.int32, (u.shape[0], 1), 0) % seq_len
    u1 = jnp.where(step < 1, p1_ref[...], pltpu.roll(u, 1, axis=0))
    u2 = jnp.where(step < 2, p2_ref[...], pltpu.roll(u, 2, axis=0))
    conv = _conv_taps(u, u1, u2, wconv_ref[...])
    u_ref[...] = u
    gate_ref[...] = (_dot(h, wb_ref[...]) * conv).astype(BF16)


def _conv_in_specs(tm, d, tn):
    nj = d // tn
    return [
        pl.BlockSpec((tm, d), lambda i, j: (i, 0)),
        pl.BlockSpec((1, d), lambda i, j: (0, 0)),
        pl.BlockSpec((d, tn), lambda i, j: (0, j)),
        pl.BlockSpec((d, tn), lambda i, j: (0, nj + j)),
        pl.BlockSpec((d, tn), lambda i, j: (0, 2 * nj + j)),
        pl.BlockSpec((CONV_WIDTH, tn), lambda i, j: (0, j)),
    ]


def _conv_in_prompt(x, gain, w_in, w_conv, *, seq, tm, tn):
    m, d = x.shape
    body = functools.partial(_conv_in_prompt_body, tiles_per_seq=seq // tm)
    return pl.pallas_call(
        body,
        grid=(m // tm, d // tn),
        in_specs=_conv_in_specs(tm, d, tn),
        out_specs=[
            pl.BlockSpec((tm, tn), lambda i, j: (i, j)),
            pl.BlockSpec((1, SUBLANES, tn), lambda i, j: (i, 0, j)),
        ],
        out_shape=[
            jax.ShapeDtypeStruct((m, d), BF16),
            jax.ShapeDtypeStruct((m // tm, SUBLANES, d), F32),
        ],
        scratch_shapes=[
            pltpu.VMEM((tm, d), BF16),
            pltpu.VMEM((d // tn, SUBLANES, tn), F32),
        ],
        compiler_params=_params(2, 48),
        name="conv_in_prompt",
    )(x, gain, w_in, w_in, w_in, w_conv)


def _conv_in_sample(x, gain, w_in, w_conv, p1, p2, *, seq_len, tn):
    m, d = x.shape
    body = functools.partial(_conv_in_sample_body, seq_len=seq_len)
    return pl.pallas_call(
        body,
        grid=(1, d // tn),
        in_specs=_conv_in_specs(m, d, tn) + [
            pl.BlockSpec((m, tn), lambda i, j: (0, j)),
            pl.BlockSpec((m, tn), lambda i, j: (0, j)),
        ],
        out_specs=[
            pl.BlockSpec((m, tn), lambda i, j: (0, j)),
            pl.BlockSpec((m, tn), lambda i, j: (0, j)),
        ],
        out_shape=[
            jax.ShapeDtypeStruct((m, d), BF16),
            jax.ShapeDtypeStruct((m, d), F32),
        ],
        scratch_shapes=[pltpu.VMEM((m, d), BF16)],
        compiler_params=_params(2, 48),
        name="conv_in_sample",
    )(x, gain, w_in, w_in, w_in, w_conv, p1, p2)


def _head_norm_rope(z, gain, ones_blockdiag, cos, sin_signed, *, split_ssq):
    w = z.shape[1]
    zz = z * z
    hi = zz.astype(BF16)
    ssq = _dot(hi, ones_blockdiag)
    if split_ssq:
        ssq += _dot((zz - hi.astype(F32)).astype(BF16), ones_blockdiag)
    zn = z * lax.rsqrt(ssq * (1.0 / HEAD_DIM) + EPS) * gain
    reps = w // cos.shape[1]
    cos = jnp.concatenate([cos] * reps, axis=1)
    sin_signed = jnp.concatenate([sin_signed] * reps, axis=1)
    lane = lax.broadcasted_iota(jnp.int32, (1, w), 1)
    first_half = (lane % HEAD_DIM) < (HEAD_DIM // 2)
    partner = jnp.where(first_half,
                        pltpu.roll(zn, w - HEAD_DIM // 2, axis=1),
                        pltpu.roll(zn, HEAD_DIM // 2, axis=1))
    return zn * cos + partner * sin_signed


def _qkv_body(x_ref, g_ref, w_ref, ones_ref, qgain_ref, kgain_ref, cos_ref, sin_ref,
              q_ref, kv_ref, h_ref, *, n_q_tiles):
    j = pl.program_id(1)

    @pl.when(j == 0)
    def _():
        h_ref[...] = _rms_norm(x_ref[...], g_ref[...]).astype(BF16)

    tm = h_ref.shape[0]
    chunk = min(tm, QKV_ROW_CHUNK)
    chunks = [pl.ds(c * chunk, chunk) for c in range(tm // chunk)]

    @pl.when(j < n_q_tiles)
    def _():
        for rows in chunks:
            z = _dot(h_ref[rows, :], w_ref[...])
            q = _head_norm_rope(z, qgain_ref[...], ones_ref[...], cos_ref[rows, :],
                                sin_ref[rows, :], split_ssq=False)
            q_ref[rows, :] = (q * HEAD_DIM ** -0.5).astype(BF16)

    @pl.when(j == n_q_tiles)
    def _():
        kw = N_KV_HEADS * HEAD_DIM
        for rows in chunks:
            z = _dot(h_ref[rows, :], w_ref[...])
            k = _head_norm_rope(z[:, :kw], kgain_ref[...], ones_ref[0:kw, 0:kw],
                                cos_ref[rows, :], sin_ref[rows, :], split_ssq=True)
            kv_ref[rows, :] = jnp.concatenate([k, z[:, kw:]], axis=1)


def _qkv(x, gain, w_qkv, ones_blockdiag, q_gain, k_gain, cos, sin_signed, *, tm, rope_tiles):
    m, d = x.shape
    qw = N_HEADS * HEAD_DIM
    kvw = 2 * N_KV_HEADS * HEAD_DIM
    tn = kvw
    n_q_tiles = qw // tn
    body = functools.partial(_qkv_body, n_q_tiles=n_q_tiles)
    return pl.pallas_call(
        body,
        grid=(m // tm, n_q_tiles + 1),
        in_specs=[
            pl.BlockSpec((tm, d), lambda i, j: (i, 0)),
            pl.BlockSpec((1, d), lambda i, j: (0, 0)),
            pl.BlockSpec((d, tn), lambda i, j: (0, j)),
            pl.BlockSpec((tn, tn), lambda i, j: (0, 0)),
            pl.BlockSpec((1, tn), lambda i, j: (0, 0)),
            pl.BlockSpec((1, tn // 2), lambda i, j: (0, 0)),
            pl.BlockSpec((tm, LANES), lambda i, j: (i % rope_tiles, 0)),
            pl.BlockSpec((tm, LANES), lambda i, j: (i % rope_tiles, 0)),
        ],
        out_specs=[
            pl.BlockSpec((tm, tn), lambda i, j: (i, jnp.minimum(j, n_q_tiles - 1))),
            pl.BlockSpec((tm, tn), lambda i, j: (i, 0)),
        ],
        out_shape=[
            jax.ShapeDtypeStruct((m, qw), BF16),
            jax.ShapeDtypeStruct((m, kvw), F32),
        ],
        scratch_shapes=[pltpu.VMEM((tm, d), BF16)],
        compiler_params=_params(2, 48),
        name="qkv",
    )(x, gain, w_qkv, ones_blockdiag, q_gain, k_gain, cos, sin_signed)


MASKED = -1e30


def _attn_prompt_body(sinks_ref, q_ref, kv_prev_ref, kv_cur_ref, o_ref):
    n = pl.program_id(1)
    blk = q_ref.shape[0]
    kv = jnp.concatenate([kv_prev_ref[...], kv_cur_ref[...]], axis=0)
    kw = N_KV_HEADS * HEAD_DIM
    r = lax.broadcasted_iota(jnp.int32, (blk, 2 * blk), 0)
    c = lax.broadcasted_iota(jnp.int32, (blk, 2 * blk), 1)
    visible = (c > r + (blk - WINDOW)) & (c <= r + blk) & ((c >= blk) | (n > 0))
    lower = lax.broadcasted_iota(jnp.int32, (1, LANES), 1) < HEAD_DIM
    heads_per_col = LANES // HEAD_DIM
    for g in range(N_KV_HEADS):
        col, half = divmod(g, heads_per_col)
        kcol = kv[:, col * LANES:(col + 1) * LANES]
        vcol = kv[:, kw + col * LANES:kw + (col + 1) * LANES]
        kswap = pltpu.roll(kcol, HEAD_DIM, axis=1)
        vswap = pltpu.roll(vcol, HEAD_DIM, axis=1)
        in_lower = (kcol, vcol) if half == 0 else (kswap, vswap)
        in_upper = (kswap, vswap) if half == 0 else (kcol, vcol)
        k_lo, v_lo = (jnp.where(lower, t, 0.0).astype(BF16) for t in in_lower)
        k_hi, v_hi = (jnp.where(lower, 0.0, t).astype(BF16) for t in in_upper)
        for pair in range(GROUP // heads_per_col):
            h0 = g * GROUP + pair * heads_per_col
            lanes = pl.ds(h0 * HEAD_DIM, LANES)
            q_pair = q_ref[:, lanes]
            acc = None
            for h, k, v in ((h0, k_lo, v_lo), (h0 + 1, k_hi, v_hi)):
                s = lax.dot_general(q_pair, k, (((1,), (1,)), ((), ())),
                                    preferred_element_type=F32)
                s = jnp.where(visible, s, MASKED)
                sink = sinks_ref[h]
                m = jnp.maximum(jnp.max(s, axis=-1, keepdims=True), sink)
                p = jnp.exp(s - m)
                den = jnp.sum(p, axis=-1, keepdims=True) + jnp.exp(sink - m)
                o = _dot(p.astype(BF16), v) * (1.0 / den)
                acc = o if acc is None else acc + o
            o_ref[:, lanes] = acc.astype(BF16)


def _attn_prompt(sinks, q, kv, *, batch, seq):
    m, qw = q.shape
    kvw = kv.shape[1]
    blk = WINDOW
    nb = seq // blk
    return pl.pallas_call(
        _attn_prompt_body,
        grid_spec=pltpu.PrefetchScalarGridSpec(
            num_scalar_prefetch=1,
            grid=(batch, nb),
            in_specs=[
                pl.BlockSpec((blk, qw), lambda b, n, s: (b * nb + n, 0)),
                pl.BlockSpec((blk, kvw), lambda b, n, s: (b * nb + jnp.maximum(n - 1, 0), 0)),
                pl.BlockSpec((blk, kvw), lambda b, n, s: (b * nb + n, 0)),
            ],
            out_specs=pl.BlockSpec((blk, qw), lambda b, n, s: (b * nb + n, 0)),
        ),
        out_shape=jax.ShapeDtypeStruct((m, qw), BF16),
        compiler_params=_params(2, 32),
        name="attn_prompt",
    )(sinks, q, kv, kv)


def _attn_sample_body(sinks_ref, q_ref, kv_ref, ck_ref, cv_ref, o_ref, *, seq_len):
    nseq, win, kw = ck_ref.shape
    rows = GROUP * seq_len
    q = q_ref[...].astype(F32)
    kv_new = kv_ref[...]
    pad = jnp.zeros((nseq, win - seq_len, HEAD_DIM), F32)
    step = lax.broadcasted_iota(jnp.int32, (1, rows, 2 * win), 1) % seq_len
    c = lax.broadcasted_iota(jnp.int32, (1, rows, 2 * win), 2)
    visible = ((c < win) & (c > step + (win - WINDOW))) | ((c >= win) & (c - win <= step))
    pieces = []
    for g in range(N_KV_HEADS):
        lo, hi = g * HEAD_DIM, (g + 1) * HEAD_DIM
        heads = range(g * GROUP, (g + 1) * GROUP)
        k_new = kv_new[:, lo:hi].reshape(nseq, seq_len, HEAD_DIM)
        v_new = kv_new[:, kw + lo:kw + hi].reshape(nseq, seq_len, HEAD_DIM)
        k = jnp.concatenate([ck_ref[:, :, lo:hi], k_new, pad], axis=1).astype(BF16)
        v = jnp.concatenate([cv_ref[:, :, lo:hi], v_new, pad], axis=1).astype(BF16)
        qg = jnp.concatenate(
            [q[:, h * HEAD_DIM:(h + 1) * HEAD_DIM].reshape(nseq, seq_len, HEAD_DIM)
             for h in heads], axis=1).astype(BF16)
        s = jnp.einsum("bqd,bkd->bqk", qg, k, preferred_element_type=F32)
        s = jnp.where(visible, s, MASKED)
        sink_col = jnp.concatenate(
            [jnp.full((1, seq_len, 1), sinks_ref[h], F32) for h in heads], axis=1)
        m = jnp.maximum(jnp.max(s, axis=-1, keepdims=True), sink_col)
        p = jnp.exp(s - m)
        den = jnp.sum(p, axis=-1, keepdims=True) + jnp.exp(sink_col - m)
        o = jnp.einsum("bqk,bkd->bqd", p.astype(BF16), v, preferred_element_type=F32) / den
        pieces += [o[:, t * seq_len:(t + 1) * seq_len].reshape(nseq * seq_len, HEAD_DIM)
                   for t in range(GROUP)]
    o_ref[...] = jnp.concatenate(pieces, axis=1).astype(BF16)


def _attn_sample(sinks, q, kv, cache_k, cache_v, *, seq_len):
    m, qw = q.shape
    body = functools.partial(_attn_sample_body, seq_len=seq_len)
    full = lambda shape: pl.BlockSpec(shape, lambda i, s: (0,) * len(shape))
    return pl.pallas_call(
        body,
        grid_spec=pltpu.PrefetchScalarGridSpec(
            num_scalar_prefetch=1,
            grid=(1,),
            in_specs=[full(q.shape), full(kv.shape), full(cache_k.shape), full(cache_v.shape)],
            out_specs=full((m, qw)),
        ),
        out_shape=jax.ShapeDtypeStruct((m, qw), BF16),
        compiler_params=_params(1, 48),
        name="attn_sample",
    )(sinks, q, kv, cache_k, cache_v)


def _rope_tables(pos):
    half = HEAD_DIM // 2
    inv = ROPE_THETA ** (-jnp.arange(half, dtype=F32) / half)
    ang = pos.astype(F32)[:, None] * inv[None, :]
    cos, sin = jnp.cos(ang), jnp.sin(ang)
    reps = LANES // HEAD_DIM
    cos = jnp.tile(jnp.concatenate([cos, cos], axis=1), (1, reps))
    sin_signed = jnp.tile(jnp.concatenate([-sin, sin], axis=1), (1, reps))
    return cos, sin_signed


def kernel(x_prompt, x_sample, state_conv, cache_k_win, cache_v_win, ln_mix, ln_mlp,
           w_conv_in, w_conv, w_conv_out, w_qkv, w_attn_out, q_norm, k_norm, sinks,
           w_up, w_down):
    bp, tp, d = x_prompt.shape
    bs, ts, _ = x_sample.shape
    win = cache_k_win.shape[2]
    kw = N_KV_HEADS * HEAD_DIM
    assert ts == SUBLANES and win == WINDOW and tp % WINDOW == 0

    tm_p = 512
    m_s = bs * ts

    xp = x_prompt.reshape(bp * tp, d)
    xs = x_sample.reshape(m_s, d)
    wb = lambda w: w.astype(BF16)

    w_in, w_out = wb(w_conv_in[0]), wb(w_conv_out[0])
    gain = ln_mix[0][None]
    gate_p, u_tail = _conv_in_prompt(xp, gain, w_in, w_conv[0], seq=tp, tm=tm_p, tn=512)
    past = state_conv[0]
    zeros = lambda n: jnp.zeros((bs, n, d), F32)
    p1 = jnp.concatenate([past[:, 1:2], zeros(ts - 1)], axis=1).reshape(m_s, d)
    p2 = jnp.concatenate([past, zeros(ts - 2)], axis=1).reshape(m_s, d)
    gate_s, u_s = _conv_in_sample(xs, gain, w_in, w_conv[0], p1, p2, seq_len=ts, tn=512)
    xp = _proj_residual(xp, gate_p, w_out, tm=tm_p)
    xs = _proj_residual(xs, gate_s, w_out, tm=m_s)
    tiles_per_seq = tp // tm_p
    new_conv_prompt = u_tail[tiles_per_seq - 1::tiles_per_seq, SUBLANES - (CONV_WIDTH - 1):][None]
    new_conv_sample = u_s.reshape(bs, ts, d)[:, ts - (CONV_WIDTH - 1):][None]

    wu, wd = wb(w_up), wb(w_down)
    xp = _mlp(xp, ln_mlp[0][None], wu, wd, 0, tm=tm_p, tf=1024)
    xs = _mlp(xs, ln_mlp[0][None], wu, wd, 0, tm=m_s, tf=1024)

    wqkv, wo = wb(w_qkv[0]), wb(w_attn_out[0])
    gain = ln_mix[1][None]
    tn = 2 * kw
    head_id = jnp.arange(tn) // HEAD_DIM
    ones_blockdiag = (head_id[:, None] == head_id[None, :]).astype(BF16)
    q_gain = jnp.tile(q_norm[0], tn // HEAD_DIM)[None]
    k_gain = jnp.tile(k_norm[0], kw // HEAD_DIM)[None]
    cos_p, sin_p = _rope_tables(jnp.arange(tp, dtype=jnp.int32))
    cos_s, sin_s = _rope_tables(PAST_LEN + jnp.arange(ts, dtype=jnp.int32))
    cos_s, sin_s = jnp.tile(cos_s, (bs, 1)), jnp.tile(sin_s, (bs, 1))
    q_p, kv_p = _qkv(xp, gain, wqkv, ones_blockdiag, q_gain, k_gain, cos_p, sin_p,
                     tm=tm_p, rope_tiles=tp // tm_p)
    q_s, kv_s = _qkv(xs, gain, wqkv, ones_blockdiag, q_gain, k_gain, cos_s, sin_s,
                     tm=m_s, rope_tiles=1)
    o_p = _attn_prompt(sinks[0], q_p, kv_p, batch=bp, seq=tp)
    ck = cache_k_win[0].reshape(bs, win, kw)
    cv = cache_v_win[0].reshape(bs, win, kw)
    o_s = _attn_sample(sinks[0], q_s, kv_s, ck, cv, seq_len=ts)
    xp = _proj_residual(xp, o_p, wo, tm=tm_p)
    xs = _proj_residual(xs, o_s, wo, tm=m_s)

    kv_p3 = kv_p.reshape(bp, tp, 2 * kw)[:, tp - WINDOW:]
    new_k_prompt = kv_p3[:, :, :kw].reshape(1, bp, WINDOW, N_KV_HEADS, HEAD_DIM)
    new_v_prompt = kv_p3[:, :, kw:].reshape(1, bp, WINDOW, N_KV_HEADS, HEAD_DIM)
    kv_s3 = kv_s.reshape(bs, ts, 2 * kw)
    new_k_sample = jnp.concatenate([ck[:, ts:], kv_s3[:, :, :kw]], axis=1)
    new_v_sample = jnp.concatenate([cv[:, ts:], kv_s3[:, :, kw:]], axis=1)
    new_k_sample = new_k_sample.reshape(1, bs, win, N_KV_HEADS, HEAD_DIM)
    new_v_sample = new_v_sample.reshape(1, bs, win, N_KV_HEADS, HEAD_DIM)

    xp = _mlp(xp, ln_mlp[1][None], wu, wd, 1, tm=tm_p, tf=1024)
    xs = _mlp(xs, ln_mlp[1][None], wu, wd, 1, tm=m_s, tf=1024)

    return (xp.reshape(bp, tp, d), xs.reshape(bs, ts, d), new_conv_prompt, new_conv_sample,
            new_k_prompt, new_v_prompt, new_k_sample, new_v_sample)
```

```python
import functools

import jax
import jax.numpy as jnp
from jax import lax
from jax.experimental import pallas as pl
from jax.experimental.pallas import tpu as pltpu

F32 = jnp.float32
BF16 = jnp.bfloat16

HEAD_DIM = 64
N_HEADS = 32
N_KV_HEADS = 4
GROUP = N_HEADS // N_KV_HEADS
WINDOW = 128
PAST_LEN = 16384
ROPE_THETA = 10000.0
EPS = 1e-6
CONV_WIDTH = 3

SUBLANES = 8
LANES = 128
MIB = 1024 * 1024
QKV_ROW_CHUNK = 256


def _params(n_grid_dims, vmem_mib):
    return pltpu.CompilerParams(
        dimension_semantics=("arbitrary",) * n_grid_dims,
        vmem_limit_bytes=vmem_mib * MIB,
    )


def _rms_norm(x, gain):
    return x * lax.rsqrt(jnp.mean(x * x, axis=-1, keepdims=True) + EPS) * gain


def _dot(a, b):
    return jnp.dot(a, b, preferred_element_type=F32)


def _cast_specs(weights, grid):
    n_steps = 1
    for g in grid:
        n_steps *= g

    def step_of(*ids):
        step = ids[0]
        for g, idx in zip(grid[1:], ids[1:]):
            step = step * g + idx
        return step

    in_specs, out_specs, out_shapes = [], [], []
    for w, layer in weights:
        rows, cols = w.shape[1:]
        assert rows % (n_steps * 2 * SUBLANES) == 0, (w.shape, n_steps)
        rb = rows // n_steps
        in_specs.append(pl.BlockSpec((None, rb, cols),
                                     lambda *ids, layer=layer: (layer, step_of(*ids), 0)))
        out_specs.append(pl.BlockSpec((rb, cols), lambda *ids: (step_of(*ids), 0)))
        out_shapes.append(jax.ShapeDtypeStruct((rows, cols), BF16))
    return in_specs, out_specs, out_shapes


def _cast_blocks(src_refs, dst_refs):
    for src, dst in zip(src_refs, dst_refs):
        dst[...] = src[...].astype(BF16)


def _mlp_body(x_ref, g_ref, wu_ref, wd_ref, *rest, n_cast):
    cast_src, (o_ref,), cast_dst, (h_ref,) = (
        rest[:n_cast], rest[n_cast:n_cast + 1], rest[n_cast + 1:2 * n_cast + 1],
        rest[2 * n_cast + 1:])

    @pl.when(pl.program_id(1) == 0)
    def _():
        x = x_ref[...]
        h_ref[...] = _rms_norm(x, g_ref[...]).astype(BF16)
        o_ref[...] = x

    a = jnp.maximum(_dot(h_ref[...], wu_ref[...]), 0.0)
    o_ref[...] += _dot((a * a).astype(BF16), wd_ref[...])
    _cast_blocks(cast_src, cast_dst)


def _mlp(x, gain, w_up, w_down, *, tm, tf, cast=()):
    m, d = x.shape
    ff = w_up.shape[1]
    grid = (m // tm, ff // tf)
    cast_in, cast_out, cast_shapes = _cast_specs(cast, grid)
    return pl.pallas_call(
        functools.partial(_mlp_body, n_cast=len(cast)),
        grid=grid,
        in_specs=[
            pl.BlockSpec((tm, d), lambda i, f: (i, 0)),
            pl.BlockSpec((1, d), lambda i, f: (0, 0)),
            pl.BlockSpec((d, tf), lambda i, f: (0, f)),
            pl.BlockSpec((tf, d), lambda i, f: (f, 0)),
        ] + cast_in,
        out_specs=[pl.BlockSpec((tm, d), lambda i, f: (i, 0))] + cast_out,
        out_shape=[jax.ShapeDtypeStruct((m, d), F32)] + cast_shapes,
        scratch_shapes=[pltpu.VMEM((tm, d), BF16)],
        compiler_params=_params(2, 52),
        name="mlp",
    )(x, gain, w_up, w_down, *[w for w, _ in cast])


def _proj_body(x_ref, a_ref, w_ref, o_ref):
    o_ref[...] = x_ref[...] + _dot(a_ref[...], w_ref[...])


def _proj_residual(x, a, w, *, tm):
    m, d = x.shape
    k = a.shape[1]
    return pl.pallas_call(
        _proj_body,
        grid=(m // tm,),
        in_specs=[
            pl.BlockSpec((tm, d), lambda i: (i, 0)),
            pl.BlockSpec((tm, k), lambda i: (i, 0)),
            pl.BlockSpec((k, d), lambda i: (0, 0)),
        ],
        out_specs=pl.BlockSpec((tm, d), lambda i: (i, 0)),
        out_shape=jax.ShapeDtypeStruct((m, d), F32),
        compiler_params=_params(1, 48),
        name="proj_residual",
    )(x, a, w)


def _conv_taps(u, u1, u2, w):
    return w[0:1, :] * u2 + w[1:2, :] * u1 + w[2:3, :] * u


def _conv_in_prompt_body(x_ref, g_ref, wb_ref, wc_ref, wv_ref, wconv_ref, *rest,
                         tiles_per_seq, n_cast):
    cast_src, (gate_ref, utail_ref), cast_dst, (h_ref, carry_ref) = (
        rest[:n_cast], rest[n_cast:n_cast + 2], rest[n_cast + 2:2 * n_cast + 2],
        rest[2 * n_cast + 2:])
    _cast_blocks(cast_src, cast_dst)
    i = pl.program_id(0)
    j = pl.program_id(1)

    @pl.when(j == 0)
    def _():
        h_ref[...] = _rms_norm(x_ref[...], g_ref[...]).astype(BF16)

    @pl.when(i % tiles_per_seq == 0)
    def _():
        carry_ref[j] = jnp.zeros(carry_ref.shape[1:], F32)

    h = h_ref[...]
    u = _dot(h, wc_ref[...]) * _dot(h, wv_ref[...])
    tm = u.shape[0]
    prev = carry_ref[j]
    row8 = lax.broadcasted_iota(jnp.int32, (SUBLANES, 1), 0)
    u1 = pltpu.roll(u, 1, axis=0)
    u2 = pltpu.roll(u, 2, axis=0)
    top1 = jnp.where(row8 < 1, pltpu.roll(prev, 1, axis=0), u1[0:SUBLANES])
    top2 = jnp.where(row8 < 2, pltpu.roll(prev, 2, axis=0), u2[0:SUBLANES])
    u1 = jnp.concatenate([top1, u1[SUBLANES:]], axis=0)
    u2 = jnp.concatenate([top2, u2[SUBLANES:]], axis=0)
    conv = _conv_taps(u, u1, u2, wconv_ref[...])
    tail = u[tm - SUBLANES:]
    carry_ref[j] = tail
    utail_ref[0] = tail
    gate_ref[...] = (_dot(h, wb_ref[...]) * conv).astype(BF16)


def _conv_in_sample_body(x_ref, g_ref, wb_ref, wc_ref, wv_ref, wconv_ref, p1_ref, p2_ref,
                         gate_ref, u_ref, h_ref, *, seq_len):
    j = pl.program_id(1)

    @pl.when(j == 0)
    def _():
        h_ref[...] = _rms_norm(x_ref[...], g_ref[...]).astype(BF16)

    h = h_ref[...]
    u = _dot(h, wc_ref[...]) * _dot(h, wv_ref[...])
    step = lax.broadcasted_iota(jnp.int32, (u.shape[0], 1), 0) % seq_len
    u1 = jnp.where(step < 1, p1_ref[...], pltpu.roll(u, 1, axis=0))
    u2 = jnp.where(step < 2, p2_ref[...], pltpu.roll(u, 2, axis=0))
    conv = _conv_taps(u, u1, u2, wconv_ref[...])
    u_ref[...] = u
    gate_ref[...] = (_dot(h, wb_ref[...]) * conv).astype(BF16)


def _conv_in_specs(tm, d, tn):
    nj = d // tn
    return [
        pl.BlockSpec((tm, d), lambda i, j: (i, 0)),
        pl.BlockSpec((1, d), lambda i, j: (0, 0)),
        pl.BlockSpec((d, tn), lambda i, j: (0, j)),
        pl.BlockSpec((d, tn), lambda i, j: (0, nj + j)),
        pl.BlockSpec((d, tn), lambda i, j: (0, 2 * nj + j)),
        pl.BlockSpec((CONV_WIDTH, tn), lambda i, j: (0, j)),
    ]


def _conv_in_prompt(x, gain, w_in, w_conv, *, seq, tm, tn, cast=()):
    m, d = x.shape
    grid = (m // tm, d // tn)
    cast_in, cast_out, cast_shapes = _cast_specs(cast, grid)
    body = functools.partial(_conv_in_prompt_body, tiles_per_seq=seq // tm, n_cast=len(cast))
    return pl.pallas_call(
        body,
        grid=grid,
        in_specs=_conv_in_specs(tm, d, tn) + cast_in,
        out_specs=[
            pl.BlockSpec((tm, tn), lambda i, j: (i, j)),
            pl.BlockSpec((1, SUBLANES, tn), lambda i, j: (i, 0, j)),
        ] + cast_out,
        out_shape=[
            jax.ShapeDtypeStruct((m, d), BF16),
            jax.ShapeDtypeStruct((m // tm, SUBLANES, d), F32),
        ] + cast_shapes,
        scratch_shapes=[
            pltpu.VMEM((tm, d), BF16),
            pltpu.VMEM((d // tn, SUBLANES, tn), F32),
        ],
        compiler_params=_params(2, 52),
        name="conv_in_prompt",
    )(x, gain, w_in, w_in, w_in, w_conv, *[w for w, _ in cast])


def _conv_in_sample(x, gain, w_in, w_conv, p1, p2, *, seq_len, tn):
    m, d = x.shape
    body = functools.partial(_conv_in_sample_body, seq_len=seq_len)
    return pl.pallas_call(
        body,
        grid=(1, d // tn),
        in_specs=_conv_in_specs(m, d, tn) + [
            pl.BlockSpec((m, tn), lambda i, j: (0, j)),
            pl.BlockSpec((m, tn), lambda i, j: (0, j)),
        ],
        out_specs=[
            pl.BlockSpec((m, tn), lambda i, j: (0, j)),
            pl.BlockSpec((m, tn), lambda i, j: (0, j)),
        ],
        out_shape=[
            jax.ShapeDtypeStruct((m, d), BF16),
            jax.ShapeDtypeStruct((m, d), F32),
        ],
        scratch_shapes=[pltpu.VMEM((m, d), BF16)],
        compiler_params=_params(2, 48),
        name="conv_in_sample",
    )(x, gain, w_in, w_in, w_in, w_conv, p1, p2)


def _head_norm_rope(z, gain, ones_blockdiag, cos, sin_signed, *, split_ssq):
    w = z.shape[1]
    zz = z * z
    hi = zz.astype(BF16)
    ssq = _dot(hi, ones_blockdiag)
    if split_ssq:
        ssq += _dot((zz - hi.astype(F32)).astype(BF16), ones_blockdiag)
    zn = z * lax.rsqrt(ssq * (1.0 / HEAD_DIM) + EPS) * gain
    reps = w // cos.shape[1]
    cos = jnp.concatenate([cos] * reps, axis=1)
    sin_signed = jnp.concatenate([sin_signed] * reps, axis=1)
    lane = lax.broadcasted_iota(jnp.int32, (1, w), 1)
    first_half = (lane % HEAD_DIM) < (HEAD_DIM // 2)
    partner = jnp.where(first_half,
                        pltpu.roll(zn, w - HEAD_DIM // 2, axis=1),
                        pltpu.roll(zn, HEAD_DIM // 2, axis=1))
    return zn * cos + partner * sin_signed


def _qkv_body(x_ref, g_ref, w_ref, ones_ref, qgain_ref, kgain_ref, cos_ref, sin_ref,
              q_ref, kv_ref):
    tm = x_ref.shape[0]
    qw = q_ref.shape[1]
    tn = ones_ref.shape[0]
    kw = N_KV_HEADS * HEAD_DIM
    chunk = min(tm, QKV_ROW_CHUNK)
    for c in range(tm // chunk):
        rows = pl.ds(c * chunk, chunk)
        h = _rms_norm(x_ref[rows, :], g_ref[...]).astype(BF16)
        cos, sin_signed = cos_ref[rows, :], sin_ref[rows, :]
        for t in range(qw // tn):
            cols = pl.ds(t * tn, tn)
            z = _dot(h, w_ref[:, cols])
            q = _head_norm_rope(z, qgain_ref[...], ones_ref[...], cos, sin_signed,
                                split_ssq=False)
            q_ref[rows, cols] = (q * HEAD_DIM ** -0.5).astype(BF16)
        z = _dot(h, w_ref[:, pl.ds(qw, 2 * kw)])
        k = _head_norm_rope(z[:, :kw], kgain_ref[...], ones_ref[0:kw, 0:kw], cos, sin_signed,
                            split_ssq=True)
        kv_ref[rows, :] = jnp.concatenate([k, z[:, kw:]], axis=1)


def _qkv(x, gain, w_qkv, ones_blockdiag, q_gain, k_gain, cos, sin_signed, *, tm, rope_tiles):
    m, d = x.shape
    qw = N_HEADS * HEAD_DIM
    kvw = 2 * N_KV_HEADS * HEAD_DIM
    tn = ones_blockdiag.shape[0]
    const = lambda shape: pl.BlockSpec(shape, lambda i: (0, 0), pipeline_mode=pl.Buffered(1))
    return pl.pallas_call(
        _qkv_body,
        grid=(m // tm,),
        in_specs=[
            pl.BlockSpec((tm, d), lambda i: (i, 0)),
            const((1, d)),
            const(w_qkv.shape),
            const((tn, tn)),
            const((1, tn)),
            const((1, kvw // 2)),
            pl.BlockSpec((tm, LANES), lambda i: (i % rope_tiles, 0)),
            pl.BlockSpec((tm, LANES), lambda i: (i % rope_tiles, 0)),
        ],
        out_specs=[
            pl.BlockSpec((tm, qw), lambda i: (i, 0)),
            pl.BlockSpec((tm, kvw), lambda i: (i, 0)),
        ],
        out_shape=[
            jax.ShapeDtypeStruct((m, qw), BF16),
            jax.ShapeDtypeStruct((m, kvw), F32),
        ],
        compiler_params=_params(1, 48),
        name="qkv",
    )(x, gain, w_qkv, ones_blockdiag, q_gain, k_gain, cos, sin_signed)


MASKED = -1e30


def _attn_prompt_body(sinks_ref, q_ref, kv_prev_ref, kv_cur_ref, o_ref):
    n = pl.program_id(1)
    blk = q_ref.shape[0]
    kv = jnp.concatenate([kv_prev_ref[...], kv_cur_ref[...]], axis=0)
    kw = N_KV_HEADS * HEAD_DIM
    r = lax.broadcasted_iota(jnp.int32, (blk, 2 * blk), 0)
    c = lax.broadcasted_iota(jnp.int32, (blk, 2 * blk), 1)
    visible = (c > r + (blk - WINDOW)) & (c <= r + blk) & ((c >= blk) | (n > 0))
    lower = lax.broadcasted_iota(jnp.int32, (1, LANES), 1) < HEAD_DIM
    heads_per_col = LANES // HEAD_DIM
    for g in range(N_KV_HEADS):
        col, half = divmod(g, heads_per_col)
        kcol = kv[:, col * LANES:(col + 1) * LANES]
        vcol = kv[:, kw + col * LANES:kw + (col + 1) * LANES]
        kswap = pltpu.roll(kcol, HEAD_DIM, axis=1)
        vswap = pltpu.roll(vcol, HEAD_DIM, axis=1)
        in_lower = (kcol, vcol) if half == 0 else (kswap, vswap)
        in_upper = (kswap, vswap) if half == 0 else (kcol, vcol)
        k_lo, v_lo = (jnp.where(lower, t, 0.0).astype(BF16) for t in in_lower)
        k_hi, v_hi = (jnp.where(lower, 0.0, t).astype(BF16) for t in in_upper)
        for pair in range(GROUP // heads_per_col):
            h0 = g * GROUP + pair * heads_per_col
            lanes = pl.ds(h0 * HEAD_DIM, LANES)
            q_pair = q_ref[:, lanes]
            acc = None
            for h, k, v in ((h0, k_lo, v_lo), (h0 + 1, k_hi, v_hi)):
                s = lax.dot_general(q_pair, k, (((1,), (1,)), ((), ())),
                                    preferred_element_type=F32)
                s = jnp.where(visible, s, MASKED)
                sink = sinks_ref[h]
                m = jnp.maximum(jnp.max(s, axis=-1, keepdims=True), sink)
                p = jnp.exp(s - m)
                den = jnp.sum(p, axis=-1, keepdims=True) + jnp.exp(sink - m)
                o = _dot(p.astype(BF16), v) * (1.0 / den)
                acc = o if acc is None else acc + o
            o_ref[:, lanes] = acc.astype(BF16)


def _attn_prompt(sinks, q, kv, *, batch, seq):
    m, qw = q.shape
    kvw = kv.shape[1]
    blk = WINDOW
    nb = seq // blk
    return pl.pallas_call(
        _attn_prompt_body,
        grid_spec=pltpu.PrefetchScalarGridSpec(
            num_scalar_prefetch=1,
            grid=(batch, nb),
            in_specs=[
                pl.BlockSpec((blk, qw), lambda b, n, s: (b * nb + n, 0)),
                pl.BlockSpec((blk, kvw), lambda b, n, s: (b * nb + jnp.maximum(n - 1, 0), 0)),
                pl.BlockSpec((blk, kvw), lambda b, n, s: (b * nb + n, 0)),
            ],
            out_specs=pl.BlockSpec((blk, qw), lambda b, n, s: (b * nb + n, 0)),
        ),
        out_shape=jax.ShapeDtypeStruct((m, qw), BF16),
        compiler_params=_params(2, 32),
        name="attn_prompt",
    )(sinks, q, kv, kv)


def _attn_sample_body(sinks_ref, q_ref, kv_ref, ck_ref, cv_ref, o_ref, *, seq_len):
    nseq, win, kw = ck_ref.shape
    rows = GROUP * seq_len
    q = q_ref[...].astype(F32)
    kv_new = kv_ref[...]
    pad = jnp.zeros((nseq, win - seq_len, HEAD_DIM), F32)
    step = lax.broadcasted_iota(jnp.int32, (1, rows, 2 * win), 1) % seq_len
    c = lax.broadcasted_iota(jnp.int32, (1, rows, 2 * win), 2)
    visible = ((c < win) & (c > step + (win - WINDOW))) | ((c >= win) & (c - win <= step))
    pieces = []
    for g in range(N_KV_HEADS):
        lo, hi = g * HEAD_DIM, (g + 1) * HEAD_DIM
        heads = range(g * GROUP, (g + 1) * GROUP)
        k_new = kv_new[:, lo:hi].reshape(nseq, seq_len, HEAD_DIM)
        v_new = kv_new[:, kw + lo:kw + hi].reshape(nseq, seq_len, HEAD_DIM)
        k = jnp.concatenate([ck_ref[:, :, lo:hi], k_new, pad], axis=1).astype(BF16)
        v = jnp.concatenate([cv_ref[:, :, lo:hi], v_new, pad], axis=1).astype(BF16)
        qg = jnp.concatenate(
            [q[:, h * HEAD_DIM:(h + 1) * HEAD_DIM].reshape(nseq, seq_len, HEAD_DIM)
             for h in heads], axis=1).astype(BF16)
        s = jnp.einsum("bqd,bkd->bqk", qg, k, preferred_element_type=F32)
        s = jnp.where(visible, s, MASKED)
        sink_col = jnp.concatenate(
            [jnp.full((1, seq_len, 1), sinks_ref[h], F32) for h in heads], axis=1)
        m = jnp.maximum(jnp.max(s, axis=-1, keepdims=True), sink_col)
        p = jnp.exp(s - m)
        den = jnp.sum(p, axis=-1, keepdims=True) + jnp.exp(sink_col - m)
        o = jnp.einsum("bqk,bkd->bqd", p.astype(BF16), v, preferred_element_type=F32) / den
        pieces += [o[:, t * seq_len:(t + 1) * seq_len].reshape(nseq * seq_len, HEAD_DIM)
                   for t in range(GROUP)]
    o_ref[...] = jnp.concatenate(pieces, axis=1).astype(BF16)


def _attn_sample(sinks, q, kv, cache_k, cache_v, *, seq_len):
    m, qw = q.shape
    body = functools.partial(_attn_sample_body, seq_len=seq_len)
    full = lambda shape: pl.BlockSpec(shape, lambda i, s: (0,) * len(shape))
    return pl.pallas_call(
        body,
        grid_spec=pltpu.PrefetchScalarGridSpec(
            num_scalar_prefetch=1,
            grid=(1,),
            in_specs=[full(q.shape), full(kv.shape), full(cache_k.shape), full(cache_v.shape)],
            out_specs=full((m, qw)),
        ),
        out_shape=jax.ShapeDtypeStruct((m, qw), BF16),
        compiler_params=_params(1, 48),
        name="attn_sample",
    )(sinks, q, kv, cache_k, cache_v)


def _rope_tables(pos):
    half = HEAD_DIM // 2
    inv = ROPE_THETA ** (-jnp.arange(half, dtype=F32) / half)
    ang = pos.astype(F32)[:, None] * inv[None, :]
    cos, sin = jnp.cos(ang), jnp.sin(ang)
    reps = LANES // HEAD_DIM
    cos = jnp.tile(jnp.concatenate([cos, cos], axis=1), (1, reps))
    sin_signed = jnp.tile(jnp.concatenate([-sin, sin], axis=1), (1, reps))
    return cos, sin_signed


def kernel(x_prompt, x_sample, state_conv, cache_k_win, cache_v_win, ln_mix, ln_mlp,
           w_conv_in, w_conv, w_conv_out, w_qkv, w_attn_out, q_norm, k_norm, sinks,
           w_up, w_down):
    bp, tp, d = x_prompt.shape
    bs, ts, _ = x_sample.shape
    win = cache_k_win.shape[2]
    kw = N_KV_HEADS * HEAD_DIM
    assert ts == SUBLANES and win == WINDOW and tp % WINDOW == 0

    tm_p = 512
    tm_conv = 1024
    m_s = bs * ts

    xp = x_prompt.reshape(bp * tp, d)
    xs = x_sample.reshape(m_s, d)

    w_in = w_conv_in[0].astype(BF16)
    gain = ln_mix[0][None]
    gate_p, u_tail, wu0, wd0, w_out = _conv_in_prompt(
        xp, gain, w_in, w_conv[0], seq=tp, tm=tm_conv, tn=256,
        cast=[(w_up, 0), (w_down, 0), (w_conv_out, 0)])
    past = state_conv[0]
    zeros = lambda n: jnp.zeros((bs, n, d), F32)
    p1 = jnp.concatenate([past[:, 1:2], zeros(ts - 1)], axis=1).reshape(m_s, d)
    p2 = jnp.concatenate([past, zeros(ts - 2)], axis=1).reshape(m_s, d)
    gate_s, u_s = _conv_in_sample(xs, gain, w_in, w_conv[0], p1, p2, seq_len=ts, tn=512)
    xp = _proj_residual(xp, gate_p, w_out, tm=tm_p)
    xs = _proj_residual(xs, gate_s, w_out, tm=m_s)
    tiles_per_seq = tp // tm_conv
    new_conv_prompt = u_tail[tiles_per_seq - 1::tiles_per_seq, SUBLANES - (CONV_WIDTH - 1):][None]
    new_conv_sample = u_s.reshape(bs, ts, d)[:, ts - (CONV_WIDTH - 1):][None]

    xp, wu1, wd1, wqkv, wo = _mlp(
        xp, ln_mlp[0][None], wu0, wd0, tm=tm_p, tf=1024,
        cast=[(w_up, 1), (w_down, 1), (w_qkv, 0), (w_attn_out, 0)])
    xs, = _mlp(xs, ln_mlp[0][None], wu0, wd0, tm=m_s, tf=1024)

    gain = ln_mix[1][None]
    tn = 2 * kw
    head_id = jnp.arange(tn) // HEAD_DIM
    ones_blockdiag = (head_id[:, None] == head_id[None, :]).astype(BF16)
    q_gain = jnp.tile(q_norm[0], tn // HEAD_DIM)[None]
    k_gain = jnp.tile(k_norm[0], kw // HEAD_DIM)[None]
    cos_p, sin_p = _rope_tables(jnp.arange(tp, dtype=jnp.int32))
    cos_s, sin_s = _rope_tables(PAST_LEN + jnp.arange(ts, dtype=jnp.int32))
    cos_s, sin_s = jnp.tile(cos_s, (bs, 1)), jnp.tile(sin_s, (bs, 1))
    q_p, kv_p = _qkv(xp, gain, wqkv, ones_blockdiag, q_gain, k_gain, cos_p, sin_p,
                     tm=tm_p, rope_tiles=tp // tm_p)
    q_s, kv_s = _qkv(xs, gain, wqkv, ones_blockdiag, q_gain, k_gain, cos_s, sin_s,
                     tm=m_s, rope_tiles=1)
    o_p = _attn_prompt(sinks[0], q_p, kv_p, batch=bp, seq=tp)
    ck = cache_k_win[0].reshape(bs, win, kw)
    cv = cache_v_win[0].reshape(bs, win, kw)
    o_s = _attn_sample(sinks[0], q_s, kv_s, ck, cv, seq_len=ts)
    xp = _proj_residual(xp, o_p, wo, tm=tm_p)
    xs = _proj_residual(xs, o_s, wo, tm=m_s)

    kv_p3 = kv_p.reshape(bp, tp, 2 * kw)[:, tp - WINDOW:]
    new_k_prompt = kv_p3[:, :, :kw].reshape(1, bp, WINDOW, N_KV_HEADS, HEAD_DIM)
    new_v_prompt = kv_p3[:, :, kw:].reshape(1, bp, WINDOW, N_KV_HEADS, HEAD_DIM)
    kv_s3 = kv_s.reshape(bs, ts, 2 * kw)
    new_k_sample = jnp.concatenate([ck[:, ts:], kv_s3[:, :, :kw]], axis=1)
    new_v_sample = jnp.concatenate([cv[:, ts:], kv_s3[:, :, kw:]], axis=1)
    new_k_sample = new_k_sample.reshape(1, bs, win, N_KV_HEADS, HEAD_DIM)
    new_v_sample = new_v_sample.reshape(1, bs, win, N_KV_HEADS, HEAD_DIM)

    xp, = _mlp(xp, ln_mlp[1][None], wu1, wd1, tm=tm_p, tf=1024)
    xs, = _mlp(xs, ln_mlp[1][None], wu1, wd1, tm=m_s, tf=1024)

    return (xp.reshape(bp, tp, d), xs.reshape(bs, ts, d), new_conv_prompt, new_conv_sample,
            new_k_prompt, new_v_prompt, new_k_sample, new_v_sample)
```

```python
import functools

import jax
import jax.numpy as jnp
from jax import lax
from jax.experimental import pallas as pl
from jax.experimental.pallas import tpu as pltpu

F32 = jnp.float32
BF16 = jnp.bfloat16

HEAD_DIM = 64
N_HEADS = 32
N_KV_HEADS = 4
GROUP = N_HEADS // N_KV_HEADS
WINDOW = 128
PAST_LEN = 16384
ROPE_THETA = 10000.0
EPS = 1e-6
CONV_WIDTH = 3

SUBLANES = 8
LANES = 128
MIB = 1024 * 1024
QKV_ROW_CHUNK = 256
CONV_ROW_CHUNK = 256
MLP_NORM_ROW_CHUNK = 256


def _params(n_grid_dims, vmem_mib):
    return pltpu.CompilerParams(
        dimension_semantics=("arbitrary",) * n_grid_dims,
        vmem_limit_bytes=vmem_mib * MIB,
    )


def _rms_norm(x, gain):
    return x * lax.rsqrt(jnp.mean(x * x, axis=-1, keepdims=True) + EPS) * gain


def _dot(a, b):
    return jnp.dot(a, b, preferred_element_type=F32)


def _cast_specs(weights, grid):
    n_steps = 1
    for g in grid:
        n_steps *= g

    def step_of(*ids):
        step = ids[0]
        for g, idx in zip(grid[1:], ids[1:]):
            step = step * g + idx
        return step

    in_specs, out_specs, out_shapes = [], [], []
    for w, layer in weights:
        rows, cols = w.shape[1:]
        assert rows % (n_steps * 2 * SUBLANES) == 0, (w.shape, n_steps)
        rb = rows // n_steps
        in_specs.append(pl.BlockSpec((None, rb, cols),
                                     lambda *ids, layer=layer: (layer, step_of(*ids), 0)))
        out_specs.append(pl.BlockSpec((rb, cols), lambda *ids: (step_of(*ids), 0)))
        out_shapes.append(jax.ShapeDtypeStruct((rows, cols), BF16))
    return in_specs, out_specs, out_shapes


def _cast_blocks(src_refs, dst_refs):
    for src, dst in zip(src_refs, dst_refs):
        dst[...] = src[...].astype(BF16)


def _mlp_body(x_ref, g_ref, wu_ref, wd_ref, *rest, n_cast):
    cast_src, (o_ref,), cast_dst, (h_ref,) = (
        rest[:n_cast], rest[n_cast:n_cast + 1], rest[n_cast + 1:2 * n_cast + 1],
        rest[2 * n_cast + 1:])

    _cast_blocks(cast_src, cast_dst)

    def ffn(h):
        a = jnp.maximum(_dot(h, wu_ref[...]), 0.0)
        return _dot((a * a).astype(BF16), wd_ref[...])

    @pl.when(pl.program_id(1) == 0)
    def _():
        tm = x_ref.shape[0]
        chunk = min(tm, MLP_NORM_ROW_CHUNK)
        for c in range(tm // chunk):
            rows = pl.ds(c * chunk, chunk)
            x = x_ref[rows, :]
            h = _rms_norm(x, g_ref[...]).astype(BF16)
            h_ref[rows, :] = h
            o_ref[rows, :] = x + ffn(h)

    @pl.when(pl.program_id(1) != 0)
    def _():
        o_ref[...] += ffn(h_ref[...])


def _mlp(x, gain, w_up, w_down, *, tm, tf, cast=()):
    m, d = x.shape
    ff = w_up.shape[1]
    grid = (m // tm, ff // tf)
    cast_in, cast_out, cast_shapes = _cast_specs(cast, grid)
    return pl.pallas_call(
        functools.partial(_mlp_body, n_cast=len(cast)),
        grid=grid,
        in_specs=[
            pl.BlockSpec((tm, d), lambda i, f: (i, 0)),
            pl.BlockSpec((1, d), lambda i, f: (0, 0)),
            pl.BlockSpec((d, tf), lambda i, f: (0, f)),
            pl.BlockSpec((tf, d), lambda i, f: (f, 0)),
        ] + cast_in,
        out_specs=[pl.BlockSpec((tm, d), lambda i, f: (i, 0))] + cast_out,
        out_shape=[jax.ShapeDtypeStruct((m, d), F32)] + cast_shapes,
        scratch_shapes=[pltpu.VMEM((tm, d), BF16)],
        compiler_params=_params(2, 52),
        name="mlp",
    )(x, gain, w_up, w_down, *[w for w, _ in cast])


def _proj_body(x_ref, a_ref, w_ref, o_ref):
    o_ref[...] = x_ref[...] + _dot(a_ref[...], w_ref[...])


def _proj_residual(x, a, w, *, tm):
    m, d = x.shape
    k = a.shape[1]
    return pl.pallas_call(
        _proj_body,
        grid=(m // tm,),
        in_specs=[
            pl.BlockSpec((tm, d), lambda i: (i, 0)),
            pl.BlockSpec((tm, k), lambda i: (i, 0)),
            pl.BlockSpec((k, d), lambda i: (0, 0)),
        ],
        out_specs=pl.BlockSpec((tm, d), lambda i: (i, 0)),
        out_shape=jax.ShapeDtypeStruct((m, d), F32),
        compiler_params=_params(1, 48),
        name="proj_residual",
    )(x, a, w)


def _conv_taps(u, u1, u2, w):
    return w[0:1, :] * u2 + w[1:2, :] * u1 + w[2:3, :] * u


def _conv_in_prompt_body(x_ref, g_ref, wb_ref, wc_ref, wv_ref, wconv_ref, *rest,
                         tiles_per_seq, n_cast):
    n_w = 3
    cast_src, (gate_ref, utail_ref), w16_refs, cast_dst, (h_ref, carry_ref) = (
        rest[:n_cast], rest[n_cast:n_cast + 2], rest[n_cast + 2:n_cast + 2 + n_w],
        rest[n_cast + 2 + n_w:2 * n_cast + 2 + n_w], rest[2 * n_cast + 2 + n_w:])
    _cast_blocks(cast_src, cast_dst)
    i = pl.program_id(0)
    j = pl.program_id(1)

    @pl.when(i == 0)
    def _():
        _cast_blocks((wb_ref, wc_ref, wv_ref), w16_refs)

    @pl.when(i % tiles_per_seq == 0)
    def _():
        carry_ref[j] = jnp.zeros(carry_ref.shape[1:], F32)

    tm = h_ref.shape[0]
    chunk = min(tm, CONV_ROW_CHUNK)
    row8 = lax.broadcasted_iota(jnp.int32, (SUBLANES, 1), 0)
    w_conv = wconv_ref[...]

    def run(first_column_tile):
        prev = carry_ref[j]
        w_b, w_c, w_v = (r[...].astype(BF16) for r in (wb_ref, wc_ref, wv_ref))
        for c in range(tm // chunk):
            rows = pl.ds(c * chunk, chunk)
            if first_column_tile:
                h = _rms_norm(x_ref[rows, :], g_ref[...]).astype(BF16)
                h_ref[rows, :] = h
            else:
                h = h_ref[rows, :]
            u = _dot(h, w_c) * _dot(h, w_v)
            u1 = pltpu.roll(u, 1, axis=0)
            u2 = pltpu.roll(u, 2, axis=0)
            top1 = jnp.where(row8 < 1, pltpu.roll(prev, 1, axis=0), u1[0:SUBLANES])
            top2 = jnp.where(row8 < 2, pltpu.roll(prev, 2, axis=0), u2[0:SUBLANES])
            u1 = jnp.concatenate([top1, u1[SUBLANES:]], axis=0)
            u2 = jnp.concatenate([top2, u2[SUBLANES:]], axis=0)
            gate_ref[rows, :] = (_dot(h, w_b) * _conv_taps(u, u1, u2, w_conv)).astype(BF16)
            prev = u[chunk - SUBLANES:]
        carry_ref[j] = prev
        utail_ref[0] = prev

    pl.when(j == 0)(functools.partial(run, True))
    pl.when(j != 0)(functools.partial(run, False))


def _conv_in_sample_body(x_ref, g_ref, wb_ref, wc_ref, wv_ref, wconv_ref, p1_ref, p2_ref,
                         gate_ref, u_ref, h_ref, *, seq_len):
    j = pl.program_id(1)

    @pl.when(j == 0)
    def _():
        h_ref[...] = _rms_norm(x_ref[...], g_ref[...]).astype(BF16)

    h = h_ref[...]
    u = _dot(h, wc_ref[...]) * _dot(h, wv_ref[...])
    step = lax.broadcasted_iota(jnp.int32, (u.shape[0], 1), 0) % seq_len
    u1 = jnp.where(step < 1, p1_ref[...], pltpu.roll(u, 1, axis=0))
    u2 = jnp.where(step < 2, p2_ref[...], pltpu.roll(u, 2, axis=0))
    conv = _conv_taps(u, u1, u2, wconv_ref[...])
    u_ref[...] = u
    gate_ref[...] = (_dot(h, wb_ref[...]) * conv).astype(BF16)


def _conv_in_specs(tm, d, tn, *, fused_weight):
    third = d // tn if fused_weight else 0
    return [
        pl.BlockSpec((tm, d), lambda i, j: (i, 0)),
        pl.BlockSpec((1, d), lambda i, j: (0, 0)),
        pl.BlockSpec((d, tn), lambda i, j: (0, j)),
        pl.BlockSpec((d, tn), lambda i, j: (0, third + j)),
        pl.BlockSpec((d, tn), lambda i, j: (0, 2 * third + j)),
        pl.BlockSpec((CONV_WIDTH, tn), lambda i, j: (0, j)),
    ]


def _conv_in_prompt(x, gain, w_in, w_conv, *, seq, tm, tn, cast=()):
    m, d = x.shape
    nj = d // tn
    grid = (m // tm, nj)
    cast_in, cast_out, cast_shapes = _cast_specs(cast, grid)
    body = functools.partial(_conv_in_prompt_body, tiles_per_seq=seq // tm, n_cast=len(cast))
    w16_spec = pl.BlockSpec((d, tn), lambda i, j: (0, jnp.where(i == 0, j, nj - 1)))
    return pl.pallas_call(
        body,
        grid=grid,
        in_specs=_conv_in_specs(tm, d, tn, fused_weight=True) + cast_in,
        out_specs=[
            pl.BlockSpec((tm, tn), lambda i, j: (i, j)),
            pl.BlockSpec((1, SUBLANES, tn), lambda i, j: (i, 0, j)),
        ] + [w16_spec] * 3 + cast_out,
        out_shape=[
            jax.ShapeDtypeStruct((m, d), BF16),
            jax.ShapeDtypeStruct((m // tm, SUBLANES, d), F32),
        ] + [jax.ShapeDtypeStruct((d, d), BF16)] * 3 + cast_shapes,
        scratch_shapes=[
            pltpu.VMEM((tm, d), BF16),
            pltpu.VMEM((d // tn, SUBLANES, tn), F32),
        ],
        compiler_params=_params(2, 52),
        name="conv_in_prompt",
    )(x, gain, w_in, w_in, w_in, w_conv, *[w for w, _ in cast])


def _conv_in_sample(x, gain, w_bcv, w_conv, p1, p2, *, seq_len, tn):
    m, d = x.shape
    body = functools.partial(_conv_in_sample_body, seq_len=seq_len)
    return pl.pallas_call(
        body,
        grid=(1, d // tn),
        in_specs=_conv_in_specs(m, d, tn, fused_weight=False) + [
            pl.BlockSpec((m, tn), lambda i, j: (0, j)),
            pl.BlockSpec((m, tn), lambda i, j: (0, j)),
        ],
        out_specs=[
            pl.BlockSpec((m, tn), lambda i, j: (0, j)),
            pl.BlockSpec((m, tn), lambda i, j: (0, j)),
        ],
        out_shape=[
            jax.ShapeDtypeStruct((m, d), BF16),
            jax.ShapeDtypeStruct((m, d), F32),
        ],
        scratch_shapes=[pltpu.VMEM((m, d), BF16)],
        compiler_params=_params(2, 48),
        name="conv_in_sample",
    )(x, gain, *w_bcv, w_conv, p1, p2)


def _head_norm_rope(z, gain, ones_blockdiag, cos, sin_signed, *, split_ssq):
    w = z.shape[1]
    zz = z * z
    hi = zz.astype(BF16)
    ssq = _dot(hi, ones_blockdiag)
    if split_ssq:
        ssq += _dot((zz - hi.astype(F32)).astype(BF16), ones_blockdiag)
    zn = z * lax.rsqrt(ssq * (1.0 / HEAD_DIM) + EPS) * gain
    reps = w // cos.shape[1]
    cos = jnp.concatenate([cos] * reps, axis=1)
    sin_signed = jnp.concatenate([sin_signed] * reps, axis=1)
    lane = lax.broadcasted_iota(jnp.int32, (1, w), 1)
    first_half = (lane % HEAD_DIM) < (HEAD_DIM // 2)
    partner = jnp.where(first_half,
                        pltpu.roll(zn, w - HEAD_DIM // 2, axis=1),
                        pltpu.roll(zn, HEAD_DIM // 2, axis=1))
    return zn * cos + partner * sin_signed


def _qkv_body(x_ref, g_ref, w_ref, ones_ref, qgain_ref, kgain_ref, cos_ref, sin_ref,
              q_ref, kv_ref):
    tm = x_ref.shape[0]
    qw = q_ref.shape[1]
    tn = ones_ref.shape[0]
    kw = N_KV_HEADS * HEAD_DIM
    chunk = min(tm, QKV_ROW_CHUNK)
    for c in range(tm // chunk):
        rows = pl.ds(c * chunk, chunk)
        h = _rms_norm(x_ref[rows, :], g_ref[...]).astype(BF16)
        cos, sin_signed = cos_ref[rows, :], sin_ref[rows, :]
        for t in range(qw // tn):
            cols = pl.ds(t * tn, tn)
            z = _dot(h, w_ref[:, cols])
            q = _head_norm_rope(z, qgain_ref[...], ones_ref[...], cos, sin_signed,
                                split_ssq=False)
            q_ref[rows, cols] = (q * HEAD_DIM ** -0.5).astype(BF16)
        z = _dot(h, w_ref[:, pl.ds(qw, 2 * kw)])
        k = _head_norm_rope(z[:, :kw], kgain_ref[...], ones_ref[0:kw, 0:kw], cos, sin_signed,
                            split_ssq=True)
        kv_ref[rows, :] = jnp.concatenate([k, z[:, kw:]], axis=1)


def _qkv(x, gain, w_qkv, ones_blockdiag, q_gain, k_gain, cos, sin_signed, *, tm, rope_tiles):
    m, d = x.shape
    qw = N_HEADS * HEAD_DIM
    kvw = 2 * N_KV_HEADS * HEAD_DIM
    tn = ones_blockdiag.shape[0]
    const = lambda shape: pl.BlockSpec(shape, lambda i: (0, 0), pipeline_mode=pl.Buffered(1))
    return pl.pallas_call(
        _qkv_body,
        grid=(m // tm,),
        in_specs=[
            pl.BlockSpec((tm, d), lambda i: (i, 0)),
            const((1, d)),
            const(w_qkv.shape),
            const((tn, tn)),
            const((1, tn)),
            const((1, kvw // 2)),
            pl.BlockSpec((tm, LANES), lambda i: (i % rope_tiles, 0)),
            pl.BlockSpec((tm, LANES), lambda i: (i % rope_tiles, 0)),
        ],
        out_specs=[
            pl.BlockSpec((tm, qw), lambda i: (i, 0)),
            pl.BlockSpec((tm, kvw), lambda i: (i, 0)),
        ],
        out_shape=[
            jax.ShapeDtypeStruct((m, qw), BF16),
            jax.ShapeDtypeStruct((m, kvw), F32),
        ],
        compiler_params=_params(1, 48),
        name="qkv",
    )(x, gain, w_qkv, ones_blockdiag, q_gain, k_gain, cos, sin_signed)


MASKED = -1e30


def _attn_prompt_body(sinks_ref, q_ref, kv_prev_ref, kv_cur_ref, o_ref):
    n = pl.program_id(1)
    blk = q_ref.shape[0]
    kv = jnp.concatenate([kv_prev_ref[...], kv_cur_ref[...]], axis=0)
    kw = N_KV_HEADS * HEAD_DIM
    r = lax.broadcasted_iota(jnp.int32, (blk, 2 * blk), 0)
    c = lax.broadcasted_iota(jnp.int32, (blk, 2 * blk), 1)
    visible = (c > r + (blk - WINDOW)) & (c <= r + blk) & ((c >= blk) | (n > 0))
    lower = lax.broadcasted_iota(jnp.int32, (1, LANES), 1) < HEAD_DIM
    heads_per_col = LANES // HEAD_DIM
    for g in range(N_KV_HEADS):
        col, half = divmod(g, heads_per_col)
        kcol = kv[:, col * LANES:(col + 1) * LANES]
        vcol = kv[:, kw + col * LANES:kw + (col + 1) * LANES]
        kswap = pltpu.roll(kcol, HEAD_DIM, axis=1)
        vswap = pltpu.roll(vcol, HEAD_DIM, axis=1)
        in_lower = (kcol, vcol) if half == 0 else (kswap, vswap)
        in_upper = (kswap, vswap) if half == 0 else (kcol, vcol)
        k_lo, v_lo = (jnp.where(lower, t, 0.0).astype(BF16) for t in in_lower)
        k_hi, v_hi = (jnp.where(lower, 0.0, t).astype(BF16) for t in in_upper)
        for pair in range(GROUP // heads_per_col):
            h0 = g * GROUP + pair * heads_per_col
            lanes = pl.ds(h0 * HEAD_DIM, LANES)
            q_pair = q_ref[:, lanes]
            acc = None
            for h, k, v in ((h0, k_lo, v_lo), (h0 + 1, k_hi, v_hi)):
                s = lax.dot_general(q_pair, k, (((1,), (1,)), ((), ())),
                                    preferred_element_type=F32)
                s = jnp.where(visible, s, MASKED)
                sink = sinks_ref[h]
                m = jnp.maximum(jnp.max(s, axis=-1, keepdims=True), sink)
                p = jnp.exp(s - m)
                den = jnp.sum(p, axis=-1, keepdims=True) + jnp.exp(sink - m)
                o = _dot(p.astype(BF16), v) * (1.0 / den)
                acc = o if acc is None else acc + o
            o_ref[:, lanes] = acc.astype(BF16)


def _attn_prompt(sinks, q, kv, *, batch, seq):
    m, qw = q.shape
    kvw = kv.shape[1]
    blk = WINDOW
    nb = seq // blk
    return pl.pallas_call(
        _attn_prompt_body,
        grid_spec=pltpu.PrefetchScalarGridSpec(
            num_scalar_prefetch=1,
            grid=(batch, nb),
            in_specs=[
                pl.BlockSpec((blk, qw), lambda b, n, s: (b * nb + n, 0)),
                pl.BlockSpec((blk, kvw), lambda b, n, s: (b * nb + jnp.maximum(n - 1, 0), 0)),
                pl.BlockSpec((blk, kvw), lambda b, n, s: (b * nb + n, 0)),
            ],
            out_specs=pl.BlockSpec((blk, qw), lambda b, n, s: (b * nb + n, 0)),
        ),
        out_shape=jax.ShapeDtypeStruct((m, qw), BF16),
        compiler_params=_params(2, 32),
        name="attn_prompt",
    )(sinks, q, kv, kv)


def _attn_sample_body(sinks_ref, q_ref, kv_ref, ck_ref, cv_ref, o_ref, *, seq_len):
    nseq, win, kw = ck_ref.shape
    rows = GROUP * seq_len
    q = q_ref[...].astype(F32)
    kv_new = kv_ref[...]
    pad = jnp.zeros((nseq, win - seq_len, HEAD_DIM), F32)
    step = lax.broadcasted_iota(jnp.int32, (1, rows, 2 * win), 1) % seq_len
    c = lax.broadcasted_iota(jnp.int32, (1, rows, 2 * win), 2)
    visible = ((c < win) & (c > step + (win - WINDOW))) | ((c >= win) & (c - win <= step))
    pieces = []
    for g in range(N_KV_HEADS):
        lo, hi = g * HEAD_DIM, (g + 1) * HEAD_DIM
        heads = range(g * GROUP, (g + 1) * GROUP)
        k_new = kv_new[:, lo:hi].reshape(nseq, seq_len, HEAD_DIM)
        v_new = kv_new[:, kw + lo:kw + hi].reshape(nseq, seq_len, HEAD_DIM)
        k = jnp.concatenate([ck_ref[:, :, lo:hi], k_new, pad], axis=1).astype(BF16)
        v = jnp.concatenate([cv_ref[:, :, lo:hi], v_new, pad], axis=1).astype(BF16)
        qg = jnp.concatenate(
            [q[:, h * HEAD_DIM:(h + 1) * HEAD_DIM].reshape(nseq, seq_len, HEAD_DIM)
             for h in heads], axis=1).astype(BF16)
        s = jnp.einsum("bqd,bkd->bqk", qg, k, preferred_element_type=F32)
        s = jnp.where(visible, s, MASKED)
        sink_col = jnp.concatenate(
            [jnp.full((1, seq_len, 1), sinks_ref[h], F32) for h in heads], axis=1)
        m = jnp.maximum(jnp.max(s, axis=-1, keepdims=True), sink_col)
        p = jnp.exp(s - m)
        den = jnp.sum(p, axis=-1, keepdims=True) + jnp.exp(sink_col - m)
        o = jnp.einsum("bqk,bkd->bqd", p.astype(BF16), v, preferred_element_type=F32) / den
        pieces += [o[:, t * seq_len:(t + 1) * seq_len].reshape(nseq * seq_len, HEAD_DIM)
                   for t in range(GROUP)]
    o_ref[...] = jnp.concatenate(pieces, axis=1).astype(BF16)


def _attn_sample(sinks, q, kv, cache_k, cache_v, *, seq_len):
    m, qw = q.shape
    body = functools.partial(_attn_sample_body, seq_len=seq_len)
    full = lambda shape: pl.BlockSpec(shape, lambda i, s: (0,) * len(shape))
    return pl.pallas_call(
        body,
        grid_spec=pltpu.PrefetchScalarGridSpec(
            num_scalar_prefetch=1,
            grid=(1,),
            in_specs=[full(q.shape), full(kv.shape), full(cache_k.shape), full(cache_v.shape)],
            out_specs=full((m, qw)),
        ),
        out_shape=jax.ShapeDtypeStruct((m, qw), BF16),
        compiler_params=_params(1, 48),
        name="attn_sample",
    )(sinks, q, kv, cache_k, cache_v)


def _rope_tables(pos):
    half = HEAD_DIM // 2
    inv = ROPE_THETA ** (-jnp.arange(half, dtype=F32) / half)
    ang = pos.astype(F32)[:, None] * inv[None, :]
    cos, sin = jnp.cos(ang), jnp.sin(ang)
    reps = LANES // HEAD_DIM
    cos = jnp.tile(jnp.concatenate([cos, cos], axis=1), (1, reps))
    sin_signed = jnp.tile(jnp.concatenate([-sin, sin], axis=1), (1, reps))
    return cos, sin_signed


def kernel(x_prompt, x_sample, state_conv, cache_k_win, cache_v_win, ln_mix, ln_mlp,
           w_conv_in, w_conv, w_conv_out, w_qkv, w_attn_out, q_norm, k_norm, sinks,
           w_up, w_down):
    bp, tp, d = x_prompt.shape
    bs, ts, _ = x_sample.shape
    win = cache_k_win.shape[2]
    kw = N_KV_HEADS * HEAD_DIM
    assert ts == SUBLANES and win == WINDOW and tp % WINDOW == 0

    tm_p = 512
    tm_conv = 1024
    m_s = bs * ts

    xp = x_prompt.reshape(bp * tp, d)
    xs = x_sample.reshape(m_s, d)

    gain = ln_mix[0][None]
    gate_p, u_tail, *w_bcv, wu0, wd0, w_out = _conv_in_prompt(
        xp, gain, w_conv_in[0], w_conv[0], seq=tp, tm=tm_conv, tn=256,
        cast=[(w_up, 0), (w_down, 0), (w_conv_out, 0)])
    past = state_conv[0]
    zeros = lambda n: jnp.zeros((bs, n, d), F32)
    p1 = jnp.concatenate([past[:, 1:2], zeros(ts - 1)], axis=1).reshape(m_s, d)
    p2 = jnp.concatenate([past, zeros(ts - 2)], axis=1).reshape(m_s, d)
    gate_s, u_s = _conv_in_sample(xs, gain, w_bcv, w_conv[0], p1, p2, seq_len=ts, tn=512)
    xp = _proj_residual(xp, gate_p, w_out, tm=tm_p)
    xs = _proj_residual(xs, gate_s, w_out, tm=m_s)
    tiles_per_seq = tp // tm_conv
    new_conv_prompt = u_tail[tiles_per_seq - 1::tiles_per_seq, SUBLANES - (CONV_WIDTH - 1):][None]
    new_conv_sample = u_s.reshape(bs, ts, d)[:, ts - (CONV_WIDTH - 1):][None]

    xp, wu1, wd1, wqkv, wo = _mlp(
        xp, ln_mlp[0][None], wu0, wd0, tm=tm_p, tf=1024,
        cast=[(w_up, 1), (w_down, 1), (w_qkv, 0), (w_attn_out, 0)])
    xs, = _mlp(xs, ln_mlp[0][None], wu0, wd0, tm=m_s, tf=1024)

    gain = ln_mix[1][None]
    tn = 2 * kw
    head_id = jnp.arange(tn) // HEAD_DIM
    ones_blockdiag = (head_id[:, None] == head_id[None, :]).astype(BF16)
    q_gain = jnp.tile(q_norm[0], tn // HEAD_DIM)[None]
    k_gain = jnp.tile(k_norm[0], kw // HEAD_DIM)[None]
    cos_p, sin_p = _rope_tables(jnp.arange(tp, dtype=jnp.int32))
    cos_s, sin_s = _rope_tables(PAST_LEN + jnp.arange(ts, dtype=jnp.int32))
    cos_s, sin_s = jnp.tile(cos_s, (bs, 1)), jnp.tile(sin_s, (bs, 1))
    q_p, kv_p = _qkv(xp, gain, wqkv, ones_blockdiag, q_gain, k_gain, cos_p, sin_p,
                     tm=tm_p, rope_tiles=tp // tm_p)
    q_s, kv_s = _qkv(xs, gain, wqkv, ones_blockdiag, q_gain, k_gain, cos_s, sin_s,
                     tm=m_s, rope_tiles=1)
    o_p = _attn_prompt(sinks[0], q_p, kv_p, batch=bp, seq=tp)
    ck = cache_k_win[0].reshape(bs, win, kw)
    cv = cache_v_win[0].reshape(bs, win, kw)
    o_s = _attn_sample(sinks[0], q_s, kv_s, ck, cv, seq_len=ts)
    xp = _proj_residual(xp, o_p, wo, tm=tm_p)
    xs = _proj_residual(xs, o_s, wo, tm=m_s)

    kv_p3 = kv_p.reshape(bp, tp, 2 * kw)[:, tp - WINDOW:]
    new_k_prompt = kv_p3[:, :, :kw].reshape(1, bp, WINDOW, N_KV_HEADS, HEAD_DIM)
    new_v_prompt = kv_p3[:, :, kw:].reshape(1, bp, WINDOW, N_KV_HEADS, HEAD_DIM)
    kv_s3 = kv_s.reshape(bs, ts, 2 * kw)
    new_k_sample = jnp.concatenate([ck[:, ts:], kv_s3[:, :, :kw]], axis=1)
    new_v_sample = jnp.concatenate([cv[:, ts:], kv_s3[:, :, kw:]], axis=1)
    new_k_sample = new_k_sample.reshape(1, bs, win, N_KV_HEADS, HEAD_DIM)
    new_v_sample = new_v_sample.reshape(1, bs, win, N_KV_HEADS, HEAD_DIM)

    xp, = _mlp(xp, ln_mlp[1][None], wu1, wd1, tm=tm_p, tf=1024)
    xs, = _mlp(xs, ln_mlp[1][None], wu1, wd1, tm=m_s, tf=1024)

    return (xp.reshape(bp, tp, d), xs.reshape(bs, ts, d), new_conv_prompt, new_conv_sample,
            new_k_prompt, new_v_prompt, new_k_sample, new_v_sample)
```

```python
import functools

import jax
import jax.numpy as jnp
from jax import lax
from jax.experimental import pallas as pl
from jax.experimental.pallas import tpu as pltpu

F32 = jnp.float32
BF16 = jnp.bfloat16

HEAD_DIM = 64
N_HEADS = 32
N_KV_HEADS = 4
GROUP = N_HEADS // N_KV_HEADS
WINDOW = 128
PAST_LEN = 16384
ROPE_THETA = 10000.0
EPS = 1e-6
CONV_WIDTH = 3

SUBLANES = 8
LANES = 128
MIB = 1024 * 1024
QKV_ROW_CHUNK = 256
CONV_ROW_CHUNK = 256
MLP_NORM_ROW_CHUNK = 256


def _params(n_grid_dims, vmem_mib):
    return pltpu.CompilerParams(
        dimension_semantics=("arbitrary",) * n_grid_dims,
        vmem_limit_bytes=vmem_mib * MIB,
    )


def _rms_norm(x, gain):
    return x * lax.rsqrt(jnp.mean(x * x, axis=-1, keepdims=True) + EPS) * gain


def _dot(a, b):
    return jnp.dot(a, b, preferred_element_type=F32)


def _cast_specs(weights, grid):
    n_steps = 1
    for g in grid:
        n_steps *= g

    def step_of(*ids):
        step = ids[0]
        for g, idx in zip(grid[1:], ids[1:]):
            step = step * g + idx
        return step

    in_specs, out_specs, out_shapes = [], [], []
    for w, layer in weights:
        rows, cols = w.shape[1:]
        assert rows % (n_steps * 2 * SUBLANES) == 0, (w.shape, n_steps)
        rb = rows // n_steps
        in_specs.append(pl.BlockSpec((None, rb, cols),
                                     lambda *ids, layer=layer: (layer, step_of(*ids), 0)))
        out_specs.append(pl.BlockSpec((rb, cols), lambda *ids: (step_of(*ids), 0)))
        out_shapes.append(jax.ShapeDtypeStruct((rows, cols), BF16))
    return in_specs, out_specs, out_shapes


def _cast_blocks(src_refs, dst_refs):
    for src, dst in zip(src_refs, dst_refs):
        dst[...] = src[...].astype(BF16)


def _mixer_out_mlp_body(x_ref, a_ref, wp_ref, g_ref, wu_ref, wd_ref, *rest, n_cast):
    cast_src, (o_ref,), cast_dst, (h_ref,) = (
        rest[:n_cast], rest[n_cast:n_cast + 1], rest[n_cast + 1:2 * n_cast + 1],
        rest[2 * n_cast + 1:])

    _cast_blocks(cast_src, cast_dst)

    def ffn(h):
        a = jnp.maximum(_dot(h, wu_ref[...]), 0.0)
        return _dot((a * a).astype(BF16), wd_ref[...])

    @pl.when(pl.program_id(1) == 0)
    def _():
        tm = x_ref.shape[0]
        chunk = min(tm, MLP_NORM_ROW_CHUNK)
        for c in range(tm // chunk):
            rows = pl.ds(c * chunk, chunk)
            x1 = x_ref[rows, :] + _dot(a_ref[rows, :], wp_ref[...])
            h = _rms_norm(x1, g_ref[...]).astype(BF16)
            h_ref[rows, :] = h
            o_ref[rows, :] = x1 + ffn(h)

    @pl.when(pl.program_id(1) != 0)
    def _():
        o_ref[...] += ffn(h_ref[...])


def _mixer_out_mlp(x, a, w_proj, gain, w_up, w_down, *, tm, tf, cast=()):
    m, d = x.shape
    k = a.shape[1]
    ff = w_up.shape[1]
    grid = (m // tm, ff // tf)
    cast_in, cast_out, cast_shapes = _cast_specs(cast, grid)
    return pl.pallas_call(
        functools.partial(_mixer_out_mlp_body, n_cast=len(cast)),
        grid=grid,
        in_specs=[
            pl.BlockSpec((tm, d), lambda i, f: (i, 0)),
            pl.BlockSpec((tm, k), lambda i, f: (i, 0)),
            pl.BlockSpec((k, d), lambda i, f: (0, 0), pipeline_mode=pl.Buffered(1)),
            pl.BlockSpec((1, d), lambda i, f: (0, 0)),
            pl.BlockSpec((d, tf), lambda i, f: (0, f)),
            pl.BlockSpec((tf, d), lambda i, f: (f, 0)),
        ] + cast_in,
        out_specs=[pl.BlockSpec((tm, d), lambda i, f: (i, 0))] + cast_out,
        out_shape=[jax.ShapeDtypeStruct((m, d), F32)] + cast_shapes,
        scratch_shapes=[pltpu.VMEM((tm, d), BF16)],
        compiler_params=_params(2, 56),
        name="mixer_out_mlp",
    )(x, a, w_proj, gain, w_up, w_down, *[w for w, _ in cast])


def _conv_taps(u, u1, u2, w):
    return w[0:1, :] * u2 + w[1:2, :] * u1 + w[2:3, :] * u


def _conv_in_prompt_body(x_ref, g_ref, wb_ref, wc_ref, wv_ref, wconv_ref, *rest,
                         tiles_per_seq, n_cast):
    cast_src, (gate_ref, utail_ref), cast_dst, (h_ref, carry_ref) = (
        rest[:n_cast], rest[n_cast:n_cast + 2], rest[n_cast + 2:2 * n_cast + 2],
        rest[2 * n_cast + 2:])
    _cast_blocks(cast_src, cast_dst)
    i = pl.program_id(0)
    j = pl.program_id(1)

    @pl.when(i % tiles_per_seq == 0)
    def _():
        carry_ref[j] = jnp.zeros(carry_ref.shape[1:], F32)

    tm = h_ref.shape[0]
    chunk = min(tm, CONV_ROW_CHUNK)
    row8 = lax.broadcasted_iota(jnp.int32, (SUBLANES, 1), 0)
    w_conv = wconv_ref[...]

    def run(first_column_tile):
        prev = carry_ref[j]
        w_b, w_c, w_v = wb_ref[...], wc_ref[...], wv_ref[...]
        for c in range(tm // chunk):
            rows = pl.ds(c * chunk, chunk)
            if first_column_tile:
                h = _rms_norm(x_ref[rows, :], g_ref[...]).astype(BF16)
                h_ref[rows, :] = h
            else:
                h = h_ref[rows, :]
            u = _dot(h, w_c) * _dot(h, w_v)
            u1 = pltpu.roll(u, 1, axis=0)
            u2 = pltpu.roll(u, 2, axis=0)
            top1 = jnp.where(row8 < 1, pltpu.roll(prev, 1, axis=0), u1[0:SUBLANES])
            top2 = jnp.where(row8 < 2, pltpu.roll(prev, 2, axis=0), u2[0:SUBLANES])
            u1 = jnp.concatenate([top1, u1[SUBLANES:]], axis=0)
            u2 = jnp.concatenate([top2, u2[SUBLANES:]], axis=0)
            gate_ref[rows, :] = (_dot(h, w_b) * _conv_taps(u, u1, u2, w_conv)).astype(BF16)
            prev = u[chunk - SUBLANES:]
        carry_ref[j] = prev
        utail_ref[0] = prev

    pl.when(j == 0)(functools.partial(run, True))
    pl.when(j != 0)(functools.partial(run, False))


def _conv_in_sample_body(x_ref, g_ref, wb_ref, wc_ref, wv_ref, wconv_ref, p1_ref, p2_ref,
                         gate_ref, u_ref, h_ref, *, seq_len):
    j = pl.program_id(1)

    @pl.when(j == 0)
    def _():
        h_ref[...] = _rms_norm(x_ref[...], g_ref[...]).astype(BF16)

    h = h_ref[...]
    u = _dot(h, wc_ref[...]) * _dot(h, wv_ref[...])
    step = lax.broadcasted_iota(jnp.int32, (u.shape[0], 1), 0) % seq_len
    u1 = jnp.where(step < 1, p1_ref[...], pltpu.roll(u, 1, axis=0))
    u2 = jnp.where(step < 2, p2_ref[...], pltpu.roll(u, 2, axis=0))
    conv = _conv_taps(u, u1, u2, wconv_ref[...])
    u_ref[...] = u
    gate_ref[...] = (_dot(h, wb_ref[...]) * conv).astype(BF16)


def _conv_in_specs(tm, d, tn):
    third = d // tn
    return [
        pl.BlockSpec((tm, d), lambda i, j: (i, 0)),
        pl.BlockSpec((1, d), lambda i, j: (0, 0)),
        pl.BlockSpec((d, tn), lambda i, j: (0, j)),
        pl.BlockSpec((d, tn), lambda i, j: (0, third + j)),
        pl.BlockSpec((d, tn), lambda i, j: (0, 2 * third + j)),
        pl.BlockSpec((CONV_WIDTH, tn), lambda i, j: (0, j)),
    ]


def _conv_in_prompt(x, gain, w_in, w_conv, *, seq, tm, tn, cast=()):
    m, d = x.shape
    grid = (m // tm, d // tn)
    cast_in, cast_out, cast_shapes = _cast_specs(cast, grid)
    body = functools.partial(_conv_in_prompt_body, tiles_per_seq=seq // tm, n_cast=len(cast))
    return pl.pallas_call(
        body,
        grid=grid,
        in_specs=_conv_in_specs(tm, d, tn) + cast_in,
        out_specs=[
            pl.BlockSpec((tm, tn), lambda i, j: (i, j)),
            pl.BlockSpec((1, SUBLANES, tn), lambda i, j: (i, 0, j)),
        ] + cast_out,
        out_shape=[
            jax.ShapeDtypeStruct((m, d), BF16),
            jax.ShapeDtypeStruct((m // tm, SUBLANES, d), F32),
        ] + cast_shapes,
        scratch_shapes=[
            pltpu.VMEM((tm, d), BF16),
            pltpu.VMEM((d // tn, SUBLANES, tn), F32),
        ],
        compiler_params=_params(2, 52),
        name="conv_in_prompt",
    )(x, gain, w_in, w_in, w_in, w_conv, *[w for w, _ in cast])


def _conv_in_sample(x, gain, w_in, w_conv, p1, p2, *, seq_len, tn):
    m, d = x.shape
    body = functools.partial(_conv_in_sample_body, seq_len=seq_len)
    return pl.pallas_call(
        body,
        grid=(1, d // tn),
        in_specs=_conv_in_specs(m, d, tn) + [
            pl.BlockSpec((m, tn), lambda i, j: (0, j)),
            pl.BlockSpec((m, tn), lambda i, j: (0, j)),
        ],
        out_specs=[
            pl.BlockSpec((m, tn), lambda i, j: (0, j)),
            pl.BlockSpec((m, tn), lambda i, j: (0, j)),
        ],
        out_shape=[
            jax.ShapeDtypeStruct((m, d), BF16),
            jax.ShapeDtypeStruct((m, d), F32),
        ],
        scratch_shapes=[pltpu.VMEM((m, d), BF16)],
        compiler_params=_params(2, 48),
        name="conv_in_sample",
    )(x, gain, w_in, w_in, w_in, w_conv, p1, p2)


def _head_norm_rope(z, gain, ones_blockdiag, cos, sin_signed, *, split_ssq):
    w = z.shape[1]
    zz = z * z
    hi = zz.astype(BF16)
    ssq = _dot(hi, ones_blockdiag)
    if split_ssq:
        ssq += _dot((zz - hi.astype(F32)).astype(BF16), ones_blockdiag)
    zn = z * lax.rsqrt(ssq * (1.0 / HEAD_DIM) + EPS) * gain
    reps = w // cos.shape[1]
    cos = jnp.concatenate([cos] * reps, axis=1)
    sin_signed = jnp.concatenate([sin_signed] * reps, axis=1)
    lane = lax.broadcasted_iota(jnp.int32, (1, w), 1)
    first_half = (lane % HEAD_DIM) < (HEAD_DIM // 2)
    partner = jnp.where(first_half,
                        pltpu.roll(zn, w - HEAD_DIM // 2, axis=1),
                        pltpu.roll(zn, HEAD_DIM // 2, axis=1))
    return zn * cos + partner * sin_signed


def _qkv_body(x_ref, g_ref, w_ref, ones_ref, qgain_ref, kgain_ref, cos_ref, sin_ref,
              q_ref, kv_ref):
    tm = x_ref.shape[0]
    qw = q_ref.shape[1]
    tn = ones_ref.shape[0]
    kw = N_KV_HEADS * HEAD_DIM
    chunk = min(tm, QKV_ROW_CHUNK)
    for c in range(tm // chunk):
        rows = pl.ds(c * chunk, chunk)
        h = _rms_norm(x_ref[rows, :], g_ref[...]).astype(BF16)
        cos, sin_signed = cos_ref[rows, :], sin_ref[rows, :]
        for t in range(qw // tn):
            cols = pl.ds(t * tn, tn)
            z = _dot(h, w_ref[:, cols])
            q = _head_norm_rope(z, qgain_ref[...], ones_ref[...], cos, sin_signed,
                                split_ssq=False)
            q_ref[rows, cols] = (q * HEAD_DIM ** -0.5).astype(BF16)
        z = _dot(h, w_ref[:, pl.ds(qw, 2 * kw)])
        k = _head_norm_rope(z[:, :kw], kgain_ref[...], ones_ref[0:kw, 0:kw], cos, sin_signed,
                            split_ssq=True)
        kv_ref[rows, :] = jnp.concatenate([k, z[:, kw:]], axis=1)


def _qkv(x, gain, w_qkv, ones_blockdiag, q_gain, k_gain, cos, sin_signed, *, tm, rope_tiles):
    m, d = x.shape
    qw = N_HEADS * HEAD_DIM
    kvw = 2 * N_KV_HEADS * HEAD_DIM
    tn = ones_blockdiag.shape[0]
    const = lambda shape: pl.BlockSpec(shape, lambda i: (0, 0), pipeline_mode=pl.Buffered(1))
    return pl.pallas_call(
        _qkv_body,
        grid=(m // tm,),
        in_specs=[
            pl.BlockSpec((tm, d), lambda i: (i, 0)),
            const((1, d)),
            const(w_qkv.shape),
            const((tn, tn)),
            const((1, tn)),
            const((1, kvw // 2)),
            pl.BlockSpec((tm, LANES), lambda i: (i % rope_tiles, 0)),
            pl.BlockSpec((tm, LANES), lambda i: (i % rope_tiles, 0)),
        ],
        out_specs=[
            pl.BlockSpec((tm, qw), lambda i: (i, 0)),
            pl.BlockSpec((tm, kvw), lambda i: (i, 0)),
        ],
        out_shape=[
            jax.ShapeDtypeStruct((m, qw), BF16),
            jax.ShapeDtypeStruct((m, kvw), F32),
        ],
        compiler_params=_params(1, 48),
        name="qkv",
    )(x, gain, w_qkv, ones_blockdiag, q_gain, k_gain, cos, sin_signed)


MASKED = -1e30


def _attn_prompt_body(sinks_ref, q_ref, kv_prev_ref, kv_cur_ref, o_ref):
    n = pl.program_id(1)
    blk = q_ref.shape[0]
    kv = jnp.concatenate([kv_prev_ref[...], kv_cur_ref[...]], axis=0)
    kw = N_KV_HEADS * HEAD_DIM
    r = lax.broadcasted_iota(jnp.int32, (blk, 2 * blk), 0)
    c = lax.broadcasted_iota(jnp.int32, (blk, 2 * blk), 1)
    visible = (c > r + (blk - WINDOW)) & (c <= r + blk) & ((c >= blk) | (n > 0))
    lower = lax.broadcasted_iota(jnp.int32, (1, LANES), 1) < HEAD_DIM
    heads_per_col = LANES // HEAD_DIM
    for g in range(N_KV_HEADS):
        col, half = divmod(g, heads_per_col)
        kcol = kv[:, col * LANES:(col + 1) * LANES]
        vcol = kv[:, kw + col * LANES:kw + (col + 1) * LANES]
        kswap = pltpu.roll(kcol, HEAD_DIM, axis=1)
        vswap = pltpu.roll(vcol, HEAD_DIM, axis=1)
        in_lower = (kcol, vcol) if half == 0 else (kswap, vswap)
        in_upper = (kswap, vswap) if half == 0 else (kcol, vcol)
        k_lo, v_lo = (jnp.where(lower, t, 0.0).astype(BF16) for t in in_lower)
        k_hi, v_hi = (jnp.where(lower, 0.0, t).astype(BF16) for t in in_upper)
        for pair in range(GROUP // heads_per_col):
            h0 = g * GROUP + pair * heads_per_col
            lanes = pl.ds(h0 * HEAD_DIM, LANES)
            q_pair = q_ref[:, lanes]
            acc = None
            for h, k, v in ((h0, k_lo, v_lo), (h0 + 1, k_hi, v_hi)):
                s = lax.dot_general(q_pair, k, (((1,), (1,)), ((), ())),
                                    preferred_element_type=F32)
                s = jnp.where(visible, s, MASKED)
                sink = sinks_ref[h]
                m = jnp.maximum(jnp.max(s, axis=-1, keepdims=True), sink)
                p = jnp.exp(s - m)
                den = jnp.sum(p, axis=-1, keepdims=True) + jnp.exp(sink - m)
                o = _dot(p.astype(BF16), v) * (1.0 / den)
                acc = o if acc is None else acc + o
            o_ref[:, lanes] = acc.astype(BF16)


def _attn_prompt(sinks, q, kv, *, batch, seq):
    m, qw = q.shape
    kvw = kv.shape[1]
    blk = WINDOW
    nb = seq // blk
    return pl.pallas_call(
        _attn_prompt_body,
        grid_spec=pltpu.PrefetchScalarGridSpec(
            num_scalar_prefetch=1,
            grid=(batch, nb),
            in_specs=[
                pl.BlockSpec((blk, qw), lambda b, n, s: (b * nb + n, 0)),
                pl.BlockSpec((blk, kvw), lambda b, n, s: (b * nb + jnp.maximum(n - 1, 0), 0)),
                pl.BlockSpec((blk, kvw), lambda b, n, s: (b * nb + n, 0)),
            ],
            out_specs=pl.BlockSpec((blk, qw), lambda b, n, s: (b * nb + n, 0)),
        ),
        out_shape=jax.ShapeDtypeStruct((m, qw), BF16),
        compiler_params=_params(2, 32),
        name="attn_prompt",
    )(sinks, q, kv, kv)


def _attn_sample_body(sinks_ref, q_ref, kv_ref, ck_ref, cv_ref, o_ref, *, seq_len):
    nseq, win, kw = ck_ref.shape
    rows = GROUP * seq_len
    q = q_ref[...].astype(F32)
    kv_new = kv_ref[...]
    pad = jnp.zeros((nseq, win - seq_len, HEAD_DIM), F32)
    step = lax.broadcasted_iota(jnp.int32, (1, rows, 2 * win), 1) % seq_len
    c = lax.broadcasted_iota(jnp.int32, (1, rows, 2 * win), 2)
    visible = ((c < win) & (c > step + (win - WINDOW))) | ((c >= win) & (c - win <= step))
    pieces = []
    for g in range(N_KV_HEADS):
        lo, hi = g * HEAD_DIM, (g + 1) * HEAD_DIM
        heads = range(g * GROUP, (g + 1) * GROUP)
        k_new = kv_new[:, lo:hi].reshape(nseq, seq_len, HEAD_DIM)
        v_new = kv_new[:, kw + lo:kw + hi].reshape(nseq, seq_len, HEAD_DIM)
        k = jnp.concatenate([ck_ref[:, :, lo:hi], k_new, pad], axis=1).astype(BF16)
        v = jnp.concatenate([cv_ref[:, :, lo:hi], v_new, pad], axis=1).astype(BF16)
        qg = jnp.concatenate(
            [q[:, h * HEAD_DIM:(h + 1) * HEAD_DIM].reshape(nseq, seq_len, HEAD_DIM)
             for h in heads], axis=1).astype(BF16)
        s = jnp.einsum("bqd,bkd->bqk", qg, k, preferred_element_type=F32)
        s = jnp.where(visible, s, MASKED)
        sink_col = jnp.concatenate(
            [jnp.full((1, seq_len, 1), sinks_ref[h], F32) for h in heads], axis=1)
        m = jnp.maximum(jnp.max(s, axis=-1, keepdims=True), sink_col)
        p = jnp.exp(s - m)
        den = jnp.sum(p, axis=-1, keepdims=True) + jnp.exp(sink_col - m)
        o = jnp.einsum("bqk,bkd->bqd", p.astype(BF16), v, preferred_element_type=F32) / den
        pieces += [o[:, t * seq_len:(t + 1) * seq_len].reshape(nseq * seq_len, HEAD_DIM)
                   for t in range(GROUP)]
    o_ref[...] = jnp.concatenate(pieces, axis=1).astype(BF16)


def _attn_sample(sinks, q, kv, cache_k, cache_v, *, seq_len):
    m, qw = q.shape
    body = functools.partial(_attn_sample_body, seq_len=seq_len)
    full = lambda shape: pl.BlockSpec(shape, lambda i, s: (0,) * len(shape))
    return pl.pallas_call(
        body,
        grid_spec=pltpu.PrefetchScalarGridSpec(
            num_scalar_prefetch=1,
            grid=(1,),
            in_specs=[full(q.shape), full(kv.shape), full(cache_k.shape), full(cache_v.shape)],
            out_specs=full((m, qw)),
        ),
        out_shape=jax.ShapeDtypeStruct((m, qw), BF16),
        compiler_params=_params(1, 48),
        name="attn_sample",
    )(sinks, q, kv, cache_k, cache_v)


def _rope_tables(pos):
    half = HEAD_DIM // 2
    inv = ROPE_THETA ** (-jnp.arange(half, dtype=F32) / half)
    ang = pos.astype(F32)[:, None] * inv[None, :]
    cos, sin = jnp.cos(ang), jnp.sin(ang)
    reps = LANES // HEAD_DIM
    cos = jnp.tile(jnp.concatenate([cos, cos], axis=1), (1, reps))
    sin_signed = jnp.tile(jnp.concatenate([-sin, sin], axis=1), (1, reps))
    return cos, sin_signed


def kernel(x_prompt, x_sample, state_conv, cache_k_win, cache_v_win, ln_mix, ln_mlp,
           w_conv_in, w_conv, w_conv_out, w_qkv, w_attn_out, q_norm, k_norm, sinks,
           w_up, w_down):
    bp, tp, d = x_prompt.shape
    bs, ts, _ = x_sample.shape
    win = cache_k_win.shape[2]
    kw = N_KV_HEADS * HEAD_DIM
    assert ts == SUBLANES and win == WINDOW and tp % WINDOW == 0

    tm_p = 512
    tm_conv = 1024
    m_s = bs * ts

    xp = x_prompt.reshape(bp * tp, d)
    xs = x_sample.reshape(m_s, d)

    w_in = w_conv_in[0].astype(BF16)
    gain = ln_mix[0][None]
    gate_p, u_tail, wu0, wd0, w_out = _conv_in_prompt(
        xp, gain, w_in, w_conv[0], seq=tp, tm=tm_conv, tn=256,
        cast=[(w_up, 0), (w_down, 0), (w_conv_out, 0)])
    past = state_conv[0]
    zeros = lambda n: jnp.zeros((bs, n, d), F32)
    p1 = jnp.concatenate([past[:, 1:2], zeros(ts - 1)], axis=1).reshape(m_s, d)
    p2 = jnp.concatenate([past, zeros(ts - 2)], axis=1).reshape(m_s, d)
    gate_s, u_s = _conv_in_sample(xs, gain, w_in, w_conv[0], p1, p2, seq_len=ts, tn=512)
    tiles_per_seq = tp // tm_conv
    new_conv_prompt = u_tail[tiles_per_seq - 1::tiles_per_seq, SUBLANES - (CONV_WIDTH - 1):][None]
    new_conv_sample = u_s.reshape(bs, ts, d)[:, ts - (CONV_WIDTH - 1):][None]

    xp, wu1, wd1, wqkv, wo = _mixer_out_mlp(
        xp, gate_p, w_out, ln_mlp[0][None], wu0, wd0, tm=tm_p, tf=1024,
        cast=[(w_up, 1), (w_down, 1), (w_qkv, 0), (w_attn_out, 0)])
    xs, = _mixer_out_mlp(xs, gate_s, w_out, ln_mlp[0][None], wu0, wd0, tm=m_s, tf=1024)

    gain = ln_mix[1][None]
    tn = 2 * kw
    head_id = jnp.arange(tn) // HEAD_DIM
    ones_blockdiag = (head_id[:, None] == head_id[None, :]).astype(BF16)
    q_gain = jnp.tile(q_norm[0], tn // HEAD_DIM)[None]
    k_gain = jnp.tile(k_norm[0], kw // HEAD_DIM)[None]
    cos_p, sin_p = _rope_tables(jnp.arange(tp, dtype=jnp.int32))
    cos_s, sin_s = _rope_tables(PAST_LEN + jnp.arange(ts, dtype=jnp.int32))
    cos_s, sin_s = jnp.tile(cos_s, (bs, 1)), jnp.tile(sin_s, (bs, 1))
    q_p, kv_p = _qkv(xp, gain, wqkv, ones_blockdiag, q_gain, k_gain, cos_p, sin_p,
                     tm=tm_p, rope_tiles=tp // tm_p)
    q_s, kv_s = _qkv(xs, gain, wqkv, ones_blockdiag, q_gain, k_gain, cos_s, sin_s,
                     tm=m_s, rope_tiles=1)
    o_p = _attn_prompt(sinks[0], q_p, kv_p, batch=bp, seq=tp)
    ck = cache_k_win[0].reshape(bs, win, kw)
    cv = cache_v_win[0].reshape(bs, win, kw)
    o_s = _attn_sample(sinks[0], q_s, kv_s, ck, cv, seq_len=ts)

    kv_p3 = kv_p.reshape(bp, tp, 2 * kw)[:, tp - WINDOW:]
    new_k_prompt = kv_p3[:, :, :kw].reshape(1, bp, WINDOW, N_KV_HEADS, HEAD_DIM)
    new_v_prompt = kv_p3[:, :, kw:].reshape(1, bp, WINDOW, N_KV_HEADS, HEAD_DIM)
    kv_s3 = kv_s.reshape(bs, ts, 2 * kw)
    new_k_sample = jnp.concatenate([ck[:, ts:], kv_s3[:, :, :kw]], axis=1)
    new_v_sample = jnp.concatenate([cv[:, ts:], kv_s3[:, :, kw:]], axis=1)
    new_k_sample = new_k_sample.reshape(1, bs, win, N_KV_HEADS, HEAD_DIM)
    new_v_sample = new_v_sample.reshape(1, bs, win, N_KV_HEADS, HEAD_DIM)

    xp, = _mixer_out_mlp(xp, o_p, wo, ln_mlp[1][None], wu1, wd1, tm=tm_p, tf=1024)
    xs, = _mixer_out_mlp(xs, o_s, wo, ln_mlp[1][None], wu1, wd1, tm=m_s, tf=1024)

    return (xp.reshape(bp, tp, d), xs.reshape(bs, ts, d), new_conv_prompt, new_conv_sample,
            new_k_prompt, new_v_prompt, new_k_sample, new_v_sample)
```

```python
import functools

import jax
import jax.numpy as jnp
from jax import lax
from jax.experimental import pallas as pl
from jax.experimental.pallas import tpu as pltpu

F32 = jnp.float32
BF16 = jnp.bfloat16

HEAD_DIM = 64
N_HEADS = 32
N_KV_HEADS = 4
GROUP = N_HEADS // N_KV_HEADS
WINDOW = 128
PAST_LEN = 16384
ROPE_THETA = 10000.0
EPS = 1e-6
CONV_WIDTH = 3

SUBLANES = 8
LANES = 128
MIB = 1024 * 1024
QKV_ROW_CHUNK = 256
CONV_ROW_CHUNK = 256
MLP_NORM_ROW_CHUNK = 256
MLP_FF_CHUNK = 1024


def _params(n_grid_dims, vmem_mib):
    return pltpu.CompilerParams(
        dimension_semantics=("arbitrary",) * n_grid_dims,
        vmem_limit_bytes=vmem_mib * MIB,
    )


def _rms_norm(x, gain):
    return x * lax.rsqrt(jnp.mean(x * x, axis=-1, keepdims=True) + EPS) * gain


def _dot(a, b):
    return jnp.dot(a, b, preferred_element_type=F32)


def _cast_specs(weights, grid):
    n_steps = 1
    for g in grid:
        n_steps *= g

    def step_of(*ids):
        step = ids[0]
        for g, idx in zip(grid[1:], ids[1:]):
            step = step * g + idx
        return step

    in_specs, out_specs, out_shapes = [], [], []
    for w, layer in weights:
        rows, cols = w.shape[1:]
        assert rows % (n_steps * 2 * SUBLANES) == 0, (w.shape, n_steps)
        rb = rows // n_steps
        in_specs.append(pl.BlockSpec((None, rb, cols),
                                     lambda *ids, layer=layer: (layer, step_of(*ids), 0)))
        out_specs.append(pl.BlockSpec((rb, cols), lambda *ids: (step_of(*ids), 0)))
        out_shapes.append(jax.ShapeDtypeStruct((rows, cols), BF16))
    return in_specs, out_specs, out_shapes


def _cast_blocks(src_refs, dst_refs):
    for src, dst in zip(src_refs, dst_refs):
        dst[...] = src[...].astype(BF16)


def _mlp_body(x_ref, g_ref, wu_ref, wd_ref, *rest, n_cast, ff_chunk):
    cast_src, (o_ref,), cast_dst, (h_ref,) = (
        rest[:n_cast], rest[n_cast:n_cast + 1], rest[n_cast + 1:2 * n_cast + 1],
        rest[2 * n_cast + 1:])

    _cast_blocks(cast_src, cast_dst)
    tm = x_ref.shape[0]
    tf = wu_ref.shape[1]

    def add_ffn(rows, h, base):
        for t in range(tf // ff_chunk):
            cols = pl.ds(t * ff_chunk, ff_chunk)
            a = jnp.maximum(_dot(h, wu_ref[:, cols]), 0.0)
            part = _dot((a * a).astype(BF16), wd_ref[cols, :])
            if base is None:
                o_ref[rows, :] += part
            else:
                o_ref[rows, :] = base + part
                base = None

    @pl.when(pl.program_id(1) == 0)
    def _():
        chunk = min(tm, MLP_NORM_ROW_CHUNK)
        for c in range(tm // chunk):
            rows = pl.ds(c * chunk, chunk)
            x = x_ref[rows, :]
            h = _rms_norm(x, g_ref[...]).astype(BF16)
            h_ref[rows, :] = h
            add_ffn(rows, h, x)

    @pl.when(pl.program_id(1) != 0)
    def _():
        add_ffn(pl.ds(0, tm), h_ref[...], None)


def _mlp(x, gain, w_up, w_down, *, tm, tf, cast=()):
    m, d = x.shape
    ff = w_up.shape[1]
    grid = (m // tm, ff // tf)
    cast_in, cast_out, cast_shapes = _cast_specs(cast, grid)
    return pl.pallas_call(
        functools.partial(_mlp_body, n_cast=len(cast), ff_chunk=min(tf, MLP_FF_CHUNK)),
        grid=grid,
        in_specs=[
            pl.BlockSpec((tm, d), lambda i, f: (i, 0)),
            pl.BlockSpec((1, d), lambda i, f: (0, 0)),
            pl.BlockSpec((d, tf), lambda i, f: (0, f)),
            pl.BlockSpec((tf, d), lambda i, f: (f, 0)),
        ] + cast_in,
        out_specs=[pl.BlockSpec((tm, d), lambda i, f: (i, 0))] + cast_out,
        out_shape=[jax.ShapeDtypeStruct((m, d), F32)] + cast_shapes,
        scratch_shapes=[pltpu.VMEM((tm, d), BF16)],
        compiler_params=_params(2, 56),
        name="mlp",
    )(x, gain, w_up, w_down, *[w for w, _ in cast])


def _proj_body(x_ref, a_ref, w_ref, o_ref):
    o_ref[...] = x_ref[...] + _dot(a_ref[...], w_ref[...])


def _proj_residual(x, a, w, *, tm):
    m, d = x.shape
    k = a.shape[1]
    return pl.pallas_call(
        _proj_body,
        grid=(m // tm,),
        in_specs=[
            pl.BlockSpec((tm, d), lambda i: (i, 0)),
            pl.BlockSpec((tm, k), lambda i: (i, 0)),
            pl.BlockSpec((k, d), lambda i: (0, 0), pipeline_mode=pl.Buffered(1)),
        ],
        out_specs=pl.BlockSpec((tm, d), lambda i: (i, 0)),
        out_shape=jax.ShapeDtypeStruct((m, d), F32),
        compiler_params=_params(1, 48),
        name="proj_residual",
    )(x, a, w)


def _conv_taps(u, u1, u2, w):
    return w[0:1, :] * u2 + w[1:2, :] * u1 + w[2:3, :] * u


def _gates_and_value(h, w_bcv):
    z = _dot(h, w_bcv)
    tn = z.shape[1] // 3
    return z[:, :tn], z[:, tn:2 * tn], z[:, 2 * tn:]


def _conv_in_prompt_body(x_ref, g_ref, w_ref, wconv_ref, *rest, tiles_per_seq, n_cast):
    cast_src, (gate_ref, utail_ref), cast_dst, (h_ref, carry_ref) = (
        rest[:n_cast], rest[n_cast:n_cast + 2], rest[n_cast + 2:2 * n_cast + 2],
        rest[2 * n_cast + 2:])
    _cast_blocks(cast_src, cast_dst)
    i = pl.program_id(0)
    j = pl.program_id(1)

    @pl.when(i % tiles_per_seq == 0)
    def _():
        carry_ref[j] = jnp.zeros(carry_ref.shape[1:], F32)

    tm = h_ref.shape[0]
    chunk = min(tm, CONV_ROW_CHUNK)
    row8 = lax.broadcasted_iota(jnp.int32, (SUBLANES, 1), 0)
    w_conv = wconv_ref[...]

    def run(first_column_tile):
        prev = carry_ref[j]
        for c in range(tm // chunk):
            rows = pl.ds(c * chunk, chunk)
            if first_column_tile:
                h = _rms_norm(x_ref[rows, :], g_ref[...]).astype(BF16)
                h_ref[rows, :] = h
            else:
                h = h_ref[rows, :]
            b_gate, c_gate, value = _gates_and_value(h, w_ref[...])
            u = c_gate * value
            u1 = pltpu.roll(u, 1, axis=0)
            u2 = pltpu.roll(u, 2, axis=0)
            top1 = jnp.where(row8 < 1, pltpu.roll(prev, 1, axis=0), u1[0:SUBLANES])
            top2 = jnp.where(row8 < 2, pltpu.roll(prev, 2, axis=0), u2[0:SUBLANES])
            u1 = jnp.concatenate([top1, u1[SUBLANES:]], axis=0)
            u2 = jnp.concatenate([top2, u2[SUBLANES:]], axis=0)
            gate_ref[rows, :] = (b_gate * _conv_taps(u, u1, u2, w_conv)).astype(BF16)
            prev = u[chunk - SUBLANES:]
        carry_ref[j] = prev
        utail_ref[0] = prev

    pl.when(j == 0)(functools.partial(run, True))
    pl.when(j != 0)(functools.partial(run, False))


def _conv_in_sample_body(x_ref, g_ref, w_ref, wconv_ref, p1_ref, p2_ref,
                         gate_ref, u_ref, h_ref, *, seq_len):
    j = pl.program_id(1)

    @pl.when(j == 0)
    def _():
        h_ref[...] = _rms_norm(x_ref[...], g_ref[...]).astype(BF16)

    b_gate, c_gate, value = _gates_and_value(h_ref[...], w_ref[...])
    u = c_gate * value
    step = lax.broadcasted_iota(jnp.int32, (u.shape[0], 1), 0) % seq_len
    u1 = jnp.where(step < 1, p1_ref[...], pltpu.roll(u, 1, axis=0))
    u2 = jnp.where(step < 2, p2_ref[...], pltpu.roll(u, 2, axis=0))
    conv = _conv_taps(u, u1, u2, wconv_ref[...])
    u_ref[...] = u
    gate_ref[...] = (b_gate * conv).astype(BF16)


def _tile_in_projection(w_in, tn):
    d = w_in.shape[0]
    w = w_in.astype(BF16).reshape(d, 3, d // tn, tn)
    return w.transpose(2, 0, 1, 3).reshape(d // tn, d, 3 * tn)


def _conv_in_specs(tm, d, tn):
    return [
        pl.BlockSpec((tm, d), lambda i, j: (i, 0)),
        pl.BlockSpec((1, d), lambda i, j: (0, 0)),
        pl.BlockSpec((None, d, 3 * tn), lambda i, j: (j, 0, 0)),
        pl.BlockSpec((CONV_WIDTH, tn), lambda i, j: (0, j)),
    ]


def _conv_in_prompt(x, gain, w_in, w_conv, *, seq, tm, tn, cast=()):
    m, d = x.shape
    grid = (m // tm, d // tn)
    cast_in, cast_out, cast_shapes = _cast_specs(cast, grid)
    body = functools.partial(_conv_in_prompt_body, tiles_per_seq=seq // tm, n_cast=len(cast))
    return pl.pallas_call(
        body,
        grid=grid,
        in_specs=_conv_in_specs(tm, d, tn) + cast_in,
        out_specs=[
            pl.BlockSpec((tm, tn), lambda i, j: (i, j)),
            pl.BlockSpec((1, SUBLANES, tn), lambda i, j: (i, 0, j)),
        ] + cast_out,
        out_shape=[
            jax.ShapeDtypeStruct((m, d), BF16),
            jax.ShapeDtypeStruct((m // tm, SUBLANES, d), F32),
        ] + cast_shapes,
        scratch_shapes=[
            pltpu.VMEM((tm, d), BF16),
            pltpu.VMEM((d // tn, SUBLANES, tn), F32),
        ],
        compiler_params=_params(2, 52),
        name="conv_in_prompt",
    )(x, gain, w_in, w_conv, *[w for w, _ in cast])


def _conv_in_sample(x, gain, w_in, w_conv, p1, p2, *, seq_len, tn):
    m, d = x.shape
    body = functools.partial(_conv_in_sample_body, seq_len=seq_len)
    return pl.pallas_call(
        body,
        grid=(1, d // tn),
        in_specs=_conv_in_specs(m, d, tn) + [
            pl.BlockSpec((m, tn), lambda i, j: (0, j)),
            pl.BlockSpec((m, tn), lambda i, j: (0, j)),
        ],
        out_specs=[
            pl.BlockSpec((m, tn), lambda i, j: (0, j)),
            pl.BlockSpec((m, tn), lambda i, j: (0, j)),
        ],
        out_shape=[
            jax.ShapeDtypeStruct((m, d), BF16),
            jax.ShapeDtypeStruct((m, d), F32),
        ],
        scratch_shapes=[pltpu.VMEM((m, d), BF16)],
        compiler_params=_params(2, 48),
        name="conv_in_sample",
    )(x, gain, w_in, w_conv, p1, p2)


def _head_norm_rope(z, gain, ones_blockdiag, cos, sin_signed, *, split_ssq):
    w = z.shape[1]
    zz = z * z
    hi = zz.astype(BF16)
    ssq = _dot(hi, ones_blockdiag)
    if split_ssq:
        ssq += _dot((zz - hi.astype(F32)).astype(BF16), ones_blockdiag)
    zn = z * lax.rsqrt(ssq * (1.0 / HEAD_DIM) + EPS) * gain
    reps = w // cos.shape[1]
    cos = jnp.concatenate([cos] * reps, axis=1)
    sin_signed = jnp.concatenate([sin_signed] * reps, axis=1)
    lane = lax.broadcasted_iota(jnp.int32, (1, w), 1)
    first_half = (lane % HEAD_DIM) < (HEAD_DIM // 2)
    partner = jnp.where(first_half,
                        pltpu.roll(zn, w - HEAD_DIM // 2, axis=1),
                        pltpu.roll(zn, HEAD_DIM // 2, axis=1))
    return zn * cos + partner * sin_signed


def _qkv_body(x_ref, g_ref, w_ref, ones_ref, qgain_ref, kgain_ref, cos_ref, sin_ref,
              q_ref, kv_ref):
    tm = x_ref.shape[0]
    qw = q_ref.shape[1]
    tn = ones_ref.shape[0]
    kw = N_KV_HEADS * HEAD_DIM
    chunk = min(tm, QKV_ROW_CHUNK)
    for c in range(tm // chunk):
        rows = pl.ds(c * chunk, chunk)
        h = _rms_norm(x_ref[rows, :], g_ref[...]).astype(BF16)
        cos, sin_signed = cos_ref[rows, :], sin_ref[rows, :]
        for t in range(qw // tn):
            cols = pl.ds(t * tn, tn)
            z = _dot(h, w_ref[:, cols])
            q = _head_norm_rope(z, qgain_ref[...], ones_ref[...], cos, sin_signed,
                                split_ssq=False)
            q_ref[rows, cols] = (q * HEAD_DIM ** -0.5).astype(BF16)
        z = _dot(h, w_ref[:, pl.ds(qw, 2 * kw)])
        k = _head_norm_rope(z[:, :kw], kgain_ref[...], ones_ref[0:kw, 0:kw], cos, sin_signed,
                            split_ssq=True)
        kv_ref[rows, :] = jnp.concatenate([k, z[:, kw:]], axis=1)


def _qkv(x, gain, w_qkv, ones_blockdiag, q_gain, k_gain, cos, sin_signed, *, tm, rope_tiles):
    m, d = x.shape
    qw = N_HEADS * HEAD_DIM
    kvw = 2 * N_KV_HEADS * HEAD_DIM
    tn = ones_blockdiag.shape[0]
    const = lambda shape: pl.BlockSpec(shape, lambda i: (0, 0), pipeline_mode=pl.Buffered(1))
    return pl.pallas_call(
        _qkv_body,
        grid=(m // tm,),
        in_specs=[
            pl.BlockSpec((tm, d), lambda i: (i, 0)),
            const((1, d)),
            const(w_qkv.shape),
            const((tn, tn)),
            const((1, tn)),
            const((1, kvw // 2)),
            pl.BlockSpec((tm, LANES), lambda i: (i % rope_tiles, 0)),
            pl.BlockSpec((tm, LANES), lambda i: (i % rope_tiles, 0)),
        ],
        out_specs=[
            pl.BlockSpec((tm, qw), lambda i: (i, 0)),
            pl.BlockSpec((tm, kvw), lambda i: (i, 0)),
        ],
        out_shape=[
            jax.ShapeDtypeStruct((m, qw), BF16),
            jax.ShapeDtypeStruct((m, kvw), F32),
        ],
        compiler_params=_params(1, 48),
        name="qkv",
    )(x, gain, w_qkv, ones_blockdiag, q_gain, k_gain, cos, sin_signed)


MASKED = -1e30


def _attn_prompt_body(sinks_ref, q_ref, kv_prev_ref, kv_cur_ref, o_ref):
    n = pl.program_id(1)
    blk = q_ref.shape[0]
    kv = jnp.concatenate([kv_prev_ref[...], kv_cur_ref[...]], axis=0)
    kw = N_KV_HEADS * HEAD_DIM
    r = lax.broadcasted_iota(jnp.int32, (blk, 2 * blk), 0)
    c = lax.broadcasted_iota(jnp.int32, (blk, 2 * blk), 1)
    visible = (c > r + (blk - WINDOW)) & (c <= r + blk) & ((c >= blk) | (n > 0))
    lower = lax.broadcasted_iota(jnp.int32, (1, LANES), 1) < HEAD_DIM
    heads_per_col = LANES // HEAD_DIM
    for g in range(N_KV_HEADS):
        col, half = divmod(g, heads_per_col)
        kcol = kv[:, col * LANES:(col + 1) * LANES]
        vcol = kv[:, kw + col * LANES:kw + (col + 1) * LANES]
        kswap = pltpu.roll(kcol, HEAD_DIM, axis=1)
        vswap = pltpu.roll(vcol, HEAD_DIM, axis=1)
        in_lower = (kcol, vcol) if half == 0 else (kswap, vswap)
        in_upper = (kswap, vswap) if half == 0 else (kcol, vcol)
        k_lo, v_lo = (jnp.where(lower, t, 0.0).astype(BF16) for t in in_lower)
        k_hi, v_hi = (jnp.where(lower, 0.0, t).astype(BF16) for t in in_upper)
        for pair in range(GROUP // heads_per_col):
            h0 = g * GROUP + pair * heads_per_col
            lanes = pl.ds(h0 * HEAD_DIM, LANES)
            q_pair = q_ref[:, lanes]
            acc = None
            for h, k, v in ((h0, k_lo, v_lo), (h0 + 1, k_hi, v_hi)):
                s = lax.dot_general(q_pair, k, (((1,), (1,)), ((), ())),
                                    preferred_element_type=F32)
                s = jnp.where(visible, s, MASKED)
                sink = sinks_ref[h]
                m = jnp.maximum(jnp.max(s, axis=-1, keepdims=True), sink)
                p = jnp.exp(s - m)
                den = jnp.sum(p, axis=-1, keepdims=True) + jnp.exp(sink - m)
                o = _dot(p.astype(BF16), v) * (1.0 / den)
                acc = o if acc is None else acc + o
            o_ref[:, lanes] = acc.astype(BF16)


def _attn_prompt(sinks, q, kv, *, batch, seq):
    m, qw = q.shape
    kvw = kv.shape[1]
    blk = WINDOW
    nb = seq // blk
    return pl.pallas_call(
        _attn_prompt_body,
        grid_spec=pltpu.PrefetchScalarGridSpec(
            num_scalar_prefetch=1,
            grid=(batch, nb),
            in_specs=[
                pl.BlockSpec((blk, qw), lambda b, n, s: (b * nb + n, 0)),
                pl.BlockSpec((blk, kvw), lambda b, n, s: (b * nb + jnp.maximum(n - 1, 0), 0)),
                pl.BlockSpec((blk, kvw), lambda b, n, s: (b * nb + n, 0)),
            ],
            out_specs=pl.BlockSpec((blk, qw), lambda b, n, s: (b * nb + n, 0)),
        ),
        out_shape=jax.ShapeDtypeStruct((m, qw), BF16),
        compiler_params=_params(2, 32),
        name="attn_prompt",
    )(sinks, q, kv, kv)


def _attn_sample_body(sinks_ref, q_ref, kv_ref, ck_ref, cv_ref, o_ref, *, seq_len):
    nseq, win, kw = ck_ref.shape
    rows = GROUP * seq_len
    q = q_ref[...].astype(F32)
    kv_new = kv_ref[...]
    pad = jnp.zeros((nseq, win - seq_len, HEAD_DIM), F32)
    step = lax.broadcasted_iota(jnp.int32, (1, rows, 2 * win), 1) % seq_len
    c = lax.broadcasted_iota(jnp.int32, (1, rows, 2 * win), 2)
    visible = ((c < win) & (c > step + (win - WINDOW))) | ((c >= win) & (c - win <= step))
    pieces = []
    for g in range(N_KV_HEADS):
        lo, hi = g * HEAD_DIM, (g + 1) * HEAD_DIM
        heads = range(g * GROUP, (g + 1) * GROUP)
        k_new = kv_new[:, lo:hi].reshape(nseq, seq_len, HEAD_DIM)
        v_new = kv_new[:, kw + lo:kw + hi].reshape(nseq, seq_len, HEAD_DIM)
        k = jnp.concatenate([ck_ref[:, :, lo:hi], k_new, pad], axis=1).astype(BF16)
        v = jnp.concatenate([cv_ref[:, :, lo:hi], v_new, pad], axis=1).astype(BF16)
        qg = jnp.concatenate(
            [q[:, h * HEAD_DIM:(h + 1) * HEAD_DIM].reshape(nseq, seq_len, HEAD_DIM)
             for h in heads], axis=1).astype(BF16)
        s = jnp.einsum("bqd,bkd->bqk", qg, k, preferred_element_type=F32)
        s = jnp.where(visible, s, MASKED)
        sink_col = jnp.concatenate(
            [jnp.full((1, seq_len, 1), sinks_ref[h], F32) for h in heads], axis=1)
        m = jnp.maximum(jnp.max(s, axis=-1, keepdims=True), sink_col)
        p = jnp.exp(s - m)
        den = jnp.sum(p, axis=-1, keepdims=True) + jnp.exp(sink_col - m)
        o = jnp.einsum("bqk,bkd->bqd", p.astype(BF16), v, preferred_element_type=F32) / den
        pieces += [o[:, t * seq_len:(t + 1) * seq_len].reshape(nseq * seq_len, HEAD_DIM)
                   for t in range(GROUP)]
    o_ref[...] = jnp.concatenate(pieces, axis=1).astype(BF16)


def _attn_sample(sinks, q, kv, cache_k, cache_v, *, seq_len):
    m, qw = q.shape
    body = functools.partial(_attn_sample_body, seq_len=seq_len)
    full = lambda shape: pl.BlockSpec(shape, lambda i, s: (0,) * len(shape))
    return pl.pallas_call(
        body,
        grid_spec=pltpu.PrefetchScalarGridSpec(
            num_scalar_prefetch=1,
            grid=(1,),
            in_specs=[full(q.shape), full(kv.shape), full(cache_k.shape), full(cache_v.shape)],
            out_specs=full((m, qw)),
        ),
        out_shape=jax.ShapeDtypeStruct((m, qw), BF16),
        compiler_params=_params(1, 48),
        name="attn_sample",
    )(sinks, q, kv, cache_k, cache_v)


def _rope_tables(pos):
    half = HEAD_DIM // 2
    inv = ROPE_THETA ** (-jnp.arange(half, dtype=F32) / half)
    ang = pos.astype(F32)[:, None] * inv[None, :]
    cos, sin = jnp.cos(ang), jnp.sin(ang)
    reps = LANES // HEAD_DIM
    cos = jnp.tile(jnp.concatenate([cos, cos], axis=1), (1, reps))
    sin_signed = jnp.tile(jnp.concatenate([-sin, sin], axis=1), (1, reps))
    return cos, sin_signed


def kernel(x_prompt, x_sample, state_conv, cache_k_win, cache_v_win, ln_mix, ln_mlp,
           w_conv_in, w_conv, w_conv_out, w_qkv, w_attn_out, q_norm, k_norm, sinks,
           w_up, w_down):
    bp, tp, d = x_prompt.shape
    bs, ts, _ = x_sample.shape
    win = cache_k_win.shape[2]
    kw = N_KV_HEADS * HEAD_DIM
    assert ts == SUBLANES and win == WINDOW and tp % WINDOW == 0

    tm_p = 512
    tm_conv = 1024
    m_s = bs * ts

    xp = x_prompt.reshape(bp * tp, d)
    xs = x_sample.reshape(m_s, d)

    tn_conv = 256
    w_in = _tile_in_projection(w_conv_in[0], tn_conv)
    gain = ln_mix[0][None]
    gate_p, u_tail, wu0, wd0, w_out = _conv_in_prompt(
        xp, gain, w_in, w_conv[0], seq=tp, tm=tm_conv, tn=tn_conv,
        cast=[(w_up, 0), (w_down, 0), (w_conv_out, 0)])
    past = state_conv[0]
    zeros = lambda n: jnp.zeros((bs, n, d), F32)
    p1 = jnp.concatenate([past[:, 1:2], zeros(ts - 1)], axis=1).reshape(m_s, d)
    p2 = jnp.concatenate([past, zeros(ts - 2)], axis=1).reshape(m_s, d)
    gate_s, u_s = _conv_in_sample(xs, gain, w_in, w_conv[0], p1, p2, seq_len=ts, tn=tn_conv)
    xp = _proj_residual(xp, gate_p, w_out, tm=tm_p)
    xs = _proj_residual(xs, gate_s, w_out, tm=m_s)
    tiles_per_seq = tp // tm_conv
    new_conv_prompt = u_tail[tiles_per_seq - 1::tiles_per_seq, SUBLANES - (CONV_WIDTH - 1):][None]
    new_conv_sample = u_s.reshape(bs, ts, d)[:, ts - (CONV_WIDTH - 1):][None]

    xp, wu1, wd1, wqkv, wo = _mlp(
        xp, ln_mlp[0][None], wu0, wd0, tm=tm_p, tf=1024,
        cast=[(w_up, 1), (w_down, 1), (w_qkv, 0), (w_attn_out, 0)])
    xs, = _mlp(xs, ln_mlp[0][None], wu0, wd0, tm=m_s, tf=2048)

    gain = ln_mix[1][None]
    tn = 2 * kw
    head_id = jnp.arange(tn) // HEAD_DIM
    ones_blockdiag = (head_id[:, None] == head_id[None, :]).astype(BF16)
    q_gain = jnp.tile(q_norm[0], tn // HEAD_DIM)[None]
    k_gain = jnp.tile(k_norm[0], kw // HEAD_DIM)[None]
    cos_p, sin_p = _rope_tables(jnp.arange(tp, dtype=jnp.int32))
    cos_s, sin_s = _rope_tables(PAST_LEN + jnp.arange(ts, dtype=jnp.int32))
    cos_s, sin_s = jnp.tile(cos_s, (bs, 1)), jnp.tile(sin_s, (bs, 1))
    q_p, kv_p = _qkv(xp, gain, wqkv, ones_blockdiag, q_gain, k_gain, cos_p, sin_p,
                     tm=tm_p, rope_tiles=tp // tm_p)
    q_s, kv_s = _qkv(xs, gain, wqkv, ones_blockdiag, q_gain, k_gain, cos_s, sin_s,
                     tm=m_s, rope_tiles=1)
    o_p = _attn_prompt(sinks[0], q_p, kv_p, batch=bp, seq=tp)
    ck = cache_k_win[0].reshape(bs, win, kw)
    cv = cache_v_win[0].reshape(bs, win, kw)
    o_s = _attn_sample(sinks[0], q_s, kv_s, ck, cv, seq_len=ts)
    xp = _proj_residual(xp, o_p, wo, tm=tm_p)
    xs = _proj_residual(xs, o_s, wo, tm=m_s)

    kv_p3 = kv_p.reshape(bp, tp, 2 * kw)[:, tp - WINDOW:]
    new_k_prompt = kv_p3[:, :, :kw].reshape(1, bp, WINDOW, N_KV_HEADS, HEAD_DIM)
    new_v_prompt = kv_p3[:, :, kw:].reshape(1, bp, WINDOW, N_KV_HEADS, HEAD_DIM)
    kv_s3 = kv_s.reshape(bs, ts, 2 * kw)
    new_k_sample = jnp.concatenate([ck[:, ts:], kv_s3[:, :, :kw]], axis=1)
    new_v_sample = jnp.concatenate([cv[:, ts:], kv_s3[:, :, kw:]], axis=1)
    new_k_sample = new_k_sample.reshape(1, bs, win, N_KV_HEADS, HEAD_DIM)
    new_v_sample = new_v_sample.reshape(1, bs, win, N_KV_HEADS, HEAD_DIM)

    xp, = _mlp(xp, ln_mlp[1][None], wu1, wd1, tm=tm_p, tf=2048)
    xs, = _mlp(xs, ln_mlp[1][None], wu1, wd1, tm=m_s, tf=2048)

    return (xp.reshape(bp, tp, d), xs.reshape(bs, ts, d), new_conv_prompt, new_conv_sample,
            new_k_prompt, new_v_prompt, new_k_sample, new_v_sample)
```

```python
import functools

import jax
import jax.numpy as jnp
from jax import lax
from jax.experimental import pallas as pl
from jax.experimental.pallas import tpu as pltpu

F32 = jnp.float32
BF16 = jnp.bfloat16

HEAD_DIM = 64
N_HEADS = 32
N_KV_HEADS = 4
GROUP = N_HEADS // N_KV_HEADS
WINDOW = 128
PAST_LEN = 16384
ROPE_THETA = 10000.0
EPS = 1e-6
CONV_WIDTH = 3

SUBLANES = 8
LANES = 128
MIB = 1024 * 1024
QKV_ROW_CHUNK = 256
CONV_ROW_CHUNK = 256
MLP_NORM_ROW_CHUNK = 256
MLP_FF_CHUNK = 1024


def _params(n_grid_dims, vmem_mib):
    return pltpu.CompilerParams(
        dimension_semantics=("arbitrary",) * n_grid_dims,
        vmem_limit_bytes=vmem_mib * MIB,
    )


def _rms_norm(x, gain):
    return x * lax.rsqrt(jnp.mean(x * x, axis=-1, keepdims=True) + EPS) * gain


def _dot(a, b):
    return jnp.dot(a, b, preferred_element_type=F32)


def _cast_specs(weights, grid):
    n_steps = 1
    for g in grid:
        n_steps *= g

    def step_of(*ids):
        step = ids[0]
        for g, idx in zip(grid[1:], ids[1:len(grid)]):
            step = step * g + idx
        return step

    in_specs, out_specs, out_shapes = [], [], []
    for w, layer in weights:
        rows, cols = w.shape[1:]
        assert rows % (n_steps * 2 * SUBLANES) == 0, (w.shape, n_steps)
        rb = rows // n_steps
        in_specs.append(pl.BlockSpec((None, rb, cols),
                                     lambda *ids, layer=layer: (layer, step_of(*ids), 0)))
        out_specs.append(pl.BlockSpec((rb, cols), lambda *ids: (step_of(*ids), 0)))
        out_shapes.append(jax.ShapeDtypeStruct((rows, cols), BF16))
    return in_specs, out_specs, out_shapes


def _cast_blocks(src_refs, dst_refs):
    for src, dst in zip(src_refs, dst_refs):
        dst[...] = src[...].astype(BF16)


def _mlp_body(x_ref, g_ref, wu_ref, wd_ref, *rest, n_cast, ff_chunk):
    cast_src, (o_ref,), cast_dst, (h_ref,) = (
        rest[:n_cast], rest[n_cast:n_cast + 1], rest[n_cast + 1:2 * n_cast + 1],
        rest[2 * n_cast + 1:])

    _cast_blocks(cast_src, cast_dst)
    tm = x_ref.shape[0]
    tf = wu_ref.shape[1]

    def add_ffn(rows, h, base):
        for t in range(tf // ff_chunk):
            cols = pl.ds(t * ff_chunk, ff_chunk)
            a = jnp.maximum(_dot(h, wu_ref[:, cols]), 0.0)
            part = _dot((a * a).astype(BF16), wd_ref[cols, :])
            if base is None:
                o_ref[rows, :] += part
            else:
                o_ref[rows, :] = base + part
                base = None

    @pl.when(pl.program_id(1) == 0)
    def _():
        chunk = min(tm, MLP_NORM_ROW_CHUNK)
        for c in range(tm // chunk):
            rows = pl.ds(c * chunk, chunk)
            x = x_ref[rows, :]
            h = _rms_norm(x, g_ref[...]).astype(BF16)
            h_ref[rows, :] = h
            add_ffn(rows, h, x)

    @pl.when(pl.program_id(1) != 0)
    def _():
        add_ffn(pl.ds(0, tm), h_ref[...], None)


def _mlp(x, gain, w_up, w_down, *, tm, tf, cast=()):
    m, d = x.shape
    ff = w_up.shape[1]
    grid = (m // tm, ff // tf)
    cast_in, cast_out, cast_shapes = _cast_specs(cast, grid)
    return pl.pallas_call(
        functools.partial(_mlp_body, n_cast=len(cast), ff_chunk=min(tf, MLP_FF_CHUNK)),
        grid=grid,
        in_specs=[
            pl.BlockSpec((tm, d), lambda i, f: (i, 0)),
            pl.BlockSpec((1, d), lambda i, f: (0, 0)),
            pl.BlockSpec((d, tf), lambda i, f: (0, f)),
            pl.BlockSpec((tf, d), lambda i, f: (f, 0)),
        ] + cast_in,
        out_specs=[pl.BlockSpec((tm, d), lambda i, f: (i, 0))] + cast_out,
        out_shape=[jax.ShapeDtypeStruct((m, d), F32)] + cast_shapes,
        scratch_shapes=[pltpu.VMEM((tm, d), BF16)],
        compiler_params=_params(2, 56),
        name="mlp",
    )(x, gain, w_up, w_down, *[w for w, _ in cast])


def _proj_body(x_ref, a_ref, w_ref, o_ref):
    o_ref[...] = x_ref[...] + _dot(a_ref[...], w_ref[...])


def _proj_residual(x, a, w, *, tm):
    m, d = x.shape
    k = a.shape[1]
    return pl.pallas_call(
        _proj_body,
        grid=(m // tm,),
        in_specs=[
            pl.BlockSpec((tm, d), lambda i: (i, 0)),
            pl.BlockSpec((tm, k), lambda i: (i, 0)),
            pl.BlockSpec((k, d), lambda i: (0, 0), pipeline_mode=pl.Buffered(1)),
        ],
        out_specs=pl.BlockSpec((tm, d), lambda i: (i, 0)),
        out_shape=jax.ShapeDtypeStruct((m, d), F32),
        compiler_params=_params(1, 48),
        name="proj_residual",
    )(x, a, w)


def _conv_taps(u, u1, u2, w):
    return w[0:1, :] * u2 + w[1:2, :] * u1 + w[2:3, :] * u


def _conv_in_prompt_body(x_ref, g_ref, wb_ref, wc_ref, wv_ref, wconv_ref, *rest,
                         tiles_per_seq, n_cast):
    cast_src, (gate_ref, utail_ref), cast_dst, (h_ref, carry_ref) = (
        rest[:n_cast], rest[n_cast:n_cast + 2], rest[n_cast + 2:2 * n_cast + 2],
        rest[2 * n_cast + 2:])
    _cast_blocks(cast_src, cast_dst)
    i = pl.program_id(0)
    j = pl.program_id(1)

    @pl.when(i % tiles_per_seq == 0)
    def _():
        carry_ref[j] = jnp.zeros(carry_ref.shape[1:], F32)

    tm = h_ref.shape[0]
    chunk = min(tm, CONV_ROW_CHUNK)
    row8 = lax.broadcasted_iota(jnp.int32, (SUBLANES, 1), 0)
    w_conv = wconv_ref[...]

    def run(first_column_tile):
        prev = carry_ref[j]
        for c in range(tm // chunk):
            rows = pl.ds(c * chunk, chunk)
            if first_column_tile:
                h = _rms_norm(x_ref[rows, :], g_ref[...]).astype(BF16)
                h_ref[rows, :] = h
            else:
                h = h_ref[rows, :]
            u = _dot(h, wc_ref[...]) * _dot(h, wv_ref[...])
            u1 = pltpu.roll(u, 1, axis=0)
            u2 = pltpu.roll(u, 2, axis=0)
            top1 = jnp.where(row8 < 1, pltpu.roll(prev, 1, axis=0), u1[0:SUBLANES])
            top2 = jnp.where(row8 < 2, pltpu.roll(prev, 2, axis=0), u2[0:SUBLANES])
            u1 = jnp.concatenate([top1, u1[SUBLANES:]], axis=0)
            u2 = jnp.concatenate([top2, u2[SUBLANES:]], axis=0)
            conv = _conv_taps(u, u1, u2, w_conv)
            gate_ref[rows, :] = (_dot(h, wb_ref[...]) * conv).astype(BF16)
            prev = u[chunk - SUBLANES:]
        carry_ref[j] = prev
        utail_ref[0] = prev

    pl.when(j == 0)(functools.partial(run, True))
    pl.when(j != 0)(functools.partial(run, False))


def _conv_in_sample_body(x_ref, g_ref, wb_ref, wc_ref, wv_ref, wconv_ref, p1_ref, p2_ref,
                         gate_ref, u_ref, h_ref, *, seq_len):
    j = pl.program_id(1)

    @pl.when(j == 0)
    def _():
        h_ref[...] = _rms_norm(x_ref[...], g_ref[...]).astype(BF16)

    h = h_ref[...]
    u = _dot(h, wc_ref[...]) * _dot(h, wv_ref[...])
    step = lax.broadcasted_iota(jnp.int32, (u.shape[0], 1), 0) % seq_len
    u1 = jnp.where(step < 1, p1_ref[...], pltpu.roll(u, 1, axis=0))
    u2 = jnp.where(step < 2, p2_ref[...], pltpu.roll(u, 2, axis=0))
    conv = _conv_taps(u, u1, u2, wconv_ref[...])
    u_ref[...] = u
    gate_ref[...] = (_dot(h, wb_ref[...]) * conv).astype(BF16)


def _conv_in_specs(tm, d, tn):
    third = d // tn
    return [
        pl.BlockSpec((tm, d), lambda i, j: (i, 0)),
        pl.BlockSpec((1, d), lambda i, j: (0, 0)),
        pl.BlockSpec((d, tn), lambda i, j: (0, j)),
        pl.BlockSpec((d, tn), lambda i, j: (0, third + j)),
        pl.BlockSpec((d, tn), lambda i, j: (0, 2 * third + j)),
        pl.BlockSpec((CONV_WIDTH, tn), lambda i, j: (0, j)),
    ]


def _conv_in_prompt(x, gain, w_in, w_conv, *, seq, tm, tn, cast=()):
    m, d = x.shape
    grid = (m // tm, d // tn)
    cast_in, cast_out, cast_shapes = _cast_specs(cast, grid)
    body = functools.partial(_conv_in_prompt_body, tiles_per_seq=seq // tm, n_cast=len(cast))
    return pl.pallas_call(
        body,
        grid=grid,
        in_specs=_conv_in_specs(tm, d, tn) + cast_in,
        out_specs=[
            pl.BlockSpec((tm, tn), lambda i, j: (i, j)),
            pl.BlockSpec((1, SUBLANES, tn), lambda i, j: (i, 0, j)),
        ] + cast_out,
        out_shape=[
            jax.ShapeDtypeStruct((m, d), BF16),
            jax.ShapeDtypeStruct((m // tm, SUBLANES, d), F32),
        ] + cast_shapes,
        scratch_shapes=[
            pltpu.VMEM((tm, d), BF16),
            pltpu.VMEM((d // tn, SUBLANES, tn), F32),
        ],
        compiler_params=_params(2, 52),
        name="conv_in_prompt",
    )(x, gain, w_in, w_in, w_in, w_conv, *[w for w, _ in cast])


def _conv_in_sample(x, gain, w_in, w_conv, p1, p2, *, seq_len, tn):
    m, d = x.shape
    body = functools.partial(_conv_in_sample_body, seq_len=seq_len)
    return pl.pallas_call(
        body,
        grid=(1, d // tn),
        in_specs=_conv_in_specs(m, d, tn) + [
            pl.BlockSpec((m, tn), lambda i, j: (0, j)),
            pl.BlockSpec((m, tn), lambda i, j: (0, j)),
        ],
        out_specs=[
            pl.BlockSpec((m, tn), lambda i, j: (0, j)),
            pl.BlockSpec((m, tn), lambda i, j: (0, j)),
        ],
        out_shape=[
            jax.ShapeDtypeStruct((m, d), BF16),
            jax.ShapeDtypeStruct((m, d), F32),
        ],
        scratch_shapes=[pltpu.VMEM((m, d), BF16)],
        compiler_params=_params(2, 48),
        name="conv_in_sample",
    )(x, gain, w_in, w_in, w_in, w_conv, p1, p2)


def _head_norm_rope(z, gain, ones_blockdiag, cos, sin_signed, *, split_ssq):
    w = z.shape[1]
    zz = z * z
    hi = zz.astype(BF16)
    ssq = _dot(hi, ones_blockdiag)
    if split_ssq:
        ssq += _dot((zz - hi.astype(F32)).astype(BF16), ones_blockdiag)
    zn = z * lax.rsqrt(ssq * (1.0 / HEAD_DIM) + EPS) * gain
    reps = w // cos.shape[1]
    cos = jnp.concatenate([cos] * reps, axis=1)
    sin_signed = jnp.concatenate([sin_signed] * reps, axis=1)
    lane = lax.broadcasted_iota(jnp.int32, (1, w), 1)
    first_half = (lane % HEAD_DIM) < (HEAD_DIM // 2)
    partner = jnp.where(first_half,
                        pltpu.roll(zn, w - HEAD_DIM // 2, axis=1),
                        pltpu.roll(zn, HEAD_DIM // 2, axis=1))
    return zn * cos + partner * sin_signed


def _qkv_body(x_ref, g_ref, w_ref, ones_ref, qgain_ref, kgain_ref, cos_ref, sin_ref,
              q_ref, kv_ref, kv_tail_ref, *, seq_tiles):
    tm = x_ref.shape[0]
    qw = q_ref.shape[1]
    tn = ones_ref.shape[0]
    kw = N_KV_HEADS * HEAD_DIM
    chunk = min(tm, QKV_ROW_CHUNK)
    for c in range(tm // chunk):
        rows = pl.ds(c * chunk, chunk)
        h = _rms_norm(x_ref[rows, :], g_ref[...]).astype(BF16)
        cos, sin_signed = cos_ref[rows, :], sin_ref[rows, :]
        for t in range(qw // tn):
            cols = pl.ds(t * tn, tn)
            z = _dot(h, w_ref[:, cols])
            q = _head_norm_rope(z, qgain_ref[...], ones_ref[...], cos, sin_signed,
                                split_ssq=False)
            q_ref[rows, cols] = (q * HEAD_DIM ** -0.5).astype(BF16)
        z = _dot(h, w_ref[:, pl.ds(qw, 2 * kw)])
        k = _head_norm_rope(z[:, :kw], kgain_ref[...], ones_ref[0:kw, 0:kw], cos, sin_signed,
                            split_ssq=True)
        kv_ref[rows, :] = jnp.concatenate([k, z[:, kw:]], axis=1)

    @pl.when(pl.program_id(0) % seq_tiles == seq_tiles - 1)
    def _():
        kv_tail_ref[...] = kv_ref[tm - WINDOW:, :]


def _qkv(x, gain, w_qkv, ones_blockdiag, q_gain, k_gain, cos, sin_signed, *, tm, seq_tiles):
    m, d = x.shape
    qw = N_HEADS * HEAD_DIM
    kvw = 2 * N_KV_HEADS * HEAD_DIM
    tn = ones_blockdiag.shape[0]
    rope_tiles = seq_tiles
    const = lambda shape: pl.BlockSpec(shape, lambda i: (0, 0), pipeline_mode=pl.Buffered(1))
    return pl.pallas_call(
        functools.partial(_qkv_body, seq_tiles=seq_tiles),
        grid=(m // tm,),
        in_specs=[
            pl.BlockSpec((tm, d), lambda i: (i, 0)),
            const((1, d)),
            const(w_qkv.shape),
            const((tn, tn)),
            const((1, tn)),
            const((1, kvw // 2)),
            pl.BlockSpec((tm, LANES), lambda i: (i % rope_tiles, 0)),
            pl.BlockSpec((tm, LANES), lambda i: (i % rope_tiles, 0)),
        ],
        out_specs=[
            pl.BlockSpec((tm, qw), lambda i: (i, 0)),
            pl.BlockSpec((tm, kvw), lambda i: (i, 0)),
            pl.BlockSpec((None, WINDOW, kvw), lambda i: (i // seq_tiles, 0, 0)),
        ],
        out_shape=[
            jax.ShapeDtypeStruct((m, qw), BF16),
            jax.ShapeDtypeStruct((m, kvw), F32),
            jax.ShapeDtypeStruct((m // (tm * seq_tiles), WINDOW, kvw), F32),
        ],
        compiler_params=_params(1, 48),
        name="qkv",
    )(x, gain, w_qkv, ones_blockdiag, q_gain, k_gain, cos, sin_signed)


MASKED = -1e30


def _attn_prompt_body(sinks_ref, q_ref, kv_prev_ref, kv_cur_ref, *rest, n_cast):
    cast_src, (o_ref,), cast_dst = rest[:n_cast], rest[n_cast:n_cast + 1], rest[n_cast + 1:]
    _cast_blocks(cast_src, cast_dst)
    n = pl.program_id(1)
    blk = q_ref.shape[0]
    kv = jnp.concatenate([kv_prev_ref[...], kv_cur_ref[...]], axis=0)
    kw = N_KV_HEADS * HEAD_DIM
    r = lax.broadcasted_iota(jnp.int32, (blk, 2 * blk), 0)
    c = lax.broadcasted_iota(jnp.int32, (blk, 2 * blk), 1)
    visible = (c > r + (blk - WINDOW)) & (c <= r + blk) & ((c >= blk) | (n > 0))
    lower = lax.broadcasted_iota(jnp.int32, (1, LANES), 1) < HEAD_DIM
    heads_per_col = LANES // HEAD_DIM
    for g in range(N_KV_HEADS):
        col, half = divmod(g, heads_per_col)
        kcol = kv[:, col * LANES:(col + 1) * LANES]
        vcol = kv[:, kw + col * LANES:kw + (col + 1) * LANES]
        kswap = pltpu.roll(kcol, HEAD_DIM, axis=1)
        vswap = pltpu.roll(vcol, HEAD_DIM, axis=1)
        in_lower = (kcol, vcol) if half == 0 else (kswap, vswap)
        in_upper = (kswap, vswap) if half == 0 else (kcol, vcol)
        k_lo, v_lo = (jnp.where(lower, t, 0.0).astype(BF16) for t in in_lower)
        k_hi, v_hi = (jnp.where(lower, 0.0, t).astype(BF16) for t in in_upper)
        for pair in range(GROUP // heads_per_col):
            h0 = g * GROUP + pair * heads_per_col
            lanes = pl.ds(h0 * HEAD_DIM, LANES)
            q_pair = q_ref[:, lanes]
            acc = None
            for h, k, v in ((h0, k_lo, v_lo), (h0 + 1, k_hi, v_hi)):
                s = lax.dot_general(q_pair, k, (((1,), (1,)), ((), ())),
                                    preferred_element_type=F32)
                s = jnp.where(visible, s, MASKED)
                sink = sinks_ref[h]
                m = jnp.maximum(jnp.max(s, axis=-1, keepdims=True), sink)
                p = jnp.exp(s - m)
                den = jnp.sum(p, axis=-1, keepdims=True) + jnp.exp(sink - m)
                o = _dot(p.astype(BF16), v) * (1.0 / den)
                acc = o if acc is None else acc + o
            o_ref[:, lanes] = acc.astype(BF16)


def _attn_prompt(sinks, q, kv, *, batch, seq, cast=()):
    m, qw = q.shape
    kvw = kv.shape[1]
    blk = WINDOW
    nb = seq // blk
    grid = (batch, nb)
    cast_in, cast_out, cast_shapes = _cast_specs(cast, grid)
    return pl.pallas_call(
        functools.partial(_attn_prompt_body, n_cast=len(cast)),
        grid_spec=pltpu.PrefetchScalarGridSpec(
            num_scalar_prefetch=1,
            grid=grid,
            in_specs=[
                pl.BlockSpec((blk, qw), lambda b, n, s: (b * nb + n, 0)),
                pl.BlockSpec((blk, kvw), lambda b, n, s: (b * nb + jnp.maximum(n - 1, 0), 0)),
                pl.BlockSpec((blk, kvw), lambda b, n, s: (b * nb + n, 0)),
            ] + cast_in,
            out_specs=[pl.BlockSpec((blk, qw), lambda b, n, s: (b * nb + n, 0))] + cast_out,
        ),
        out_shape=[jax.ShapeDtypeStruct((m, qw), BF16)] + cast_shapes,
        compiler_params=_params(2, 32),
        name="attn_prompt",
    )(sinks, q, kv, kv, *[w for w, _ in cast])


def _attn_sample_body(sinks_ref, q_ref, kv_ref, ck_ref, cv_ref, o_ref, knew_ref, vnew_ref, *,
                      seq_len):
    nseq, win, kw = ck_ref.shape
    rows = GROUP * seq_len
    q = q_ref[...].astype(F32)
    kv_new = kv_ref[...]
    for new_ref, cache_ref, fresh in ((knew_ref, ck_ref, kv_new[:, :kw]),
                                      (vnew_ref, cv_ref, kv_new[:, kw:])):
        new_ref[:, :win - seq_len, :] = cache_ref[:, seq_len:, :]
        new_ref[:, win - seq_len:, :] = fresh.reshape(nseq, seq_len, kw)
    pad = jnp.zeros((nseq, win - seq_len, HEAD_DIM), F32)
    step = lax.broadcasted_iota(jnp.int32, (1, rows, 2 * win), 1) % seq_len
    c = lax.broadcasted_iota(jnp.int32, (1, rows, 2 * win), 2)
    visible = ((c < win) & (c > step + (win - WINDOW))) | ((c >= win) & (c - win <= step))
    pieces = []
    for g in range(N_KV_HEADS):
        lo, hi = g * HEAD_DIM, (g + 1) * HEAD_DIM
        heads = range(g * GROUP, (g + 1) * GROUP)
        k_new = kv_new[:, lo:hi].reshape(nseq, seq_len, HEAD_DIM)
        v_new = kv_new[:, kw + lo:kw + hi].reshape(nseq, seq_len, HEAD_DIM)
        k = jnp.concatenate([ck_ref[:, :, lo:hi], k_new, pad], axis=1).astype(BF16)
        v = jnp.concatenate([cv_ref[:, :, lo:hi], v_new, pad], axis=1).astype(BF16)
        qg = jnp.concatenate(
            [q[:, h * HEAD_DIM:(h + 1) * HEAD_DIM].reshape(nseq, seq_len, HEAD_DIM)
             for h in heads], axis=1).astype(BF16)
        s = jnp.einsum("bqd,bkd->bqk", qg, k, preferred_element_type=F32)
        s = jnp.where(visible, s, MASKED)
        sink_col = jnp.concatenate(
            [jnp.full((1, seq_len, 1), sinks_ref[h], F32) for h in heads], axis=1)
        m = jnp.maximum(jnp.max(s, axis=-1, keepdims=True), sink_col)
        p = jnp.exp(s - m)
        den = jnp.sum(p, axis=-1, keepdims=True) + jnp.exp(sink_col - m)
        o = jnp.einsum("bqk,bkd->bqd", p.astype(BF16), v, preferred_element_type=F32) / den
        pieces += [o[:, t * seq_len:(t + 1) * seq_len].reshape(nseq * seq_len, HEAD_DIM)
                   for t in range(GROUP)]
    o_ref[...] = jnp.concatenate(pieces, axis=1).astype(BF16)


def _attn_sample(sinks, q, kv, cache_k, cache_v, *, seq_len):
    m, qw = q.shape
    body = functools.partial(_attn_sample_body, seq_len=seq_len)
    full = lambda shape: pl.BlockSpec(shape, lambda i, s: (0,) * len(shape))
    return pl.pallas_call(
        body,
        grid_spec=pltpu.PrefetchScalarGridSpec(
            num_scalar_prefetch=1,
            grid=(1,),
            in_specs=[full(q.shape), full(kv.shape), full(cache_k.shape), full(cache_v.shape)],
            out_specs=[full((m, qw)), full(cache_k.shape), full(cache_v.shape)],
        ),
        out_shape=[jax.ShapeDtypeStruct((m, qw), BF16),
                   jax.ShapeDtypeStruct(cache_k.shape, F32),
                   jax.ShapeDtypeStruct(cache_v.shape, F32)],
        compiler_params=_params(1, 48),
        name="attn_sample",
    )(sinks, q, kv, cache_k, cache_v)


def _rope_tables(pos):
    half = HEAD_DIM // 2
    inv = ROPE_THETA ** (-jnp.arange(half, dtype=F32) / half)
    ang = pos.astype(F32)[:, None] * inv[None, :]
    cos, sin = jnp.cos(ang), jnp.sin(ang)
    reps = LANES // HEAD_DIM
    cos = jnp.tile(jnp.concatenate([cos, cos], axis=1), (1, reps))
    sin_signed = jnp.tile(jnp.concatenate([-sin, sin], axis=1), (1, reps))
    return cos, sin_signed


def kernel(x_prompt, x_sample, state_conv, cache_k_win, cache_v_win, ln_mix, ln_mlp,
           w_conv_in, w_conv, w_conv_out, w_qkv, w_attn_out, q_norm, k_norm, sinks,
           w_up, w_down):
    bp, tp, d = x_prompt.shape
    bs, ts, _ = x_sample.shape
    win = cache_k_win.shape[2]
    kw = N_KV_HEADS * HEAD_DIM
    assert ts == SUBLANES and win == WINDOW and tp % WINDOW == 0

    tm_p = 512
    tm_conv = 1024
    m_s = bs * ts

    xp = x_prompt.reshape(bp * tp, d)
    xs = x_sample.reshape(m_s, d)

    w_in = w_conv_in[0].astype(BF16)
    gain = ln_mix[0][None]
    gate_p, u_tail, wu0, wd0, w_out = _conv_in_prompt(
        xp, gain, w_in, w_conv[0], seq=tp, tm=tm_conv, tn=256,
        cast=[(w_up, 0), (w_down, 0), (w_conv_out, 0)])
    past = state_conv[0]
    zeros = lambda n: jnp.zeros((bs, n, d), F32)
    p1 = jnp.concatenate([past[:, 1:2], zeros(ts - 1)], axis=1).reshape(m_s, d)
    p2 = jnp.concatenate([past, zeros(ts - 2)], axis=1).reshape(m_s, d)
    gate_s, u_s = _conv_in_sample(xs, gain, w_in, w_conv[0], p1, p2, seq_len=ts, tn=512)
    xp = _proj_residual(xp, gate_p, w_out, tm=tm_p)
    xs = _proj_residual(xs, gate_s, w_out, tm=m_s)
    tiles_per_seq = tp // tm_conv
    new_conv_prompt = u_tail[tiles_per_seq - 1::tiles_per_seq, SUBLANES - (CONV_WIDTH - 1):][None]
    new_conv_sample = u_s.reshape(bs, ts, d)[:, ts - (CONV_WIDTH - 1):][None]

    xp, wqkv = _mlp(xp, ln_mlp[0][None], wu0, wd0, tm=tm_p, tf=2048, cast=[(w_qkv, 0)])
    xs, = _mlp(xs, ln_mlp[0][None], wu0, wd0, tm=m_s, tf=2048)

    gain = ln_mix[1][None]
    tn = 2 * kw
    head_id = jnp.arange(tn) // HEAD_DIM
    ones_blockdiag = (head_id[:, None] == head_id[None, :]).astype(BF16)
    q_gain = jnp.tile(q_norm[0], tn // HEAD_DIM)[None]
    k_gain = jnp.tile(k_norm[0], kw // HEAD_DIM)[None]
    cos_p, sin_p = _rope_tables(jnp.arange(tp, dtype=jnp.int32))
    cos_s, sin_s = _rope_tables(PAST_LEN + jnp.arange(ts, dtype=jnp.int32))
    cos_s, sin_s = jnp.tile(cos_s, (bs, 1)), jnp.tile(sin_s, (bs, 1))
    q_p, kv_p, kv_tail = _qkv(xp, gain, wqkv, ones_blockdiag, q_gain, k_gain, cos_p, sin_p,
                              tm=tm_p, seq_tiles=tp // tm_p)
    q_s, kv_s, _ = _qkv(xs, gain, wqkv, ones_blockdiag, q_gain, k_gain, cos_s, sin_s,
                        tm=m_s, seq_tiles=1)
    o_p, wu1, wd1, wo = _attn_prompt(sinks[0], q_p, kv_p, batch=bp, seq=tp,
                                     cast=[(w_up, 1), (w_down, 1), (w_attn_out, 0)])
    ck = cache_k_win[0].reshape(bs, win, kw)
    cv = cache_v_win[0].reshape(bs, win, kw)
    o_s, new_k_sample, new_v_sample = _attn_sample(sinks[0], q_s, kv_s, ck, cv, seq_len=ts)
    xp = _proj_residual(xp, o_p, wo, tm=tm_p)
    xs = _proj_residual(xs, o_s, wo, tm=m_s)

    new_k_prompt = kv_tail[:, :, :kw].reshape(1, bp, WINDOW, N_KV_HEADS, HEAD_DIM)
    new_v_prompt = kv_tail[:, :, kw:].reshape(1, bp, WINDOW, N_KV_HEADS, HEAD_DIM)
    new_k_sample = new_k_sample.reshape(1, bs, win, N_KV_HEADS, HEAD_DIM)
    new_v_sample = new_v_sample.reshape(1, bs, win, N_KV_HEADS, HEAD_DIM)

    xp, = _mlp(xp, ln_mlp[1][None], wu1, wd1, tm=tm_p, tf=2048)
    xs, = _mlp(xs, ln_mlp[1][None], wu1, wd1, tm=m_s, tf=2048)

    return (xp.reshape(bp, tp, d), xs.reshape(bs, ts, d), new_conv_prompt, new_conv_sample,
            new_k_prompt, new_v_prompt, new_k_sample, new_v_sample)
```

```python
import functools

import jax
import jax.numpy as jnp
from jax import lax
from jax.experimental import pallas as pl
from jax.experimental.pallas import tpu as pltpu

F32 = jnp.float32
BF16 = jnp.bfloat16

HEAD_DIM = 64
N_HEADS = 32
N_KV_HEADS = 4
GROUP = N_HEADS // N_KV_HEADS
WINDOW = 128
PAST_LEN = 16384
ROPE_THETA = 10000.0
EPS = 1e-6
CONV_WIDTH = 3

SUBLANES = 8
LANES = 128
MIB = 1024 * 1024
QKV_ROW_CHUNK = 256
CONV_ROW_CHUNK = 256
MLP_NORM_ROW_CHUNK = 256
MLP_FF_CHUNK = 1024


def _params(n_grid_dims, vmem_mib):
    return pltpu.CompilerParams(
        dimension_semantics=("arbitrary",) * n_grid_dims,
        vmem_limit_bytes=vmem_mib * MIB,
    )


def _rms_norm(x, gain):
    return x * lax.rsqrt(jnp.mean(x * x, axis=-1, keepdims=True) + EPS) * gain


def _dot(a, b):
    return jnp.dot(a, b, preferred_element_type=F32)


def _cast_specs(weights, grid):
    n_steps = 1
    for g in grid:
        n_steps *= g

    def step_of(*ids):
        step = ids[0]
        for g, idx in zip(grid[1:], ids[1:len(grid)]):
            step = step * g + idx
        return step

    in_specs, out_specs, out_shapes = [], [], []
    for w, layer in weights:
        rows, cols = w.shape[1:]
        assert rows % (n_steps * 2 * SUBLANES) == 0, (w.shape, n_steps)
        rb = rows // n_steps
        in_specs.append(pl.BlockSpec((None, rb, cols),
                                     lambda *ids, layer=layer: (layer, step_of(*ids), 0)))
        out_specs.append(pl.BlockSpec((rb, cols), lambda *ids: (step_of(*ids), 0)))
        out_shapes.append(jax.ShapeDtypeStruct((rows, cols), BF16))
    return in_specs, out_specs, out_shapes


def _cast_blocks(src_refs, dst_refs):
    for src, dst in zip(src_refs, dst_refs):
        dst[...] = src[...].astype(BF16)


class _RowParts:
    def __init__(self, refs):
        self.refs = tuple(refs)
        self.rows = sum(r.shape[0] for r in self.refs)

    def _pieces(self, start, size):
        offset = 0
        for ref in self.refs:
            lo, hi = max(start, offset), min(start + size, offset + ref.shape[0])
            if lo < hi:
                yield ref, lo - offset, lo - start, hi - lo
            offset += ref.shape[0]

    def load(self, start=0, size=None):
        size = self.rows - start if size is None else size
        pieces = [ref[at:at + n, :] for ref, at, _, n in self._pieces(start, size)]
        return pieces[0] if len(pieces) == 1 else jnp.concatenate(pieces, axis=0)

    def store(self, start, value, cols=slice(None), accumulate=False):
        for ref, at, src, n in self._pieces(start, value.shape[0]):
            if accumulate:
                ref[at:at + n, cols] += value[src:src + n]
            else:
                ref[at:at + n, cols] = value[src:src + n]


def _row_chunks(rows, chunk):
    n = max(rows // chunk, 1)
    return [(c * chunk, chunk if c < n - 1 else rows - c * chunk) for c in range(n)]


def _mlp_body(x_ref, g_ref, wu_ref, wd_ref, *rest, n_cast, n_out, ff_chunk):
    cast_src, out_refs, cast_dst, (h_ref,) = (
        rest[:n_cast], rest[n_cast:n_cast + n_out], rest[n_cast + n_out:2 * n_cast + n_out],
        rest[2 * n_cast + n_out:])
    out = _RowParts(out_refs)
    tm = x_ref.shape[0]
    tf = wu_ref.shape[1]

    def add_ffn(start, h, base):
        for t in range(tf // ff_chunk):
            cols = pl.ds(t * ff_chunk, ff_chunk)
            a = jnp.maximum(_dot(h, wu_ref[:, cols]), 0.0)
            part = _dot((a * a).astype(BF16), wd_ref[cols, :])
            if base is None:
                out.store(start, part, accumulate=True)
            else:
                out.store(start, base + part)
                base = None

    @pl.when(pl.program_id(1) == 0)
    def _():
        _cast_blocks(cast_src, cast_dst)
        for start, size in _row_chunks(tm, MLP_NORM_ROW_CHUNK):
            rows = pl.ds(start, size)
            x = x_ref[rows, :]
            h = _rms_norm(x, g_ref[...]).astype(BF16)
            h_ref[rows, :] = h
            add_ffn(start, h, x)

    @pl.when(pl.program_id(1) != 0)
    def _():
        _cast_blocks(cast_src, cast_dst)
        add_ffn(0, h_ref[...], None)


def _mlp(x, gain, w_up, w_down, *, n_tiles, tf, out_rows=None, cast=()):
    m, d = x.shape
    tm = m // n_tiles
    out_rows = (tm,) if out_rows is None else out_rows
    assert sum(out_rows) == tm
    ff = w_up.shape[1]
    grid = (n_tiles, ff // tf)
    cast_in, cast_out, cast_shapes = _cast_specs(cast, grid)
    body = functools.partial(_mlp_body, n_cast=len(cast), n_out=len(out_rows),
                             ff_chunk=min(tf, MLP_FF_CHUNK))
    return pl.pallas_call(
        body,
        grid=grid,
        in_specs=[
            pl.BlockSpec((tm, d), lambda i, f: (i, 0)),
            pl.BlockSpec((1, d), lambda i, f: (0, 0)),
            pl.BlockSpec((d, tf), lambda i, f: (0, f)),
            pl.BlockSpec((tf, d), lambda i, f: (f, 0)),
        ] + cast_in,
        out_specs=[pl.BlockSpec((r, d), lambda i, f: (i, 0)) for r in out_rows] + cast_out,
        out_shape=[jax.ShapeDtypeStruct((r * n_tiles, d), F32) for r in out_rows] + cast_shapes,
        scratch_shapes=[pltpu.VMEM((tm, d), BF16)],
        compiler_params=_params(2, 58),
        name="mlp",
    )(x, gain, w_up, w_down, *[w for w, _ in cast])


def _proj_body(*refs, n_x, n_a):
    x, a = _RowParts(refs[:n_x]), _RowParts(refs[n_x:n_x + n_a])
    w_ref, o_ref = refs[n_x + n_a:]
    o_ref[...] = x.load() + _dot(a.load(), w_ref[...])


def _proj_residual(x_parts, a_parts, w, *, n_tiles):
    d = w.shape[1]
    tm = sum(p.shape[0] for p in x_parts) // n_tiles
    assert tm == sum(p.shape[0] for p in a_parts) // n_tiles
    tile_spec = lambda p: pl.BlockSpec((p.shape[0] // n_tiles, p.shape[1]), lambda i: (i, 0))
    return pl.pallas_call(
        functools.partial(_proj_body, n_x=len(x_parts), n_a=len(a_parts)),
        grid=(n_tiles,),
        in_specs=[tile_spec(p) for p in (*x_parts, *a_parts)] + [
            pl.BlockSpec(w.shape, lambda i: (0, 0), pipeline_mode=pl.Buffered(1))],
        out_specs=pl.BlockSpec((tm, d), lambda i: (i, 0)),
        out_shape=jax.ShapeDtypeStruct((tm * n_tiles, d), F32),
        compiler_params=_params(1, 48),
        name="proj_residual",
    )(*x_parts, *a_parts, w)


def _conv_taps(u, u1, u2, w):
    return w[0:1, :] * u2 + w[1:2, :] * u1 + w[2:3, :] * u


def _conv_in_prompt_body(x_ref, g_ref, wb_ref, wc_ref, wv_ref, wconv_ref, *rest,
                         tiles_per_seq, n_cast):
    cast_src, (gate_ref, utail_ref), cast_dst, (h_ref, carry_ref) = (
        rest[:n_cast], rest[n_cast:n_cast + 2], rest[n_cast + 2:2 * n_cast + 2],
        rest[2 * n_cast + 2:])
    i = pl.program_id(0)
    j = pl.program_id(1)

    @pl.when(i % tiles_per_seq == 0)
    def _():
        carry_ref[j] = jnp.zeros(carry_ref.shape[1:], F32)

    tm = h_ref.shape[0]
    chunk = min(tm, CONV_ROW_CHUNK)
    row8 = lax.broadcasted_iota(jnp.int32, (SUBLANES, 1), 0)
    w_conv = wconv_ref[...]

    def run(first_column_tile):
        _cast_blocks(cast_src, cast_dst)
        prev = carry_ref[j]
        for c in range(tm // chunk):
            rows = pl.ds(c * chunk, chunk)
            if first_column_tile:
                h = _rms_norm(x_ref[rows, :], g_ref[...]).astype(BF16)
                h_ref[rows, :] = h
            else:
                h = h_ref[rows, :]
            u = _dot(h, wc_ref[...]) * _dot(h, wv_ref[...])
            u1 = pltpu.roll(u, 1, axis=0)
            u2 = pltpu.roll(u, 2, axis=0)
            top1 = jnp.where(row8 < 1, pltpu.roll(prev, 1, axis=0), u1[0:SUBLANES])
            top2 = jnp.where(row8 < 2, pltpu.roll(prev, 2, axis=0), u2[0:SUBLANES])
            u1 = jnp.concatenate([top1, u1[SUBLANES:]], axis=0)
            u2 = jnp.concatenate([top2, u2[SUBLANES:]], axis=0)
            conv = _conv_taps(u, u1, u2, w_conv)
            gate_ref[rows, :] = (_dot(h, wb_ref[...]) * conv).astype(BF16)
            prev = u[chunk - SUBLANES:]
        carry_ref[j] = prev
        utail_ref[0] = prev

    pl.when(j == 0)(functools.partial(run, True))
    pl.when(j != 0)(functools.partial(run, False))


def _conv_in_sample_body(x_ref, g_ref, wb_ref, wc_ref, wv_ref, wconv_ref, p1_ref, p2_ref,
                         gate_ref, u_ref, h_ref, *, seq_len):
    j = pl.program_id(1)

    @pl.when(j == 0)
    def _():
        h_ref[...] = _rms_norm(x_ref[...], g_ref[...]).astype(BF16)

    h = h_ref[...]
    u = _dot(h, wc_ref[...]) * _dot(h, wv_ref[...])
    step = lax.broadcasted_iota(jnp.int32, (u.shape[0], 1), 0) % seq_len
    u1 = jnp.where(step < 1, p1_ref[...], pltpu.roll(u, 1, axis=0))
    u2 = jnp.where(step < 2, p2_ref[...], pltpu.roll(u, 2, axis=0))
    conv = _conv_taps(u, u1, u2, wconv_ref[...])
    u_ref[...] = u
    gate_ref[...] = (_dot(h, wb_ref[...]) * conv).astype(BF16)


def _conv_in_specs(tm, d, tn):
    third = d // tn
    return [
        pl.BlockSpec((tm, d), lambda i, j: (i, 0)),
        pl.BlockSpec((1, d), lambda i, j: (0, 0)),
        pl.BlockSpec((d, tn), lambda i, j: (0, j)),
        pl.BlockSpec((d, tn), lambda i, j: (0, third + j)),
        pl.BlockSpec((d, tn), lambda i, j: (0, 2 * third + j)),
        pl.BlockSpec((CONV_WIDTH, tn), lambda i, j: (0, j)),
    ]


def _conv_in_prompt(x, gain, w_in, w_conv, *, seq, tm, tn, cast=()):
    m, d = x.shape
    grid = (m // tm, d // tn)
    cast_in, cast_out, cast_shapes = _cast_specs(cast, grid)
    body = functools.partial(_conv_in_prompt_body, tiles_per_seq=seq // tm, n_cast=len(cast))
    return pl.pallas_call(
        body,
        grid=grid,
        in_specs=_conv_in_specs(tm, d, tn) + cast_in,
        out_specs=[
            pl.BlockSpec((tm, tn), lambda i, j: (i, j)),
            pl.BlockSpec((1, SUBLANES, tn), lambda i, j: (i, 0, j)),
        ] + cast_out,
        out_shape=[
            jax.ShapeDtypeStruct((m, d), BF16),
            jax.ShapeDtypeStruct((m // tm, SUBLANES, d), F32),
        ] + cast_shapes,
        scratch_shapes=[
            pltpu.VMEM((tm, d), BF16),
            pltpu.VMEM((d // tn, SUBLANES, tn), F32),
        ],
        compiler_params=_params(2, 52),
        name="conv_in_prompt",
    )(x, gain, w_in, w_in, w_in, w_conv, *[w for w, _ in cast])


def _conv_in_sample(x, gain, w_in, w_conv, p1, p2, *, seq_len, tn):
    m, d = x.shape
    body = functools.partial(_conv_in_sample_body, seq_len=seq_len)
    return pl.pallas_call(
        body,
        grid=(1, d // tn),
        in_specs=_conv_in_specs(m, d, tn) + [
            pl.BlockSpec((m, tn), lambda i, j: (0, j)),
            pl.BlockSpec((m, tn), lambda i, j: (0, j)),
        ],
        out_specs=[
            pl.BlockSpec((m, tn), lambda i, j: (0, j)),
            pl.BlockSpec((m, tn), lambda i, j: (0, j)),
        ],
        out_shape=[
            jax.ShapeDtypeStruct((m, d), BF16),
            jax.ShapeDtypeStruct((m, d), F32),
        ],
        scratch_shapes=[pltpu.VMEM((m, d), BF16)],
        compiler_params=_params(2, 48),
        name="conv_in_sample",
    )(x, gain, w_in, w_in, w_in, w_conv, p1, p2)


def _head_norm_rope(z, gain, ones_blockdiag, cos, sin_signed, *, split_ssq):
    w = z.shape[1]
    zz = z * z
    hi = zz.astype(BF16)
    ssq = _dot(hi, ones_blockdiag)
    if split_ssq:
        ssq += _dot((zz - hi.astype(F32)).astype(BF16), ones_blockdiag)
    zn = z * lax.rsqrt(ssq * (1.0 / HEAD_DIM) + EPS) * gain
    reps = w // cos.shape[1]
    cos = jnp.concatenate([cos] * reps, axis=1)
    sin_signed = jnp.concatenate([sin_signed] * reps, axis=1)
    lane = lax.broadcasted_iota(jnp.int32, (1, w), 1)
    first_half = (lane % HEAD_DIM) < (HEAD_DIM // 2)
    partner = jnp.where(first_half,
                        pltpu.roll(zn, w - HEAD_DIM // 2, axis=1),
                        pltpu.roll(zn, HEAD_DIM // 2, axis=1))
    return zn * cos + partner * sin_signed


def _qkv_body(x_ref, g_ref, w_ref, ones_ref, qgain_ref, kgain_ref,
              cos_p_ref, sin_p_ref, cos_s_ref, sin_s_ref,
              qp_ref, qs_ref, kvp_ref, kvs_ref, kv_tail_ref, *, seq_tiles):
    q_out, kv_out = _RowParts((qp_ref, qs_ref)), _RowParts((kvp_ref, kvs_ref))
    cos_tile, sin_tile = _RowParts((cos_p_ref, cos_s_ref)), _RowParts((sin_p_ref, sin_s_ref))
    tm = x_ref.shape[0]
    qw = qp_ref.shape[1]
    tn = ones_ref.shape[0]
    kw = N_KV_HEADS * HEAD_DIM
    for start, size in _row_chunks(tm, QKV_ROW_CHUNK):
        h = _rms_norm(x_ref[pl.ds(start, size), :], g_ref[...]).astype(BF16)
        cos, sin_signed = cos_tile.load(start, size), sin_tile.load(start, size)
        for t in range(qw // tn):
            cols = pl.ds(t * tn, tn)
            z = _dot(h, w_ref[:, cols])
            q = _head_norm_rope(z, qgain_ref[...], ones_ref[...], cos, sin_signed,
                                split_ssq=False)
            q_out.store(start, (q * HEAD_DIM ** -0.5).astype(BF16), cols=cols)
        z = _dot(h, w_ref[:, pl.ds(qw, 2 * kw)])
        k = _head_norm_rope(z[:, :kw], kgain_ref[...], ones_ref[0:kw, 0:kw], cos, sin_signed,
                            split_ssq=True)
        kv_out.store(start, jnp.concatenate([k, z[:, kw:]], axis=1))

    @pl.when(pl.program_id(0) % seq_tiles == seq_tiles - 1)
    def _():
        kv_tail_ref[...] = kvp_ref[kvp_ref.shape[0] - WINDOW:, :]


def _qkv(x, gain, w_qkv, ones_blockdiag, q_gain, k_gain, rope_prompt, rope_sample, *,
         n_tiles, prompt_rows, seq_tiles):
    m, d = x.shape
    tm = m // n_tiles
    sample_rows = tm - prompt_rows
    qw = N_HEADS * HEAD_DIM
    kvw = 2 * N_KV_HEADS * HEAD_DIM
    tn = ones_blockdiag.shape[0]
    const = lambda shape: pl.BlockSpec(shape, lambda i: (0, 0), pipeline_mode=pl.Buffered(1))
    rope_p = pl.BlockSpec((prompt_rows, LANES), lambda i: (i % seq_tiles, 0))
    split = lambda width: [pl.BlockSpec((prompt_rows, width), lambda i: (i, 0)),
                           pl.BlockSpec((sample_rows, width), lambda i: (i, 0))]
    split_shape = lambda width, dtype: [
        jax.ShapeDtypeStruct((prompt_rows * n_tiles, width), dtype),
        jax.ShapeDtypeStruct((sample_rows * n_tiles, width), dtype)]
    return pl.pallas_call(
        functools.partial(_qkv_body, seq_tiles=seq_tiles),
        grid=(n_tiles,),
        in_specs=[
            pl.BlockSpec((tm, d), lambda i: (i, 0)),
            const((1, d)),
            const(w_qkv.shape),
            const((tn, tn)),
            const((1, tn)),
            const((1, kvw // 2)),
            rope_p, rope_p,
            const((sample_rows, LANES)), const((sample_rows, LANES)),
        ],
        out_specs=split(qw) + split(kvw) + [
            pl.BlockSpec((None, WINDOW, kvw), lambda i: (i // seq_tiles, 0, 0))],
        out_shape=split_shape(qw, BF16) + split_shape(kvw, F32) + [
            jax.ShapeDtypeStruct((n_tiles // seq_tiles, WINDOW, kvw), F32)],
        compiler_params=_params(1, 48),
        name="qkv",
    )(x, gain, w_qkv, ones_blockdiag, q_gain, k_gain, *rope_prompt, *rope_sample)


MASKED = -1e30


def _attn_prompt_body(sinks_ref, q_ref, kv_prev_ref, kv_cur_ref, *rest, n_cast):
    cast_src, (o_ref,), cast_dst = rest[:n_cast], rest[n_cast:n_cast + 1], rest[n_cast + 1:]
    _cast_blocks(cast_src, cast_dst)
    n = pl.program_id(1)
    blk = q_ref.shape[0]
    kv = jnp.concatenate([kv_prev_ref[...], kv_cur_ref[...]], axis=0)
    kw = N_KV_HEADS * HEAD_DIM
    r = lax.broadcasted_iota(jnp.int32, (blk, 2 * blk), 0)
    c = lax.broadcasted_iota(jnp.int32, (blk, 2 * blk), 1)
    visible = (c > r + (blk - WINDOW)) & (c <= r + blk) & ((c >= blk) | (n > 0))
    lower = lax.broadcasted_iota(jnp.int32, (1, LANES), 1) < HEAD_DIM
    heads_per_col = LANES // HEAD_DIM
    for g in range(N_KV_HEADS):
        col, half = divmod(g, heads_per_col)
        kcol = kv[:, col * LANES:(col + 1) * LANES]
        vcol = kv[:, kw + col * LANES:kw + (col + 1) * LANES]
        kswap = pltpu.roll(kcol, HEAD_DIM, axis=1)
        vswap = pltpu.roll(vcol, HEAD_DIM, axis=1)
        in_lower = (kcol, vcol) if half == 0 else (kswap, vswap)
        in_upper = (kswap, vswap) if half == 0 else (kcol, vcol)
        k_lo, v_lo = (jnp.where(lower, t, 0.0).astype(BF16) for t in in_lower)
        k_hi, v_hi = (jnp.where(lower, 0.0, t).astype(BF16) for t in in_upper)
        for pair in range(GROUP // heads_per_col):
            h0 = g * GROUP + pair * heads_per_col
            lanes = pl.ds(h0 * HEAD_DIM, LANES)
            q_pair = q_ref[:, lanes]
            acc = None
            for h, k, v in ((h0, k_lo, v_lo), (h0 + 1, k_hi, v_hi)):
                s = lax.dot_general(q_pair, k, (((1,), (1,)), ((), ())),
                                    preferred_element_type=F32)
                s = jnp.where(visible, s, MASKED)
                sink = sinks_ref[h]
                m = jnp.maximum(jnp.max(s, axis=-1, keepdims=True), sink)
                p = jnp.exp(s - m)
                den = jnp.sum(p, axis=-1, keepdims=True) + jnp.exp(sink - m)
                o = _dot(p.astype(BF16), v) * (1.0 / den)
                acc = o if acc is None else acc + o
            o_ref[:, lanes] = acc.astype(BF16)


def _attn_prompt(sinks, q, kv, *, batch, seq, cast=()):
    m, qw = q.shape
    kvw = kv.shape[1]
    blk = WINDOW
    nb = seq // blk
    grid = (batch, nb)
    cast_in, cast_out, cast_shapes = _cast_specs(cast, grid)
    return pl.pallas_call(
        functools.partial(_attn_prompt_body, n_cast=len(cast)),
        grid_spec=pltpu.PrefetchScalarGridSpec(
            num_scalar_prefetch=1,
            grid=grid,
            in_specs=[
                pl.BlockSpec((blk, qw), lambda b, n, s: (b * nb + n, 0)),
                pl.BlockSpec((blk, kvw), lambda b, n, s: (b * nb + jnp.maximum(n - 1, 0), 0)),
                pl.BlockSpec((blk, kvw), lambda b, n, s: (b * nb + n, 0)),
            ] + cast_in,
            out_specs=[pl.BlockSpec((blk, qw), lambda b, n, s: (b * nb + n, 0))] + cast_out,
        ),
        out_shape=[jax.ShapeDtypeStruct((m, qw), BF16)] + cast_shapes,
        compiler_params=_params(2, 32),
        name="attn_prompt",
    )(sinks, q, kv, kv, *[w for w, _ in cast])


def _attn_sample_body(sinks_ref, q_ref, kv_ref, ck_ref, cv_ref, o_ref, knew_ref, vnew_ref, *,
                      seq_len):
    nseq, win, kw = ck_ref.shape
    rows = GROUP * seq_len
    q = q_ref[...].astype(F32)
    kv_new = kv_ref[...]
    for new_ref, cache_ref, fresh in ((knew_ref, ck_ref, kv_new[:, :kw]),
                                      (vnew_ref, cv_ref, kv_new[:, kw:])):
        new_ref[:, :win - seq_len, :] = cache_ref[:, seq_len:, :]
        new_ref[:, win - seq_len:, :] = fresh.reshape(nseq, seq_len, kw)
    pad = jnp.zeros((nseq, win - seq_len, HEAD_DIM), F32)
    step = lax.broadcasted_iota(jnp.int32, (1, rows, 2 * win), 1) % seq_len
    c = lax.broadcasted_iota(jnp.int32, (1, rows, 2 * win), 2)
    visible = ((c < win) & (c > step + (win - WINDOW))) | ((c >= win) & (c - win <= step))
    pieces = []
    for g in range(N_KV_HEADS):
        lo, hi = g * HEAD_DIM, (g + 1) * HEAD_DIM
        heads = range(g * GROUP, (g + 1) * GROUP)
        k_new = kv_new[:, lo:hi].reshape(nseq, seq_len, HEAD_DIM)
        v_new = kv_new[:, kw + lo:kw + hi].reshape(nseq, seq_len, HEAD_DIM)
        k = jnp.concatenate([ck_ref[:, :, lo:hi], k_new, pad], axis=1).astype(BF16)
        v = jnp.concatenate([cv_ref[:, :, lo:hi], v_new, pad], axis=1).astype(BF16)
        qg = jnp.concatenate(
            [q[:, h * HEAD_DIM:(h + 1) * HEAD_DIM].reshape(nseq, seq_len, HEAD_DIM)
             for h in heads], axis=1).astype(BF16)
        s = jnp.einsum("bqd,bkd->bqk", qg, k, preferred_element_type=F32)
        s = jnp.where(visible, s, MASKED)
        sink_col = jnp.concatenate(
            [jnp.full((1, seq_len, 1), sinks_ref[h], F32) for h in heads], axis=1)
        m = jnp.maximum(jnp.max(s, axis=-1, keepdims=True), sink_col)
        p = jnp.exp(s - m)
        den = jnp.sum(p, axis=-1, keepdims=True) + jnp.exp(sink_col - m)
        o = jnp.einsum("bqk,bkd->bqd", p.astype(BF16), v, preferred_element_type=F32) / den
        pieces += [o[:, t * seq_len:(t + 1) * seq_len].reshape(nseq * seq_len, HEAD_DIM)
                   for t in range(GROUP)]
    o_ref[...] = jnp.concatenate(pieces, axis=1).astype(BF16)


def _attn_sample(sinks, q, kv, cache_k, cache_v, *, seq_len):
    m, qw = q.shape
    body = functools.partial(_attn_sample_body, seq_len=seq_len)
    full = lambda shape: pl.BlockSpec(shape, lambda i, s: (0,) * len(shape))
    return pl.pallas_call(
        body,
        grid_spec=pltpu.PrefetchScalarGridSpec(
            num_scalar_prefetch=1,
            grid=(1,),
            in_specs=[full(q.shape), full(kv.shape), full(cache_k.shape), full(cache_v.shape)],
            out_specs=[full((m, qw)), full(cache_k.shape), full(cache_v.shape)],
        ),
        out_shape=[jax.ShapeDtypeStruct((m, qw), BF16),
                   jax.ShapeDtypeStruct(cache_k.shape, F32),
                   jax.ShapeDtypeStruct(cache_v.shape, F32)],
        compiler_params=_params(1, 48),
        name="attn_sample",
    )(sinks, q, kv, cache_k, cache_v)


def _rope_tables(pos):
    half = HEAD_DIM // 2
    inv = ROPE_THETA ** (-jnp.arange(half, dtype=F32) / half)
    ang = pos.astype(F32)[:, None] * inv[None, :]
    cos, sin = jnp.cos(ang), jnp.sin(ang)
    reps = LANES // HEAD_DIM
    cos = jnp.tile(jnp.concatenate([cos, cos], axis=1), (1, reps))
    sin_signed = jnp.tile(jnp.concatenate([-sin, sin], axis=1), (1, reps))
    return cos, sin_signed


def kernel(x_prompt, x_sample, state_conv, cache_k_win, cache_v_win, ln_mix, ln_mlp,
           w_conv_in, w_conv, w_conv_out, w_qkv, w_attn_out, q_norm, k_norm, sinks,
           w_up, w_down):
    bp, tp, d = x_prompt.shape
    bs, ts, _ = x_sample.shape
    win = cache_k_win.shape[2]
    kw = N_KV_HEADS * HEAD_DIM
    assert ts == SUBLANES and win == WINDOW and tp % WINDOW == 0

    tm_p = 512
    tm_conv = 1024
    m_s = bs * ts
    n_tiles = bp * tp // tm_p
    rows_s = m_s // n_tiles
    assert rows_s % (2 * SUBLANES) == 0 and rows_s % ts == 0

    xp = x_prompt.reshape(bp * tp, d)
    xs = x_sample.reshape(m_s, d)

    w_in = w_conv_in[0].astype(BF16)
    gain = ln_mix[0][None]
    gate_p, u_tail, wu0, wd0, w_out = _conv_in_prompt(
        xp, gain, w_in, w_conv[0], seq=tp, tm=tm_conv, tn=256,
        cast=[(w_up, 0), (w_down, 0), (w_conv_out, 0)])
    past = state_conv[0]
    zeros = lambda n: jnp.zeros((bs, n, d), F32)
    p1 = jnp.concatenate([past[:, 1:2], zeros(ts - 1)], axis=1).reshape(m_s, d)
    p2 = jnp.concatenate([past, zeros(ts - 2)], axis=1).reshape(m_s, d)
    gate_s, u_s = _conv_in_sample(xs, gain, w_in, w_conv[0], p1, p2, seq_len=ts, tn=512)
    x = _proj_residual((xp, xs), (gate_p, gate_s), w_out, n_tiles=n_tiles)
    tiles_per_seq = tp // tm_conv
    new_conv_prompt = u_tail[tiles_per_seq - 1::tiles_per_seq, SUBLANES - (CONV_WIDTH - 1):][None]
    new_conv_sample = u_s.reshape(bs, ts, d)[:, ts - (CONV_WIDTH - 1):][None]

    x, wqkv = _mlp(x, ln_mlp[0][None], wu0, wd0, n_tiles=n_tiles, tf=2048, cast=[(w_qkv, 0)])

    gain = ln_mix[1][None]
    tn = 2 * kw
    head_id = jnp.arange(tn) // HEAD_DIM
    ones_blockdiag = (head_id[:, None] == head_id[None, :]).astype(BF16)
    q_gain = jnp.tile(q_norm[0], tn // HEAD_DIM)[None]
    k_gain = jnp.tile(k_norm[0], kw // HEAD_DIM)[None]
    rope_p = _rope_tables(jnp.arange(tp, dtype=jnp.int32))
    rope_s = _rope_tables(PAST_LEN + jnp.arange(ts, dtype=jnp.int32))
    rope_s = tuple(jnp.tile(t, (rows_s // ts, 1)) for t in rope_s)
    q_p, q_s, kv_p, kv_s, kv_tail = _qkv(
        x, gain, wqkv, ones_blockdiag, q_gain, k_gain, rope_p, rope_s,
        n_tiles=n_tiles, prompt_rows=tm_p, seq_tiles=tp // tm_p)
    o_p, wu1, wd1, wo = _attn_prompt(sinks[0], q_p, kv_p, batch=bp, seq=tp,
                                     cast=[(w_up, 1), (w_down, 1), (w_attn_out, 0)])
    ck = cache_k_win[0].reshape(bs, win, kw)
    cv = cache_v_win[0].reshape(bs, win, kw)
    o_s, new_k_sample, new_v_sample = _attn_sample(sinks[0], q_s, kv_s, ck, cv, seq_len=ts)
    x = _proj_residual((x,), (o_p, o_s), wo, n_tiles=n_tiles)

    new_k_prompt = kv_tail[:, :, :kw].reshape(1, bp, WINDOW, N_KV_HEADS, HEAD_DIM)
    new_v_prompt = kv_tail[:, :, kw:].reshape(1, bp, WINDOW, N_KV_HEADS, HEAD_DIM)
    new_k_sample = new_k_sample.reshape(1, bs, win, N_KV_HEADS, HEAD_DIM)
    new_v_sample = new_v_sample.reshape(1, bs, win, N_KV_HEADS, HEAD_DIM)

    xp, xs = _mlp(x, ln_mlp[1][None], wu1, wd1, n_tiles=n_tiles, tf=2048,
                  out_rows=(tm_p, rows_s))

    return (xp.reshape(bp, tp, d), xs.reshape(bs, ts, d), new_conv_prompt, new_conv_sample,
            new_k_prompt, new_v_prompt, new_k_sample, new_v_sample)
```

```python
import functools
import math

import jax
import jax.numpy as jnp
from jax import lax
from jax.experimental import pallas as pl
from jax.experimental.pallas import tpu as pltpu

F32 = jnp.float32
BF16 = jnp.bfloat16

HEAD_DIM = 64
N_HEADS = 32
N_KV_HEADS = 4
GROUP = N_HEADS // N_KV_HEADS
WINDOW = 128
PAST_LEN = 16384
ROPE_THETA = 10000.0
EPS = 1e-6
CONV_WIDTH = 3

SUBLANES = 8
LANES = 128
MIB = 1024 * 1024
QKV_ROW_CHUNK = 256
CONV_ROW_CHUNK = 256
MLP_NORM_ROW_CHUNK = 256
MLP_FF_CHUNK = 1024


def _params(n_grid_dims, vmem_mib):
    return pltpu.CompilerParams(
        dimension_semantics=("arbitrary",) * n_grid_dims,
        vmem_limit_bytes=vmem_mib * MIB,
    )


def _rms_norm(x, gain):
    return x * lax.rsqrt(jnp.mean(x * x, axis=-1, keepdims=True) + EPS) * gain


def _dot(a, b):
    return jnp.dot(a, b, preferred_element_type=F32)


def _cast_specs(weights, grid):
    n_steps = 1
    for g in grid:
        n_steps *= g

    def step_of(*ids):
        step = ids[0]
        for g, idx in zip(grid[1:], ids[1:len(grid)]):
            step = step * g + idx
        return step

    in_specs, out_specs, out_shapes = [], [], []
    for w, layer in weights:
        rows, cols = w.shape[1:]
        assert rows % (n_steps * 2 * SUBLANES) == 0, (w.shape, n_steps)
        rb = rows // n_steps
        in_specs.append(pl.BlockSpec((None, rb, cols),
                                     lambda *ids, layer=layer: (layer, step_of(*ids), 0)))
        out_specs.append(pl.BlockSpec((rb, cols), lambda *ids: (step_of(*ids), 0)))
        out_shapes.append(jax.ShapeDtypeStruct((rows, cols), BF16))
    return in_specs, out_specs, out_shapes


def _cast_blocks(src_refs, dst_refs):
    for src, dst in zip(src_refs, dst_refs):
        dst[...] = src[...].astype(BF16)


class _RowParts:
    def __init__(self, refs):
        self.refs = tuple(refs)
        self.rows = sum(r.shape[0] for r in self.refs)

    def _pieces(self, start, size):
        offset = 0
        for ref in self.refs:
            lo, hi = max(start, offset), min(start + size, offset + ref.shape[0])
            if lo < hi:
                yield ref, lo - offset, lo - start, hi - lo
            offset += ref.shape[0]

    def load(self, start=0, size=None):
        size = self.rows - start if size is None else size
        pieces = [ref[at:at + n, :] for ref, at, _, n in self._pieces(start, size)]
        return pieces[0] if len(pieces) == 1 else jnp.concatenate(pieces, axis=0)

    def store(self, start, value, cols=slice(None), accumulate=False):
        for ref, at, src, n in self._pieces(start, value.shape[0]):
            if accumulate:
                ref[at:at + n, cols] += value[src:src + n]
            else:
                ref[at:at + n, cols] = value[src:src + n]


def _row_chunks(rows, chunk):
    n = max(rows // chunk, 1)
    return [(c * chunk, chunk if c < n - 1 else rows - c * chunk) for c in range(n)]


def _mlp_body(x_ref, g_ref, wu_ref, wd_ref, *rest, n_cast, n_out, ff_chunk):
    cast_src, out_refs, cast_dst, (h_ref,) = (
        rest[:n_cast], rest[n_cast:n_cast + n_out], rest[n_cast + n_out:2 * n_cast + n_out],
        rest[2 * n_cast + n_out:])
    out = _RowParts(out_refs)
    tm = x_ref.shape[0]
    tf = wu_ref.shape[1]

    def add_ffn(start, h, base):
        for t in range(tf // ff_chunk):
            cols = pl.ds(t * ff_chunk, ff_chunk)
            a = jnp.maximum(_dot(h, wu_ref[:, cols]), 0.0)
            part = _dot((a * a).astype(BF16), wd_ref[cols, :])
            if base is None:
                out.store(start, part, accumulate=True)
            else:
                out.store(start, base + part)
                base = None

    @pl.when(pl.program_id(1) == 0)
    def _():
        _cast_blocks(cast_src, cast_dst)
        for start, size in _row_chunks(tm, MLP_NORM_ROW_CHUNK):
            rows = pl.ds(start, size)
            x = x_ref[rows, :]
            h = _rms_norm(x, g_ref[...]).astype(BF16)
            h_ref[rows, :] = h
            add_ffn(start, h, x)

    @pl.when(pl.program_id(1) != 0)
    def _():
        _cast_blocks(cast_src, cast_dst)
        add_ffn(0, h_ref[...], None)


def _mlp(x, gain, w_up, w_down, *, n_tiles, tf, out_rows=None, cast=()):
    m, d = x.shape
    tm = m // n_tiles
    out_rows = (tm,) if out_rows is None else out_rows
    assert sum(out_rows) == tm
    ff = w_up.shape[1]
    grid = (n_tiles, ff // tf)
    cast_in, cast_out, cast_shapes = _cast_specs(cast, grid)
    body = functools.partial(_mlp_body, n_cast=len(cast), n_out=len(out_rows),
                             ff_chunk=min(tf, MLP_FF_CHUNK))
    return pl.pallas_call(
        body,
        grid=grid,
        in_specs=[
            pl.BlockSpec((tm, d), lambda i, f: (i, 0)),
            pl.BlockSpec((1, d), lambda i, f: (0, 0)),
            pl.BlockSpec((d, tf), lambda i, f: (0, f)),
            pl.BlockSpec((tf, d), lambda i, f: (f, 0)),
        ] + cast_in,
        out_specs=[pl.BlockSpec((r, d), lambda i, f: (i, 0)) for r in out_rows] + cast_out,
        out_shape=[jax.ShapeDtypeStruct((r * n_tiles, d), F32) for r in out_rows] + cast_shapes,
        scratch_shapes=[pltpu.VMEM((tm, d), BF16)],
        compiler_params=_params(2, 58),
        name="mlp",
    )(x, gain, w_up, w_down, *[w for w, _ in cast])


def _proj_body(*refs, n_x, n_a):
    x, a = _RowParts(refs[:n_x]), _RowParts(refs[n_x:n_x + n_a])
    w_ref, o_ref = refs[n_x + n_a:]
    o_ref[...] = x.load() + _dot(a.load(), w_ref[...])


def _proj_residual(x_parts, a_parts, w, *, n_tiles):
    d = w.shape[1]
    tm = sum(p.shape[0] for p in x_parts) // n_tiles
    assert tm == sum(p.shape[0] for p in a_parts) // n_tiles
    tile_spec = lambda p: pl.BlockSpec((p.shape[0] // n_tiles, p.shape[1]), lambda i: (i, 0))
    return pl.pallas_call(
        functools.partial(_proj_body, n_x=len(x_parts), n_a=len(a_parts)),
        grid=(n_tiles,),
        in_specs=[tile_spec(p) for p in (*x_parts, *a_parts)] + [
            pl.BlockSpec(w.shape, lambda i: (0, 0), pipeline_mode=pl.Buffered(1))],
        out_specs=pl.BlockSpec((tm, d), lambda i: (i, 0)),
        out_shape=jax.ShapeDtypeStruct((tm * n_tiles, d), F32),
        compiler_params=_params(1, 48),
        name="proj_residual",
    )(*x_parts, *a_parts, w)


def _conv_taps(u, u1, u2, w):
    return w[0:1, :] * u2 + w[1:2, :] * u1 + w[2:3, :] * u


def _conv_in_prompt_body(x_ref, g_ref, wb_ref, wc_ref, wv_ref, wconv_ref, *rest,
                         tiles_per_seq, n_cast):
    cast_src, (gate_ref, utail_ref), cast_dst, (h_ref, carry_ref) = (
        rest[:n_cast], rest[n_cast:n_cast + 2], rest[n_cast + 2:2 * n_cast + 2],
        rest[2 * n_cast + 2:])
    i = pl.program_id(0)
    j = pl.program_id(1)

    @pl.when(i % tiles_per_seq == 0)
    def _():
        carry_ref[j] = jnp.zeros(carry_ref.shape[1:], F32)

    tm = h_ref.shape[0]
    chunk = min(tm, CONV_ROW_CHUNK)
    row8 = lax.broadcasted_iota(jnp.int32, (SUBLANES, 1), 0)
    w_conv = wconv_ref[...]

    def run(first_column_tile):
        _cast_blocks(cast_src, cast_dst)
        prev = carry_ref[j]
        for c in range(tm // chunk):
            rows = pl.ds(c * chunk, chunk)
            if first_column_tile:
                h = _rms_norm(x_ref[rows, :], g_ref[...]).astype(BF16)
                h_ref[rows, :] = h
            else:
                h = h_ref[rows, :]
            u = _dot(h, wc_ref[...]) * _dot(h, wv_ref[...])
            u1 = pltpu.roll(u, 1, axis=0)
            u2 = pltpu.roll(u, 2, axis=0)
            top1 = jnp.where(row8 < 1, pltpu.roll(prev, 1, axis=0), u1[0:SUBLANES])
            top2 = jnp.where(row8 < 2, pltpu.roll(prev, 2, axis=0), u2[0:SUBLANES])
            u1 = jnp.concatenate([top1, u1[SUBLANES:]], axis=0)
            u2 = jnp.concatenate([top2, u2[SUBLANES:]], axis=0)
            conv = _conv_taps(u, u1, u2, w_conv)
            gate_ref[rows, :] = (_dot(h, wb_ref[...]) * conv).astype(BF16)
            prev = u[chunk - SUBLANES:]
        carry_ref[j] = prev
        utail_ref[0] = prev

    pl.when(j == 0)(functools.partial(run, True))
    pl.when(j != 0)(functools.partial(run, False))


def _conv_in_sample_body(x_ref, g_ref, wb_ref, wc_ref, wv_ref, wconv_ref, p1_ref, p2_ref,
                         gate_ref, u_ref, h_ref, *, seq_len):
    j = pl.program_id(1)

    @pl.when(j == 0)
    def _():
        h_ref[...] = _rms_norm(x_ref[...], g_ref[...]).astype(BF16)

    h = h_ref[...]
    u = _dot(h, wc_ref[...]) * _dot(h, wv_ref[...])
    step = lax.broadcasted_iota(jnp.int32, (u.shape[0], 1), 0) % seq_len
    u1 = jnp.where(step < 1, p1_ref[...], pltpu.roll(u, 1, axis=0))
    u2 = jnp.where(step < 2, p2_ref[...], pltpu.roll(u, 2, axis=0))
    conv = _conv_taps(u, u1, u2, wconv_ref[...])
    u_ref[...] = u
    gate_ref[...] = (_dot(h, wb_ref[...]) * conv).astype(BF16)


def _conv_in_specs(tm, d, tn):
    third = d // tn
    return [
        pl.BlockSpec((tm, d), lambda i, j: (i, 0)),
        pl.BlockSpec((1, d), lambda i, j: (0, 0)),
        pl.BlockSpec((d, tn), lambda i, j: (0, j)),
        pl.BlockSpec((d, tn), lambda i, j: (0, third + j)),
        pl.BlockSpec((d, tn), lambda i, j: (0, 2 * third + j)),
        pl.BlockSpec((CONV_WIDTH, tn), lambda i, j: (0, j)),
    ]


def _conv_in_prompt(x, gain, w_in, w_conv, *, seq, tm, tn, cast=()):
    m, d = x.shape
    grid = (m // tm, d // tn)
    cast_in, cast_out, cast_shapes = _cast_specs(cast, grid)
    body = functools.partial(_conv_in_prompt_body, tiles_per_seq=seq // tm, n_cast=len(cast))
    return pl.pallas_call(
        body,
        grid=grid,
        in_specs=_conv_in_specs(tm, d, tn) + cast_in,
        out_specs=[
            pl.BlockSpec((tm, tn), lambda i, j: (i, j)),
            pl.BlockSpec((1, SUBLANES, tn), lambda i, j: (i, 0, j)),
        ] + cast_out,
        out_shape=[
            jax.ShapeDtypeStruct((m, d), BF16),
            jax.ShapeDtypeStruct((m // tm, SUBLANES, d), F32),
        ] + cast_shapes,
        scratch_shapes=[
            pltpu.VMEM((tm, d), BF16),
            pltpu.VMEM((d // tn, SUBLANES, tn), F32),
        ],
        compiler_params=_params(2, 56),
        name="conv_in_prompt",
    )(x, gain, w_in, w_in, w_in, w_conv, *[w for w, _ in cast])


def _conv_in_sample(x, gain, w_in, w_conv, p1, p2, *, seq_len, tn):
    m, d = x.shape
    body = functools.partial(_conv_in_sample_body, seq_len=seq_len)
    return pl.pallas_call(
        body,
        grid=(1, d // tn),
        in_specs=_conv_in_specs(m, d, tn) + [
            pl.BlockSpec((m, tn), lambda i, j: (0, j)),
            pl.BlockSpec((m, tn), lambda i, j: (0, j)),
        ],
        out_specs=[
            pl.BlockSpec((m, tn), lambda i, j: (0, j)),
            pl.BlockSpec((m, tn), lambda i, j: (0, j)),
        ],
        out_shape=[
            jax.ShapeDtypeStruct((m, d), BF16),
            jax.ShapeDtypeStruct((m, d), F32),
        ],
        scratch_shapes=[pltpu.VMEM((m, d), BF16)],
        compiler_params=_params(2, 48),
        name="conv_in_sample",
    )(x, gain, w_in, w_in, w_in, w_conv, p1, p2)


def _head_norm_rope(z, gain, ones_blockdiag, cos, sin_signed, *, split_ssq):
    w = z.shape[1]
    zz = z * z
    hi = zz.astype(BF16)
    ssq = _dot(hi, ones_blockdiag)
    if split_ssq:
        ssq += _dot((zz - hi.astype(F32)).astype(BF16), ones_blockdiag)
    zn = z * lax.rsqrt(ssq * (1.0 / HEAD_DIM) + EPS) * gain
    reps = w // cos.shape[1]
    cos = jnp.concatenate([cos] * reps, axis=1)
    sin_signed = jnp.concatenate([sin_signed] * reps, axis=1)
    lane = lax.broadcasted_iota(jnp.int32, (1, w), 1)
    first_half = (lane % HEAD_DIM) < (HEAD_DIM // 2)
    partner = jnp.where(first_half,
                        pltpu.roll(zn, w - HEAD_DIM // 2, axis=1),
                        pltpu.roll(zn, HEAD_DIM // 2, axis=1))
    return zn * cos + partner * sin_signed


def _qkv_body(x_ref, g_ref, w_ref, ones_ref, qgain_ref, kgain_ref,
              cos_p_ref, sin_p_ref, cos_s_ref, sin_s_ref,
              qp_ref, qs_ref, kvp_ref, kvs_ref, kv_tail_ref, *, seq_tiles):
    q_out, kv_out = _RowParts((qp_ref, qs_ref)), _RowParts((kvp_ref, kvs_ref))
    cos_tile, sin_tile = _RowParts((cos_p_ref, cos_s_ref)), _RowParts((sin_p_ref, sin_s_ref))
    tm = x_ref.shape[0]
    qw = qp_ref.shape[1]
    tn = ones_ref.shape[0]
    kw = N_KV_HEADS * HEAD_DIM
    for start, size in _row_chunks(tm, QKV_ROW_CHUNK):
        h = _rms_norm(x_ref[pl.ds(start, size), :], g_ref[...]).astype(BF16)
        cos, sin_signed = cos_tile.load(start, size), sin_tile.load(start, size)
        for t in range(qw // tn):
            cols = pl.ds(t * tn, tn)
            z = _dot(h, w_ref[:, cols])
            q = _head_norm_rope(z, qgain_ref[...], ones_ref[...], cos, sin_signed,
                                split_ssq=False)
            q_out.store(start, (q * Q_SCALE).astype(BF16), cols=cols)
        z = _dot(h, w_ref[:, pl.ds(qw, 2 * kw)])
        k = _head_norm_rope(z[:, :kw], kgain_ref[...], ones_ref[0:kw, 0:kw], cos, sin_signed,
                            split_ssq=True)
        kv_out.store(start, jnp.concatenate([k, z[:, kw:]], axis=1))

    @pl.when(pl.program_id(0) % seq_tiles == seq_tiles - 1)
    def _():
        kv_tail_ref[...] = kvp_ref[kvp_ref.shape[0] - WINDOW:, :]


def _qkv(x, gain, w_qkv, ones_blockdiag, q_gain, k_gain, rope_prompt, rope_sample, *,
         n_tiles, prompt_rows, seq_tiles):
    m, d = x.shape
    tm = m // n_tiles
    sample_rows = tm - prompt_rows
    qw = N_HEADS * HEAD_DIM
    kvw = 2 * N_KV_HEADS * HEAD_DIM
    tn = ones_blockdiag.shape[0]
    const = lambda shape: pl.BlockSpec(shape, lambda i: (0, 0), pipeline_mode=pl.Buffered(1))
    rope_p = pl.BlockSpec((prompt_rows, LANES), lambda i: (i % seq_tiles, 0))
    split = lambda width: [pl.BlockSpec((prompt_rows, width), lambda i: (i, 0)),
                           pl.BlockSpec((sample_rows, width), lambda i: (i, 0))]
    split_shape = lambda width, dtype: [
        jax.ShapeDtypeStruct((prompt_rows * n_tiles, width), dtype),
        jax.ShapeDtypeStruct((sample_rows * n_tiles, width), dtype)]
    return pl.pallas_call(
        functools.partial(_qkv_body, seq_tiles=seq_tiles),
        grid=(n_tiles,),
        in_specs=[
            pl.BlockSpec((tm, d), lambda i: (i, 0)),
            const((1, d)),
            const(w_qkv.shape),
            const((tn, tn)),
            const((1, tn)),
            const((1, kvw // 2)),
            rope_p, rope_p,
            const((sample_rows, LANES)), const((sample_rows, LANES)),
        ],
        out_specs=split(qw) + split(kvw) + [
            pl.BlockSpec((None, WINDOW, kvw), lambda i: (i // seq_tiles, 0, 0))],
        out_shape=split_shape(qw, BF16) + split_shape(kvw, F32) + [
            jax.ShapeDtypeStruct((n_tiles // seq_tiles, WINDOW, kvw), F32)],
        compiler_params=_params(1, 48),
        name="qkv",
    )(x, gain, w_qkv, ones_blockdiag, q_gain, k_gain, *rope_prompt, *rope_sample)


MASKED = -1e30
LOG2_E = math.log2(math.e)
Q_SCALE = HEAD_DIM ** -0.5 * LOG2_E


def _attn_prompt_body(sinks_ref, q_ref, kv_prev_ref, kv_cur_ref, *rest, n_cast):
    cast_src, (o_ref,), cast_dst = rest[:n_cast], rest[n_cast:n_cast + 1], rest[n_cast + 1:]
    _cast_blocks(cast_src, cast_dst)
    n = pl.program_id(1)
    blk = q_ref.shape[0]
    kv = jnp.concatenate([kv_prev_ref[...], kv_cur_ref[...]], axis=0)
    kw = N_KV_HEADS * HEAD_DIM
    r = lax.broadcasted_iota(jnp.int32, (blk, 2 * blk), 0)
    c = lax.broadcasted_iota(jnp.int32, (blk, 2 * blk), 1)
    visible = (c > r + (blk - WINDOW)) & (c <= r + blk) & ((c >= blk) | (n > 0))
    lower = lax.broadcasted_iota(jnp.int32, (1, LANES), 1) < HEAD_DIM
    heads_per_col = LANES // HEAD_DIM
    for g in range(N_KV_HEADS):
        col, half = divmod(g, heads_per_col)
        kcol = kv[:, col * LANES:(col + 1) * LANES]
        vcol = kv[:, kw + col * LANES:kw + (col + 1) * LANES]
        kswap = pltpu.roll(kcol, HEAD_DIM, axis=1)
        vswap = pltpu.roll(vcol, HEAD_DIM, axis=1)
        in_lower = (kcol, vcol) if half == 0 else (kswap, vswap)
        in_upper = (kswap, vswap) if half == 0 else (kcol, vcol)
        k_lo, v_lo = (jnp.where(lower, t, 0.0).astype(BF16) for t in in_lower)
        k_hi, v_hi = (jnp.where(lower, 0.0, t).astype(BF16) for t in in_upper)
        for pair in range(GROUP // heads_per_col):
            h0 = g * GROUP + pair * heads_per_col
            lanes = pl.ds(h0 * HEAD_DIM, LANES)
            q_pair = q_ref[:, lanes]
            acc = None
            for h, k, v in ((h0, k_lo, v_lo), (h0 + 1, k_hi, v_hi)):
                s = lax.dot_general(q_pair, k, (((1,), (1,)), ((), ())),
                                    preferred_element_type=F32)
                s = jnp.where(visible, s, MASKED)
                sink = sinks_ref[h] * LOG2_E
                m = jnp.maximum(jnp.max(s, axis=-1, keepdims=True), sink)
                p = jnp.exp2(s - m)
                den = jnp.sum(p, axis=-1, keepdims=True) + jnp.exp2(sink - m)
                o = _dot(p.astype(BF16), v) * (1.0 / den)
                acc = o if acc is None else acc + o
            o_ref[:, lanes] = acc.astype(BF16)


def _attn_prompt(sinks, q, kv, *, batch, seq, cast=()):
    m, qw = q.shape
    kvw = kv.shape[1]
    blk = WINDOW
    nb = seq // blk
    grid = (batch, nb)
    cast_in, cast_out, cast_shapes = _cast_specs(cast, grid)
    return pl.pallas_call(
        functools.partial(_attn_prompt_body, n_cast=len(cast)),
        grid_spec=pltpu.PrefetchScalarGridSpec(
            num_scalar_prefetch=1,
            grid=grid,
            in_specs=[
                pl.BlockSpec((blk, qw), lambda b, n, s: (b * nb + n, 0)),
                pl.BlockSpec((blk, kvw), lambda b, n, s: (b * nb + jnp.maximum(n - 1, 0), 0)),
                pl.BlockSpec((blk, kvw), lambda b, n, s: (b * nb + n, 0)),
            ] + cast_in,
            out_specs=[pl.BlockSpec((blk, qw), lambda b, n, s: (b * nb + n, 0))] + cast_out,
        ),
        out_shape=[jax.ShapeDtypeStruct((m, qw), BF16)] + cast_shapes,
        compiler_params=_params(2, 32),
        name="attn_prompt",
    )(sinks, q, kv, kv, *[w for w, _ in cast])


def _attn_sample_body(sinks_ref, q_ref, kv_ref, ck_ref, cv_ref, o_ref, knew_ref, vnew_ref, *,
                      seq_len):
    nseq, win, kw = ck_ref.shape
    rows = GROUP * seq_len
    q = q_ref[...].astype(F32)
    kv_new = kv_ref[...]
    for new_ref, cache_ref, fresh in ((knew_ref, ck_ref, kv_new[:, :kw]),
                                      (vnew_ref, cv_ref, kv_new[:, kw:])):
        new_ref[:, :win - seq_len, :] = cache_ref[:, seq_len:, :]
        new_ref[:, win - seq_len:, :] = fresh.reshape(nseq, seq_len, kw)
    pad = jnp.zeros((nseq, win - seq_len, HEAD_DIM), F32)
    step = lax.broadcasted_iota(jnp.int32, (1, rows, 2 * win), 1) % seq_len
    c = lax.broadcasted_iota(jnp.int32, (1, rows, 2 * win), 2)
    visible = ((c < win) & (c > step + (win - WINDOW))) | ((c >= win) & (c - win <= step))
    pieces = []
    for g in range(N_KV_HEADS):
        lo, hi = g * HEAD_DIM, (g + 1) * HEAD_DIM
        heads = range(g * GROUP, (g + 1) * GROUP)
        k_new = kv_new[:, lo:hi].reshape(nseq, seq_len, HEAD_DIM)
        v_new = kv_new[:, kw + lo:kw + hi].reshape(nseq, seq_len, HEAD_DIM)
        k = jnp.concatenate([ck_ref[:, :, lo:hi], k_new, pad], axis=1).astype(BF16)
        v = jnp.concatenate([cv_ref[:, :, lo:hi], v_new, pad], axis=1).astype(BF16)
        qg = jnp.concatenate(
            [q[:, h * HEAD_DIM:(h + 1) * HEAD_DIM].reshape(nseq, seq_len, HEAD_DIM)
             for h in heads], axis=1).astype(BF16)
        s = jnp.einsum("bqd,bkd->bqk", qg, k, preferred_element_type=F32)
        s = jnp.where(visible, s, MASKED)
        sink_col = jnp.concatenate(
            [jnp.full((1, seq_len, 1), sinks_ref[h] * LOG2_E, F32) for h in heads], axis=1)
        m = jnp.maximum(jnp.max(s, axis=-1, keepdims=True), sink_col)
        p = jnp.exp2(s - m)
        den = jnp.sum(p, axis=-1, keepdims=True) + jnp.exp2(sink_col - m)
        o = jnp.einsum("bqk,bkd->bqd", p.astype(BF16), v, preferred_element_type=F32) / den
        pieces += [o[:, t * seq_len:(t + 1) * seq_len].reshape(nseq * seq_len, HEAD_DIM)
                   for t in range(GROUP)]
    o_ref[...] = jnp.concatenate(pieces, axis=1).astype(BF16)


def _attn_sample(sinks, q, kv, cache_k, cache_v, *, seq_len):
    m, qw = q.shape
    body = functools.partial(_attn_sample_body, seq_len=seq_len)
    full = lambda shape: pl.BlockSpec(shape, lambda i, s: (0,) * len(shape))
    return pl.pallas_call(
        body,
        grid_spec=pltpu.PrefetchScalarGridSpec(
            num_scalar_prefetch=1,
            grid=(1,),
            in_specs=[full(q.shape), full(kv.shape), full(cache_k.shape), full(cache_v.shape)],
            out_specs=[full((m, qw)), full(cache_k.shape), full(cache_v.shape)],
        ),
        out_shape=[jax.ShapeDtypeStruct((m, qw), BF16),
                   jax.ShapeDtypeStruct(cache_k.shape, F32),
                   jax.ShapeDtypeStruct(cache_v.shape, F32)],
        compiler_params=_params(1, 48),
        name="attn_sample",
    )(sinks, q, kv, cache_k, cache_v)


def _rope_tables(pos):
    half = HEAD_DIM // 2
    inv = ROPE_THETA ** (-jnp.arange(half, dtype=F32) / half)
    ang = pos.astype(F32)[:, None] * inv[None, :]
    cos, sin = jnp.cos(ang), jnp.sin(ang)
    reps = LANES // HEAD_DIM
    cos = jnp.tile(jnp.concatenate([cos, cos], axis=1), (1, reps))
    sin_signed = jnp.tile(jnp.concatenate([-sin, sin], axis=1), (1, reps))
    return cos, sin_signed


def kernel(x_prompt, x_sample, state_conv, cache_k_win, cache_v_win, ln_mix, ln_mlp,
           w_conv_in, w_conv, w_conv_out, w_qkv, w_attn_out, q_norm, k_norm, sinks,
           w_up, w_down):
    bp, tp, d = x_prompt.shape
    bs, ts, _ = x_sample.shape
    win = cache_k_win.shape[2]
    kw = N_KV_HEADS * HEAD_DIM
    assert ts == SUBLANES and win == WINDOW and tp % WINDOW == 0

    tm_p = 512
    tm_conv = 1024
    m_s = bs * ts
    n_tiles = bp * tp // tm_p
    rows_s = m_s // n_tiles
    assert rows_s % (2 * SUBLANES) == 0 and rows_s % ts == 0

    xp = x_prompt.reshape(bp * tp, d)
    xs = x_sample.reshape(m_s, d)

    w_in = w_conv_in[0].astype(BF16)
    gain = ln_mix[0][None]
    gate_p, u_tail, wu0, wd0, w_out = _conv_in_prompt(
        xp, gain, w_in, w_conv[0], seq=tp, tm=tm_conv, tn=512,
        cast=[(w_up, 0), (w_down, 0), (w_conv_out, 0)])
    past = state_conv[0]
    zeros = lambda n: jnp.zeros((bs, n, d), F32)
    p1 = jnp.concatenate([past[:, 1:2], zeros(ts - 1)], axis=1).reshape(m_s, d)
    p2 = jnp.concatenate([past, zeros(ts - 2)], axis=1).reshape(m_s, d)
    gate_s, u_s = _conv_in_sample(xs, gain, w_in, w_conv[0], p1, p2, seq_len=ts, tn=512)
    x = _proj_residual((xp, xs), (gate_p, gate_s), w_out, n_tiles=n_tiles)
    tiles_per_seq = tp // tm_conv
    new_conv_prompt = u_tail[tiles_per_seq - 1::tiles_per_seq, SUBLANES - (CONV_WIDTH - 1):][None]
    new_conv_sample = u_s.reshape(bs, ts, d)[:, ts - (CONV_WIDTH - 1):][None]

    x, wqkv = _mlp(x, ln_mlp[0][None], wu0, wd0, n_tiles=n_tiles, tf=2048, cast=[(w_qkv, 0)])

    gain = ln_mix[1][None]
    tn = 2 * kw
    head_id = jnp.arange(tn) // HEAD_DIM
    ones_blockdiag = (head_id[:, None] == head_id[None, :]).astype(BF16)
    q_gain = jnp.tile(q_norm[0], tn // HEAD_DIM)[None]
    k_gain = jnp.tile(k_norm[0], kw // HEAD_DIM)[None]
    rope_p = _rope_tables(jnp.arange(tp, dtype=jnp.int32))
    rope_s = _rope_tables(PAST_LEN + jnp.arange(ts, dtype=jnp.int32))
    rope_s = tuple(jnp.tile(t, (rows_s // ts, 1)) for t in rope_s)
    q_p, q_s, kv_p, kv_s, kv_tail = _qkv(
        x, gain, wqkv, ones_blockdiag, q_gain, k_gain, rope_p, rope_s,
        n_tiles=n_tiles, prompt_rows=tm_p, seq_tiles=tp // tm_p)
    o_p, wu1, wd1, wo = _attn_prompt(sinks[0], q_p, kv_p, batch=bp, seq=tp,
                                     cast=[(w_up, 1), (w_down, 1), (w_attn_out, 0)])
    ck = cache_k_win[0].reshape(bs, win, kw)
    cv = cache_v_win[0].reshape(bs, win, kw)
    o_s, new_k_sample, new_v_sample = _attn_sample(sinks[0], q_s, kv_s, ck, cv, seq_len=ts)
    x = _proj_residual((x,), (o_p, o_s), wo, n_tiles=n_tiles)

    new_k_prompt = kv_tail[:, :, :kw].reshape(1, bp, WINDOW, N_KV_HEADS, HEAD_DIM)
    new_v_prompt = kv_tail[:, :, kw:].reshape(1, bp, WINDOW, N_KV_HEADS, HEAD_DIM)
    new_k_sample = new_k_sample.reshape(1, bs, win, N_KV_HEADS, HEAD_DIM)
    new_v_sample = new_v_sample.reshape(1, bs, win, N_KV_HEADS, HEAD_DIM)

    xp, xs = _mlp(x, ln_mlp[1][None], wu1, wd1, n_tiles=n_tiles, tf=2048,
                  out_rows=(tm_p, rows_s))

    return (xp.reshape(bp, tp, d), xs.reshape(bs, ts, d), new_conv_prompt, new_conv_sample,
            new_k_prompt, new_v_prompt, new_k_sample, new_v_sample)
```

```python
import functools
import math

import jax
import jax.numpy as jnp
from jax import lax
from jax.experimental import pallas as pl
from jax.experimental.pallas import tpu as pltpu

F32 = jnp.float32
BF16 = jnp.bfloat16

HEAD_DIM = 64
N_HEADS = 32
N_KV_HEADS = 4
GROUP = N_HEADS // N_KV_HEADS
WINDOW = 128
PAST_LEN = 16384
ROPE_THETA = 10000.0
EPS = 1e-6
CONV_WIDTH = 3

SUBLANES = 8
LANES = 128
MIB = 1024 * 1024
QKV_ROW_CHUNK = 256
CONV_ROW_CHUNK = 256
MLP_NORM_ROW_CHUNK = 256
MLP_FF_CHUNK = 1024


def _params(n_grid_dims, vmem_mib):
    return pltpu.CompilerParams(
        dimension_semantics=("arbitrary",) * n_grid_dims,
        vmem_limit_bytes=vmem_mib * MIB,
    )


def _rms_norm(x, gain):
    return x * lax.rsqrt(jnp.mean(x * x, axis=-1, keepdims=True) + EPS) * gain


def _dot(a, b):
    return jnp.dot(a, b, preferred_element_type=F32)


def _cast_specs(weights, grid):
    n_steps = 1
    for g in grid:
        n_steps *= g

    def step_of(*ids):
        step = ids[0]
        for g, idx in zip(grid[1:], ids[1:len(grid)]):
            step = step * g + idx
        return step

    in_specs, out_specs, out_shapes = [], [], []
    for w, layer in weights:
        rows, cols = w.shape[1:]
        assert rows % (n_steps * 2 * SUBLANES) == 0, (w.shape, n_steps)
        rb = rows // n_steps
        in_specs.append(pl.BlockSpec((None, rb, cols),
                                     lambda *ids, layer=layer: (layer, step_of(*ids), 0)))
        out_specs.append(pl.BlockSpec((rb, cols), lambda *ids: (step_of(*ids), 0)))
        out_shapes.append(jax.ShapeDtypeStruct((rows, cols), BF16))
    return in_specs, out_specs, out_shapes


def _cast_blocks(src_refs, dst_refs):
    for src, dst in zip(src_refs, dst_refs):
        dst[...] = src[...].astype(BF16)


class _RowParts:
    def __init__(self, refs):
        self.refs = tuple(refs)
        self.rows = sum(r.shape[0] for r in self.refs)

    def _pieces(self, start, size):
        offset = 0
        for ref in self.refs:
            lo, hi = max(start, offset), min(start + size, offset + ref.shape[0])
            if lo < hi:
                yield ref, lo - offset, lo - start, hi - lo
            offset += ref.shape[0]

    def load(self, start=0, size=None):
        size = self.rows - start if size is None else size
        pieces = [ref[at:at + n, :] for ref, at, _, n in self._pieces(start, size)]
        return pieces[0] if len(pieces) == 1 else jnp.concatenate(pieces, axis=0)

    def store(self, start, value, cols=slice(None), accumulate=False):
        for ref, at, src, n in self._pieces(start, value.shape[0]):
            if accumulate:
                ref[at:at + n, cols] += value[src:src + n]
            else:
                ref[at:at + n, cols] = value[src:src + n]


def _row_chunks(rows, chunk):
    n = max(rows // chunk, 1)
    return [(c * chunk, chunk if c < n - 1 else rows - c * chunk) for c in range(n)]


def _mlp_body(x_ref, g_ref, wu_ref, wd_ref, *rest, n_cast, n_out, ff_chunk):
    cast_src, out_refs, cast_dst, (h_ref,) = (
        rest[:n_cast], rest[n_cast:n_cast + n_out], rest[n_cast + n_out:2 * n_cast + n_out],
        rest[2 * n_cast + n_out:])
    out = _RowParts(out_refs)
    tm = x_ref.shape[0]
    tf = wu_ref.shape[1]

    def add_ffn(start, h, base):
        for t in range(tf // ff_chunk):
            cols = pl.ds(t * ff_chunk, ff_chunk)
            a = jnp.maximum(_dot(h, wu_ref[:, cols]), 0.0)
            part = _dot((a * a).astype(BF16), wd_ref[cols, :])
            if base is None:
                out.store(start, part, accumulate=True)
            else:
                out.store(start, base + part)
                base = None

    @pl.when(pl.program_id(1) == 0)
    def _():
        _cast_blocks(cast_src, cast_dst)
        for start, size in _row_chunks(tm, MLP_NORM_ROW_CHUNK):
            rows = pl.ds(start, size)
            x = x_ref[rows, :]
            h = _rms_norm(x, g_ref[...]).astype(BF16)
            h_ref[rows, :] = h
            add_ffn(start, h, x)

    @pl.when(pl.program_id(1) != 0)
    def _():
        _cast_blocks(cast_src, cast_dst)
        add_ffn(0, h_ref[...], None)


def _mlp(x, gain, w_up, w_down, *, n_tiles, tf, out_rows=None, cast=()):
    m, d = x.shape
    tm = m // n_tiles
    out_rows = (tm,) if out_rows is None else out_rows
    assert sum(out_rows) == tm
    ff = w_up.shape[1]
    grid = (n_tiles, ff // tf)
    cast_in, cast_out, cast_shapes = _cast_specs(cast, grid)
    body = functools.partial(_mlp_body, n_cast=len(cast), n_out=len(out_rows),
                             ff_chunk=min(tf, MLP_FF_CHUNK))
    return pl.pallas_call(
        body,
        grid=grid,
        in_specs=[
            pl.BlockSpec((tm, d), lambda i, f: (i, 0)),
            pl.BlockSpec((1, d), lambda i, f: (0, 0)),
            pl.BlockSpec((d, tf), lambda i, f: (0, f)),
            pl.BlockSpec((tf, d), lambda i, f: (f, 0)),
        ] + cast_in,
        out_specs=[pl.BlockSpec((r, d), lambda i, f: (i, 0)) for r in out_rows] + cast_out,
        out_shape=[jax.ShapeDtypeStruct((r * n_tiles, d), F32) for r in out_rows] + cast_shapes,
        scratch_shapes=[pltpu.VMEM((tm, d), BF16)],
        compiler_params=_params(2, 58),
        name="mlp",
    )(x, gain, w_up, w_down, *[w for w, _ in cast])


def _proj_body(*refs, n_x, n_a):
    x, a = _RowParts(refs[:n_x]), _RowParts(refs[n_x:n_x + n_a])
    w_ref, o_ref = refs[n_x + n_a:]
    o_ref[...] = x.load() + _dot(a.load(), w_ref[...])


def _proj_residual(x_parts, a_parts, w, *, n_tiles):
    d = w.shape[1]
    tm = sum(p.shape[0] for p in x_parts) // n_tiles
    assert tm == sum(p.shape[0] for p in a_parts) // n_tiles
    tile_spec = lambda p: pl.BlockSpec((p.shape[0] // n_tiles, p.shape[1]), lambda i: (i, 0))
    return pl.pallas_call(
        functools.partial(_proj_body, n_x=len(x_parts), n_a=len(a_parts)),
        grid=(n_tiles,),
        in_specs=[tile_spec(p) for p in (*x_parts, *a_parts)] + [
            pl.BlockSpec(w.shape, lambda i: (0, 0), pipeline_mode=pl.Buffered(1))],
        out_specs=pl.BlockSpec((tm, d), lambda i: (i, 0)),
        out_shape=jax.ShapeDtypeStruct((tm * n_tiles, d), F32),
        compiler_params=_params(1, 48),
        name="proj_residual",
    )(*x_parts, *a_parts, w)


def _conv_taps(u, u1, u2, w):
    return w[0:1, :] * u2 + w[1:2, :] * u1 + w[2:3, :] * u


def _conv_in_prompt_body(x_ref, g_ref, wb_ref, wc_ref, wv_ref, wconv_ref, *rest,
                         tiles_per_seq, n_cast):
    cast_src, (gate_ref, utail_ref), cast_dst, (h_ref, carry_ref) = (
        rest[:n_cast], rest[n_cast:n_cast + 2], rest[n_cast + 2:2 * n_cast + 2],
        rest[2 * n_cast + 2:])
    i = pl.program_id(0)
    j = pl.program_id(1)

    @pl.when(i % tiles_per_seq == 0)
    def _():
        carry_ref[j] = jnp.zeros(carry_ref.shape[1:], F32)

    tm = h_ref.shape[0]
    chunk = min(tm, CONV_ROW_CHUNK)
    row8 = lax.broadcasted_iota(jnp.int32, (SUBLANES, 1), 0)
    w_conv = wconv_ref[...]

    def run(first_column_tile):
        _cast_blocks(cast_src, cast_dst)
        prev = carry_ref[j]
        for c in range(tm // chunk):
            rows = pl.ds(c * chunk, chunk)
            if first_column_tile:
                h = _rms_norm(x_ref[rows, :], g_ref[...]).astype(BF16)
                h_ref[rows, :] = h
            else:
                h = h_ref[rows, :]
            u = _dot(h, wc_ref[...]) * _dot(h, wv_ref[...])
            u1 = pltpu.roll(u, 1, axis=0)
            u2 = pltpu.roll(u, 2, axis=0)
            top1 = jnp.where(row8 < 1, pltpu.roll(prev, 1, axis=0), u1[0:SUBLANES])
            top2 = jnp.where(row8 < 2, pltpu.roll(prev, 2, axis=0), u2[0:SUBLANES])
            u1 = jnp.concatenate([top1, u1[SUBLANES:]], axis=0)
            u2 = jnp.concatenate([top2, u2[SUBLANES:]], axis=0)
            conv = _conv_taps(u, u1, u2, w_conv)
            gate_ref[rows, :] = (_dot(h, wb_ref[...]) * conv).astype(BF16)
            prev = u[chunk - SUBLANES:]
        carry_ref[j] = prev
        utail_ref[0] = prev

    pl.when(j == 0)(functools.partial(run, True))
    pl.when(j != 0)(functools.partial(run, False))


def _conv_in_sample_body(x_ref, g_ref, wb_ref, wc_ref, wv_ref, wconv_ref, p1_ref, p2_ref,
                         gate_ref, u_ref, h_ref, *, seq_len):
    j = pl.program_id(1)

    @pl.when(j == 0)
    def _():
        h_ref[...] = _rms_norm(x_ref[...], g_ref[...]).astype(BF16)

    h = h_ref[...]
    u = _dot(h, wc_ref[...]) * _dot(h, wv_ref[...])
    step = lax.broadcasted_iota(jnp.int32, (u.shape[0], 1), 0) % seq_len
    u1 = jnp.where(step < 1, p1_ref[...], pltpu.roll(u, 1, axis=0))
    u2 = jnp.where(step < 2, p2_ref[...], pltpu.roll(u, 2, axis=0))
    conv = _conv_taps(u, u1, u2, wconv_ref[...])
    u_ref[...] = u
    gate_ref[...] = (_dot(h, wb_ref[...]) * conv).astype(BF16)


def _conv_in_specs(tm, d, tn):
    third = d // tn
    return [
        pl.BlockSpec((tm, d), lambda i, j: (i, 0)),
        pl.BlockSpec((1, d), lambda i, j: (0, 0)),
        pl.BlockSpec((d, tn), lambda i, j: (0, j)),
        pl.BlockSpec((d, tn), lambda i, j: (0, third + j)),
        pl.BlockSpec((d, tn), lambda i, j: (0, 2 * third + j)),
        pl.BlockSpec((CONV_WIDTH, tn), lambda i, j: (0, j)),
    ]


def _conv_in_prompt(x, gain, w_in, w_conv, *, seq, tm, tn, cast=()):
    m, d = x.shape
    grid = (m // tm, d // tn)
    cast_in, cast_out, cast_shapes = _cast_specs(cast, grid)
    body = functools.partial(_conv_in_prompt_body, tiles_per_seq=seq // tm, n_cast=len(cast))
    return pl.pallas_call(
        body,
        grid=grid,
        in_specs=_conv_in_specs(tm, d, tn) + cast_in,
        out_specs=[
            pl.BlockSpec((tm, tn), lambda i, j: (i, j)),
            pl.BlockSpec((1, SUBLANES, tn), lambda i, j: (i, 0, j)),
        ] + cast_out,
        out_shape=[
            jax.ShapeDtypeStruct((m, d), BF16),
            jax.ShapeDtypeStruct((m // tm, SUBLANES, d), F32),
        ] + cast_shapes,
        scratch_shapes=[
            pltpu.VMEM((tm, d), BF16),
            pltpu.VMEM((d // tn, SUBLANES, tn), F32),
        ],
        compiler_params=_params(2, 56),
        name="conv_in_prompt",
    )(x, gain, w_in, w_in, w_in, w_conv, *[w for w, _ in cast])


def _conv_in_sample(x, gain, w_in, w_conv, p1, p2, *, seq_len, tn):
    m, d = x.shape
    body = functools.partial(_conv_in_sample_body, seq_len=seq_len)
    return pl.pallas_call(
        body,
        grid=(1, d // tn),
        in_specs=_conv_in_specs(m, d, tn) + [
            pl.BlockSpec((m, tn), lambda i, j: (0, j)),
            pl.BlockSpec((m, tn), lambda i, j: (0, j)),
        ],
        out_specs=[
            pl.BlockSpec((m, tn), lambda i, j: (0, j)),
            pl.BlockSpec((m, tn), lambda i, j: (0, j)),
        ],
        out_shape=[
            jax.ShapeDtypeStruct((m, d), BF16),
            jax.ShapeDtypeStruct((m, d), F32),
        ],
        scratch_shapes=[pltpu.VMEM((m, d), BF16)],
        compiler_params=_params(2, 48),
        name="conv_in_sample",
    )(x, gain, w_in, w_in, w_in, w_conv, p1, p2)


def _head_norm_rope(z, gain, ones_blockdiag, cos, sin_signed, *, split_ssq):
    w = z.shape[1]
    zz = z * z
    hi = zz.astype(BF16)
    ssq = _dot(hi, ones_blockdiag)
    if split_ssq:
        ssq += _dot((zz - hi.astype(F32)).astype(BF16), ones_blockdiag)
    zn = z * lax.rsqrt(ssq * (1.0 / HEAD_DIM) + EPS) * gain
    reps = w // cos.shape[1]
    cos = jnp.concatenate([cos] * reps, axis=1)
    sin_signed = jnp.concatenate([sin_signed] * reps, axis=1)
    lane = lax.broadcasted_iota(jnp.int32, (1, w), 1)
    first_half = (lane % HEAD_DIM) < (HEAD_DIM // 2)
    partner = jnp.where(first_half,
                        pltpu.roll(zn, w - HEAD_DIM // 2, axis=1),
                        pltpu.roll(zn, HEAD_DIM // 2, axis=1))
    return zn * cos + partner * sin_signed


def _qkv_body(x_ref, g_ref, w_ref, ones_ref, qgain_ref, kgain_ref,
              cos_p_ref, sin_p_ref, cos_s_ref, sin_s_ref,
              qp_ref, qs_ref, kvp_ref, kvs_ref, kv_tail_ref, *, seq_tiles):
    q_out, kv_out = _RowParts((qp_ref, qs_ref)), _RowParts((kvp_ref, kvs_ref))
    cos_tile, sin_tile = _RowParts((cos_p_ref, cos_s_ref)), _RowParts((sin_p_ref, sin_s_ref))
    tm = x_ref.shape[0]
    qw = qp_ref.shape[1]
    tn = ones_ref.shape[0]
    kw = N_KV_HEADS * HEAD_DIM
    for start, size in _row_chunks(tm, QKV_ROW_CHUNK):
        h = _rms_norm(x_ref[pl.ds(start, size), :], g_ref[...]).astype(BF16)
        cos, sin_signed = cos_tile.load(start, size), sin_tile.load(start, size)
        for t in range(qw // tn):
            cols = pl.ds(t * tn, tn)
            z = _dot(h, w_ref[:, cols])
            q = _head_norm_rope(z, qgain_ref[...], ones_ref[...], cos, sin_signed,
                                split_ssq=False)
            q_out.store(start, (q * Q_SCALE).astype(BF16), cols=cols)
        z = _dot(h, w_ref[:, pl.ds(qw, 2 * kw)])
        k = _head_norm_rope(z[:, :kw], kgain_ref[...], ones_ref[0:kw, 0:kw], cos, sin_signed,
                            split_ssq=True)
        kv_out.store(start, jnp.concatenate([k, z[:, kw:]], axis=1))

    @pl.when(pl.program_id(0) % seq_tiles == seq_tiles - 1)
    def _():
        kv_tail_ref[...] = kvp_ref[kvp_ref.shape[0] - WINDOW:, :]


def _qkv(x, gain, w_qkv, ones_blockdiag, q_gain, k_gain, rope_prompt, rope_sample, *,
         n_tiles, prompt_rows, seq_tiles):
    m, d = x.shape
    tm = m // n_tiles
    sample_rows = tm - prompt_rows
    qw = N_HEADS * HEAD_DIM
    kvw = 2 * N_KV_HEADS * HEAD_DIM
    tn = ones_blockdiag.shape[0]
    const = lambda shape: pl.BlockSpec(shape, lambda i: (0, 0), pipeline_mode=pl.Buffered(1))
    rope_p = pl.BlockSpec((prompt_rows, LANES), lambda i: (i % seq_tiles, 0))
    split = lambda width: [pl.BlockSpec((prompt_rows, width), lambda i: (i, 0)),
                           pl.BlockSpec((sample_rows, width), lambda i: (i, 0))]
    split_shape = lambda width, dtype: [
        jax.ShapeDtypeStruct((prompt_rows * n_tiles, width), dtype),
        jax.ShapeDtypeStruct((sample_rows * n_tiles, width), dtype)]
    return pl.pallas_call(
        functools.partial(_qkv_body, seq_tiles=seq_tiles),
        grid=(n_tiles,),
        in_specs=[
            pl.BlockSpec((tm, d), lambda i: (i, 0)),
            const((1, d)),
            const(w_qkv.shape),
            const((tn, tn)),
            const((1, tn)),
            const((1, kvw // 2)),
            rope_p, rope_p,
            const((sample_rows, LANES)), const((sample_rows, LANES)),
        ],
        out_specs=split(qw) + split(kvw) + [
            pl.BlockSpec((None, WINDOW, kvw), lambda i: (i // seq_tiles, 0, 0))],
        out_shape=split_shape(qw, BF16) + split_shape(kvw, F32) + [
            jax.ShapeDtypeStruct((n_tiles // seq_tiles, WINDOW, kvw), F32)],
        compiler_params=_params(1, 48),
        name="qkv",
    )(x, gain, w_qkv, ones_blockdiag, q_gain, k_gain, *rope_prompt, *rope_sample)


MASKED = -1e30
LOG2_E = math.log2(math.e)
Q_SCALE = HEAD_DIM ** -0.5 * LOG2_E


def _attn_prompt_body(sinks_ref, q_ref, kv_prev_ref, kv_cur_ref, *rest, n_cast):
    cast_src, (o_ref,), cast_dst = rest[:n_cast], rest[n_cast:n_cast + 1], rest[n_cast + 1:]
    _cast_blocks(cast_src, cast_dst)
    n = pl.program_id(1)
    blk = WINDOW
    n_blocks = q_ref.shape[0] // blk
    kw = N_KV_HEADS * HEAD_DIM
    r = lax.broadcasted_iota(jnp.int32, (blk, 2 * blk), 0)
    c = lax.broadcasted_iota(jnp.int32, (blk, 2 * blk), 1)
    band = (c > r + (blk - WINDOW)) & (c <= r + blk)
    band_at_sequence_start = band & ((c >= blk) | (n > 0))
    lower = lax.broadcasted_iota(jnp.int32, (1, LANES), 1) < HEAD_DIM
    heads_per_col = LANES // HEAD_DIM

    def padded_operands(kv):
        per_head = []
        for g in range(N_KV_HEADS):
            col, half = divmod(g, heads_per_col)
            kcol = kv[:, col * LANES:(col + 1) * LANES]
            vcol = kv[:, kw + col * LANES:kw + (col + 1) * LANES]
            kswap = pltpu.roll(kcol, HEAD_DIM, axis=1)
            vswap = pltpu.roll(vcol, HEAD_DIM, axis=1)
            in_lower = (kcol, vcol) if half == 0 else (kswap, vswap)
            in_upper = (kswap, vswap) if half == 0 else (kcol, vcol)
            per_head.append(tuple(jnp.where(lower, t, 0.0).astype(BF16) for t in in_lower)
                            + tuple(jnp.where(lower, 0.0, t).astype(BF16) for t in in_upper))
        return per_head

    key_blocks = [padded_operands(kv_prev_ref[...])]
    for b in range(n_blocks):
        rows = pl.ds(b * blk, blk)
        key_blocks.append(padded_operands(kv_cur_ref[rows, :]))
        visible = band_at_sequence_start if b == 0 else band
        for g in range(N_KV_HEADS):
            k_lo, v_lo, k_hi, v_hi = (jnp.concatenate(prev_and_own, axis=0) for prev_and_own
                                      in zip(key_blocks[b][g], key_blocks[b + 1][g]))
            for pair in range(GROUP // heads_per_col):
                h0 = g * GROUP + pair * heads_per_col
                lanes = pl.ds(h0 * HEAD_DIM, LANES)
                q_pair = q_ref[rows, lanes]
                acc = None
                for h, k, v in ((h0, k_lo, v_lo), (h0 + 1, k_hi, v_hi)):
                    s = lax.dot_general(q_pair, k, (((1,), (1,)), ((), ())),
                                        preferred_element_type=F32)
                    s = jnp.where(visible, s, MASKED)
                    sink = sinks_ref[h] * LOG2_E
                    m = jnp.maximum(jnp.max(s, axis=-1, keepdims=True), sink)
                    p = jnp.exp2(s - m)
                    den = jnp.sum(p, axis=-1, keepdims=True) + jnp.exp2(sink - m)
                    o = _dot(p.astype(BF16), v) * (1.0 / den)
                    acc = o if acc is None else acc + o
                o_ref[rows, lanes] = acc.astype(BF16)


def _attn_prompt(sinks, q, kv, *, batch, seq, tq, cast=()):
    m, qw = q.shape
    kvw = kv.shape[1]
    blk = WINDOW
    nt = seq // tq
    per_tile = tq // blk
    grid = (batch, nt)
    cast_in, cast_out, cast_shapes = _cast_specs(cast, grid)
    return pl.pallas_call(
        functools.partial(_attn_prompt_body, n_cast=len(cast)),
        grid_spec=pltpu.PrefetchScalarGridSpec(
            num_scalar_prefetch=1,
            grid=grid,
            in_specs=[
                pl.BlockSpec((tq, qw), lambda b, n, s: (b * nt + n, 0)),
                pl.BlockSpec((blk, kvw), lambda b, n, s: (
                    (b * nt + n) * per_tile - jnp.minimum(n, 1), 0)),
                pl.BlockSpec((tq, kvw), lambda b, n, s: (b * nt + n, 0)),
            ] + cast_in,
            out_specs=[pl.BlockSpec((tq, qw), lambda b, n, s: (b * nt + n, 0))] + cast_out,
        ),
        out_shape=[jax.ShapeDtypeStruct((m, qw), BF16)] + cast_shapes,
        compiler_params=_params(2, 56),
        name="attn_prompt",
    )(sinks, q, kv, kv, *[w for w, _ in cast])


def _attn_sample_body(sinks_ref, q_ref, kv_ref, ck_ref, cv_ref, o_ref, knew_ref, vnew_ref, *,
                      seq_len):
    nseq, win, kw = ck_ref.shape
    rows = GROUP * seq_len
    q = q_ref[...].astype(F32)
    kv_new = kv_ref[...]
    for new_ref, cache_ref, fresh in ((knew_ref, ck_ref, kv_new[:, :kw]),
                                      (vnew_ref, cv_ref, kv_new[:, kw:])):
        new_ref[:, :win - seq_len, :] = cache_ref[:, seq_len:, :]
        new_ref[:, win - seq_len:, :] = fresh.reshape(nseq, seq_len, kw)
    pad = jnp.zeros((nseq, win - seq_len, HEAD_DIM), F32)
    step = lax.broadcasted_iota(jnp.int32, (1, rows, 2 * win), 1) % seq_len
    c = lax.broadcasted_iota(jnp.int32, (1, rows, 2 * win), 2)
    visible = ((c < win) & (c > step + (win - WINDOW))) | ((c >= win) & (c - win <= step))
    pieces = []
    for g in range(N_KV_HEADS):
        lo, hi = g * HEAD_DIM, (g + 1) * HEAD_DIM
        heads = range(g * GROUP, (g + 1) * GROUP)
        k_new = kv_new[:, lo:hi].reshape(nseq, seq_len, HEAD_DIM)
        v_new = kv_new[:, kw + lo:kw + hi].reshape(nseq, seq_len, HEAD_DIM)
        k = jnp.concatenate([ck_ref[:, :, lo:hi], k_new, pad], axis=1).astype(BF16)
        v = jnp.concatenate([cv_ref[:, :, lo:hi], v_new, pad], axis=1).astype(BF16)
        qg = jnp.concatenate(
            [q[:, h * HEAD_DIM:(h + 1) * HEAD_DIM].reshape(nseq, seq_len, HEAD_DIM)
             for h in heads], axis=1).astype(BF16)
        s = jnp.einsum("bqd,bkd->bqk", qg, k, preferred_element_type=F32)
        s = jnp.where(visible, s, MASKED)
        sink_col = jnp.concatenate(
            [jnp.full((1, seq_len, 1), sinks_ref[h] * LOG2_E, F32) for h in heads], axis=1)
        m = jnp.maximum(jnp.max(s, axis=-1, keepdims=True), sink_col)
        p = jnp.exp2(s - m)
        den = jnp.sum(p, axis=-1, keepdims=True) + jnp.exp2(sink_col - m)
        o = jnp.einsum("bqk,bkd->bqd", p.astype(BF16), v, preferred_element_type=F32) / den
        pieces += [o[:, t * seq_len:(t + 1) * seq_len].reshape(nseq * seq_len, HEAD_DIM)
                   for t in range(GROUP)]
    o_ref[...] = jnp.concatenate(pieces, axis=1).astype(BF16)


def _attn_sample(sinks, q, kv, cache_k, cache_v, *, seq_len):
    m, qw = q.shape
    body = functools.partial(_attn_sample_body, seq_len=seq_len)
    full = lambda shape: pl.BlockSpec(shape, lambda i, s: (0,) * len(shape))
    return pl.pallas_call(
        body,
        grid_spec=pltpu.PrefetchScalarGridSpec(
            num_scalar_prefetch=1,
            grid=(1,),
            in_specs=[full(q.shape), full(kv.shape), full(cache_k.shape), full(cache_v.shape)],
            out_specs=[full((m, qw)), full(cache_k.shape), full(cache_v.shape)],
        ),
        out_shape=[jax.ShapeDtypeStruct((m, qw), BF16),
                   jax.ShapeDtypeStruct(cache_k.shape, F32),
                   jax.ShapeDtypeStruct(cache_v.shape, F32)],
        compiler_params=_params(1, 48),
        name="attn_sample",
    )(sinks, q, kv, cache_k, cache_v)


def _rope_tables(pos):
    half = HEAD_DIM // 2
    inv = ROPE_THETA ** (-jnp.arange(half, dtype=F32) / half)
    ang = pos.astype(F32)[:, None] * inv[None, :]
    cos, sin = jnp.cos(ang), jnp.sin(ang)
    reps = LANES // HEAD_DIM
    cos = jnp.tile(jnp.concatenate([cos, cos], axis=1), (1, reps))
    sin_signed = jnp.tile(jnp.concatenate([-sin, sin], axis=1), (1, reps))
    return cos, sin_signed


def kernel(x_prompt, x_sample, state_conv, cache_k_win, cache_v_win, ln_mix, ln_mlp,
           w_conv_in, w_conv, w_conv_out, w_qkv, w_attn_out, q_norm, k_norm, sinks,
           w_up, w_down):
    bp, tp, d = x_prompt.shape
    bs, ts, _ = x_sample.shape
    win = cache_k_win.shape[2]
    kw = N_KV_HEADS * HEAD_DIM
    assert ts == SUBLANES and win == WINDOW and tp % WINDOW == 0

    tm_p = 512
    tm_conv = 1024
    m_s = bs * ts
    n_tiles = bp * tp // tm_p
    rows_s = m_s // n_tiles
    assert rows_s % (2 * SUBLANES) == 0 and rows_s % ts == 0

    xp = x_prompt.reshape(bp * tp, d)
    xs = x_sample.reshape(m_s, d)

    w_in = w_conv_in[0].astype(BF16)
    gain = ln_mix[0][None]
    gate_p, u_tail, wu0, wd0, w_out = _conv_in_prompt(
        xp, gain, w_in, w_conv[0], seq=tp, tm=tm_conv, tn=512,
        cast=[(w_up, 0), (w_down, 0), (w_conv_out, 0)])
    past = state_conv[0]
    zeros = lambda n: jnp.zeros((bs, n, d), F32)
    p1 = jnp.concatenate([past[:, 1:2], zeros(ts - 1)], axis=1).reshape(m_s, d)
    p2 = jnp.concatenate([past, zeros(ts - 2)], axis=1).reshape(m_s, d)
    gate_s, u_s = _conv_in_sample(xs, gain, w_in, w_conv[0], p1, p2, seq_len=ts, tn=512)
    x = _proj_residual((xp, xs), (gate_p, gate_s), w_out, n_tiles=n_tiles)
    tiles_per_seq = tp // tm_conv
    new_conv_prompt = u_tail[tiles_per_seq - 1::tiles_per_seq, SUBLANES - (CONV_WIDTH - 1):][None]
    new_conv_sample = u_s.reshape(bs, ts, d)[:, ts - (CONV_WIDTH - 1):][None]

    x, wqkv = _mlp(x, ln_mlp[0][None], wu0, wd0, n_tiles=n_tiles, tf=2048, cast=[(w_qkv, 0)])

    gain = ln_mix[1][None]
    tn = 2 * kw
    head_id = jnp.arange(tn) // HEAD_DIM
    ones_blockdiag = (head_id[:, None] == head_id[None, :]).astype(BF16)
    q_gain = jnp.tile(q_norm[0], tn // HEAD_DIM)[None]
    k_gain = jnp.tile(k_norm[0], kw // HEAD_DIM)[None]
    rope_p = _rope_tables(jnp.arange(tp, dtype=jnp.int32))
    rope_s = _rope_tables(PAST_LEN + jnp.arange(ts, dtype=jnp.int32))
    rope_s = tuple(jnp.tile(t, (rows_s // ts, 1)) for t in rope_s)
    q_p, q_s, kv_p, kv_s, kv_tail = _qkv(
        x, gain, wqkv, ones_blockdiag, q_gain, k_gain, rope_p, rope_s,
        n_tiles=n_tiles, prompt_rows=tm_p, seq_tiles=tp // tm_p)
    o_p, wu1, wd1, wo = _attn_prompt(sinks[0], q_p, kv_p, batch=bp, seq=tp, tq=512,
                                     cast=[(w_up, 1), (w_down, 1), (w_attn_out, 0)])
    ck = cache_k_win[0].reshape(bs, win, kw)
    cv = cache_v_win[0].reshape(bs, win, kw)
    o_s, new_k_sample, new_v_sample = _attn_sample(sinks[0], q_s, kv_s, ck, cv, seq_len=ts)
    x = _proj_residual((x,), (o_p, o_s), wo, n_tiles=n_tiles)

    new_k_prompt = kv_tail[:, :, :kw].reshape(1, bp, WINDOW, N_KV_HEADS, HEAD_DIM)
    new_v_prompt = kv_tail[:, :, kw:].reshape(1, bp, WINDOW, N_KV_HEADS, HEAD_DIM)
    new_k_sample = new_k_sample.reshape(1, bs, win, N_KV_HEADS, HEAD_DIM)
    new_v_sample = new_v_sample.reshape(1, bs, win, N_KV_HEADS, HEAD_DIM)

    xp, xs = _mlp(x, ln_mlp[1][None], wu1, wd1, n_tiles=n_tiles, tf=2048,
                  out_rows=(tm_p, rows_s))

    return (xp.reshape(bp, tp, d), xs.reshape(bs, ts, d), new_conv_prompt, new_conv_sample,
            new_k_prompt, new_v_prompt, new_k_sample, new_v_sample)
```

```python
import functools
import math

import jax
import jax.numpy as jnp
from jax import lax
from jax.experimental import pallas as pl
from jax.experimental.pallas import tpu as pltpu

F32 = jnp.float32
BF16 = jnp.bfloat16

HEAD_DIM = 64
N_HEADS = 32
N_KV_HEADS = 4
GROUP = N_HEADS // N_KV_HEADS
WINDOW = 128
PAST_LEN = 16384
ROPE_THETA = 10000.0
EPS = 1e-6
CONV_WIDTH = 3

SUBLANES = 8
LANES = 128
MIB = 1024 * 1024
QKV_ROW_CHUNK = 256
CONV_ROW_CHUNK = 256
MLP_NORM_ROW_CHUNK = 256
MLP_FF_CHUNK = 1024


def _params(n_grid_dims, vmem_mib):
    return pltpu.CompilerParams(
        dimension_semantics=("arbitrary",) * n_grid_dims,
        vmem_limit_bytes=vmem_mib * MIB,
    )


def _rms_norm(x, gain):
    return x * lax.rsqrt(jnp.mean(x * x, axis=-1, keepdims=True) + EPS) * gain


def _dot(a, b):
    return jnp.dot(a, b, preferred_element_type=F32)


def _cast_specs(weights, grid):
    n_steps = 1
    for g in grid:
        n_steps *= g

    def step_of(*ids):
        step = ids[0]
        for g, idx in zip(grid[1:], ids[1:len(grid)]):
            step = step * g + idx
        return step

    in_specs, out_specs, out_shapes = [], [], []
    for w, layer in weights:
        rows, cols = w.shape[1:]
        assert rows % (n_steps * 2 * SUBLANES) == 0, (w.shape, n_steps)
        rb = rows // n_steps
        in_specs.append(pl.BlockSpec((None, rb, cols),
                                     lambda *ids, layer=layer: (layer, step_of(*ids), 0)))
        out_specs.append(pl.BlockSpec((rb, cols), lambda *ids: (step_of(*ids), 0)))
        out_shapes.append(jax.ShapeDtypeStruct((rows, cols), BF16))
    return in_specs, out_specs, out_shapes


def _cast_blocks(src_refs, dst_refs):
    for src, dst in zip(src_refs, dst_refs):
        dst[...] = src[...].astype(BF16)


class _RowParts:
    def __init__(self, refs):
        self.refs = tuple(refs)
        self.rows = sum(r.shape[0] for r in self.refs)

    def _pieces(self, start, size):
        offset = 0
        for ref in self.refs:
            lo, hi = max(start, offset), min(start + size, offset + ref.shape[0])
            if lo < hi:
                yield ref, lo - offset, lo - start, hi - lo
            offset += ref.shape[0]

    def load(self, start=0, size=None):
        size = self.rows - start if size is None else size
        pieces = [ref[at:at + n, :] for ref, at, _, n in self._pieces(start, size)]
        return pieces[0] if len(pieces) == 1 else jnp.concatenate(pieces, axis=0)

    def store(self, start, value, cols=slice(None), accumulate=False):
        for ref, at, src, n in self._pieces(start, value.shape[0]):
            if accumulate:
                ref[at:at + n, cols] += value[src:src + n]
            else:
                ref[at:at + n, cols] = value[src:src + n]


def _row_chunks(rows, chunk):
    n = max(rows // chunk, 1)
    return [(c * chunk, chunk if c < n - 1 else rows - c * chunk) for c in range(n)]


def _mlp_body(x_ref, g_ref, wu_ref, wd_ref, *rest, n_cast, n_out, ff_chunk):
    cast_src, out_refs, cast_dst, (h_ref,) = (
        rest[:n_cast], rest[n_cast:n_cast + n_out], rest[n_cast + n_out:2 * n_cast + n_out],
        rest[2 * n_cast + n_out:])
    out = _RowParts(out_refs)
    tm = x_ref.shape[0]
    tf = wu_ref.shape[1]

    def add_ffn(start, h, base):
        for t in range(tf // ff_chunk):
            cols = pl.ds(t * ff_chunk, ff_chunk)
            a = jnp.maximum(_dot(h, wu_ref[:, cols]), 0.0)
            part = _dot((a * a).astype(BF16), wd_ref[cols, :])
            if base is None:
                out.store(start, part, accumulate=True)
            else:
                out.store(start, base + part)
                base = None

    @pl.when(pl.program_id(1) == 0)
    def _():
        _cast_blocks(cast_src, cast_dst)
        for start, size in _row_chunks(tm, MLP_NORM_ROW_CHUNK):
            rows = pl.ds(start, size)
            x = x_ref[rows, :]
            h = _rms_norm(x, g_ref[...]).astype(BF16)
            h_ref[rows, :] = h
            add_ffn(start, h, x)

    @pl.when(pl.program_id(1) != 0)
    def _():
        _cast_blocks(cast_src, cast_dst)
        add_ffn(0, h_ref[...], None)


def _mlp(x, gain, w_up, w_down, *, n_tiles, tf, out_rows=None, cast=()):
    m, d = x.shape
    tm = m // n_tiles
    out_rows = (tm,) if out_rows is None else out_rows
    assert sum(out_rows) == tm
    ff = w_up.shape[1]
    grid = (n_tiles, ff // tf)
    cast_in, cast_out, cast_shapes = _cast_specs(cast, grid)
    body = functools.partial(_mlp_body, n_cast=len(cast), n_out=len(out_rows),
                             ff_chunk=min(tf, MLP_FF_CHUNK))
    return pl.pallas_call(
        body,
        grid=grid,
        in_specs=[
            pl.BlockSpec((tm, d), lambda i, f: (i, 0)),
            pl.BlockSpec((1, d), lambda i, f: (0, 0)),
            pl.BlockSpec((d, tf), lambda i, f: (0, f)),
            pl.BlockSpec((tf, d), lambda i, f: (f, 0)),
        ] + cast_in,
        out_specs=[pl.BlockSpec((r, d), lambda i, f: (i, 0)) for r in out_rows] + cast_out,
        out_shape=[jax.ShapeDtypeStruct((r * n_tiles, d), F32) for r in out_rows] + cast_shapes,
        scratch_shapes=[pltpu.VMEM((tm, d), BF16)],
        compiler_params=_params(2, 58),
        name="mlp",
    )(x, gain, w_up, w_down, *[w for w, _ in cast])


def _proj_body(*refs, n_x, n_a):
    x, a = _RowParts(refs[:n_x]), _RowParts(refs[n_x:n_x + n_a])
    w_ref, o_ref = refs[n_x + n_a:]
    o_ref[...] = x.load() + _dot(a.load(), w_ref[...])


def _proj_residual(x_parts, a_parts, w, *, n_tiles):
    d = w.shape[1]
    tm = sum(p.shape[0] for p in x_parts) // n_tiles
    assert tm == sum(p.shape[0] for p in a_parts) // n_tiles
    tile_spec = lambda p: pl.BlockSpec((p.shape[0] // n_tiles, p.shape[1]), lambda i: (i, 0))
    return pl.pallas_call(
        functools.partial(_proj_body, n_x=len(x_parts), n_a=len(a_parts)),
        grid=(n_tiles,),
        in_specs=[tile_spec(p) for p in (*x_parts, *a_parts)] + [
            pl.BlockSpec(w.shape, lambda i: (0, 0), pipeline_mode=pl.Buffered(1))],
        out_specs=pl.BlockSpec((tm, d), lambda i: (i, 0)),
        out_shape=jax.ShapeDtypeStruct((tm * n_tiles, d), F32),
        compiler_params=_params(1, 48),
        name="proj_residual",
    )(*x_parts, *a_parts, w)


def _conv_taps(u, u1, u2, w):
    return w[0:1, :] * u2 + w[1:2, :] * u1 + w[2:3, :] * u


def _conv_in_prompt_body(x_ref, g_ref, wb_ref, wc_ref, wv_ref, wconv_ref, *rest,
                         tiles_per_seq, n_cast):
    cast_src, (gate_ref, utail_ref), cast_dst, (h_ref, carry_ref) = (
        rest[:n_cast], rest[n_cast:n_cast + 2], rest[n_cast + 2:2 * n_cast + 2],
        rest[2 * n_cast + 2:])
    i = pl.program_id(0)
    j = pl.program_id(1)

    @pl.when(i % tiles_per_seq == 0)
    def _():
        carry_ref[j] = jnp.zeros(carry_ref.shape[1:], F32)

    tm = h_ref.shape[0]
    chunk = min(tm, CONV_ROW_CHUNK)
    row8 = lax.broadcasted_iota(jnp.int32, (SUBLANES, 1), 0)
    w_conv = wconv_ref[...]

    def run(first_column_tile):
        _cast_blocks(cast_src, cast_dst)
        prev = carry_ref[j]
        for c in range(tm // chunk):
            rows = pl.ds(c * chunk, chunk)
            if first_column_tile:
                h = _rms_norm(x_ref[rows, :], g_ref[...]).astype(BF16)
                h_ref[rows, :] = h
            else:
                h = h_ref[rows, :]
            u = _dot(h, wc_ref[...]) * _dot(h, wv_ref[...])
            u1 = pltpu.roll(u, 1, axis=0)
            u2 = pltpu.roll(u, 2, axis=0)
            top1 = jnp.where(row8 < 1, pltpu.roll(prev, 1, axis=0), u1[0:SUBLANES])
            top2 = jnp.where(row8 < 2, pltpu.roll(prev, 2, axis=0), u2[0:SUBLANES])
            u1 = jnp.concatenate([top1, u1[SUBLANES:]], axis=0)
            u2 = jnp.concatenate([top2, u2[SUBLANES:]], axis=0)
            conv = _conv_taps(u, u1, u2, w_conv)
            gate_ref[rows, :] = (_dot(h, wb_ref[...]) * conv).astype(BF16)
            prev = u[chunk - SUBLANES:]
        carry_ref[j] = prev
        utail_ref[0] = prev

    pl.when(j == 0)(functools.partial(run, True))
    pl.when(j != 0)(functools.partial(run, False))


def _conv_in_sample_body(x_ref, g_ref, wb_ref, wc_ref, wv_ref, wconv_ref, p1_ref, p2_ref,
                         gate_ref, u_ref, h_ref, *, seq_len):
    j = pl.program_id(1)

    @pl.when(j == 0)
    def _():
        h_ref[...] = _rms_norm(x_ref[...], g_ref[...]).astype(BF16)

    h = h_ref[...]
    u = _dot(h, wc_ref[...]) * _dot(h, wv_ref[...])
    step = lax.broadcasted_iota(jnp.int32, (u.shape[0], 1), 0) % seq_len
    u1 = jnp.where(step < 1, p1_ref[...], pltpu.roll(u, 1, axis=0))
    u2 = jnp.where(step < 2, p2_ref[...], pltpu.roll(u, 2, axis=0))
    conv = _conv_taps(u, u1, u2, wconv_ref[...])
    u_ref[...] = u
    gate_ref[...] = (_dot(h, wb_ref[...]) * conv).astype(BF16)


def _conv_in_specs(tm, d, tn):
    third = d // tn
    return [
        pl.BlockSpec((tm, d), lambda i, j: (i, 0)),
        pl.BlockSpec((1, d), lambda i, j: (0, 0)),
        pl.BlockSpec((d, tn), lambda i, j: (0, j)),
        pl.BlockSpec((d, tn), lambda i, j: (0, third + j)),
        pl.BlockSpec((d, tn), lambda i, j: (0, 2 * third + j)),
        pl.BlockSpec((CONV_WIDTH, tn), lambda i, j: (0, j)),
    ]


def _conv_in_prompt(x, gain, w_in, w_conv, *, seq, tm, tn, cast=()):
    m, d = x.shape
    grid = (m // tm, d // tn)
    cast_in, cast_out, cast_shapes = _cast_specs(cast, grid)
    body = functools.partial(_conv_in_prompt_body, tiles_per_seq=seq // tm, n_cast=len(cast))
    return pl.pallas_call(
        body,
        grid=grid,
        in_specs=_conv_in_specs(tm, d, tn) + cast_in,
        out_specs=[
            pl.BlockSpec((tm, tn), lambda i, j: (i, j)),
            pl.BlockSpec((1, SUBLANES, tn), lambda i, j: (i, 0, j)),
        ] + cast_out,
        out_shape=[
            jax.ShapeDtypeStruct((m, d), BF16),
            jax.ShapeDtypeStruct((m // tm, SUBLANES, d), F32),
        ] + cast_shapes,
        scratch_shapes=[
            pltpu.VMEM((tm, d), BF16),
            pltpu.VMEM((d // tn, SUBLANES, tn), F32),
        ],
        compiler_params=_params(2, 56),
        name="conv_in_prompt",
    )(x, gain, w_in, w_in, w_in, w_conv, *[w for w, _ in cast])


def _conv_in_sample(x, gain, w_in, w_conv, p1, p2, *, seq_len, tn):
    m, d = x.shape
    body = functools.partial(_conv_in_sample_body, seq_len=seq_len)
    return pl.pallas_call(
        body,
        grid=(1, d // tn),
        in_specs=_conv_in_specs(m, d, tn) + [
            pl.BlockSpec((m, tn), lambda i, j: (0, j)),
            pl.BlockSpec((m, tn), lambda i, j: (0, j)),
        ],
        out_specs=[
            pl.BlockSpec((m, tn), lambda i, j: (0, j)),
            pl.BlockSpec((m, tn), lambda i, j: (0, j)),
        ],
        out_shape=[
            jax.ShapeDtypeStruct((m, d), BF16),
            jax.ShapeDtypeStruct((m, d), F32),
        ],
        scratch_shapes=[pltpu.VMEM((m, d), BF16)],
        compiler_params=_params(2, 48),
        name="conv_in_sample",
    )(x, gain, w_in, w_in, w_in, w_conv, p1, p2)


def _head_norm_rope(z, gain, ones_blockdiag, cos, sin_signed, *, split_ssq):
    w = z.shape[1]
    zz = z * z
    hi = zz.astype(BF16)
    ssq = _dot(hi, ones_blockdiag)
    if split_ssq:
        ssq += _dot((zz - hi.astype(F32)).astype(BF16), ones_blockdiag)
    zn = z * lax.rsqrt(ssq * (1.0 / HEAD_DIM) + EPS) * gain
    reps = w // cos.shape[1]
    cos = jnp.concatenate([cos] * reps, axis=1)
    sin_signed = jnp.concatenate([sin_signed] * reps, axis=1)
    lane = lax.broadcasted_iota(jnp.int32, (1, w), 1)
    first_half = (lane % HEAD_DIM) < (HEAD_DIM // 2)
    partner = jnp.where(first_half,
                        pltpu.roll(zn, w - HEAD_DIM // 2, axis=1),
                        pltpu.roll(zn, HEAD_DIM // 2, axis=1))
    return zn * cos + partner * sin_signed


def _qkv_body(x_ref, g_ref, w_ref, ones_ref, qgain_ref, kgain_ref,
              cos_p_ref, sin_p_ref, cos_s_ref, sin_s_ref, *rest, seq_tiles, n_cast):
    cast_src, (qp_ref, qs_ref, kvp_ref, kvs_ref, kv_tail_ref), cast_dst = (
        rest[:n_cast], rest[n_cast:n_cast + 5], rest[n_cast + 5:])
    _cast_blocks(cast_src, cast_dst)
    q_out, kv_out = _RowParts((qp_ref, qs_ref)), _RowParts((kvp_ref, kvs_ref))
    cos_tile, sin_tile = _RowParts((cos_p_ref, cos_s_ref)), _RowParts((sin_p_ref, sin_s_ref))
    tm = x_ref.shape[0]
    qw = qp_ref.shape[1]
    tn = ones_ref.shape[0]
    kw = N_KV_HEADS * HEAD_DIM
    for start, size in _row_chunks(tm, QKV_ROW_CHUNK):
        h = _rms_norm(x_ref[pl.ds(start, size), :], g_ref[...]).astype(BF16)
        cos, sin_signed = cos_tile.load(start, size), sin_tile.load(start, size)
        for t in range(qw // tn):
            cols = pl.ds(t * tn, tn)
            z = _dot(h, w_ref[:, cols])
            q = _head_norm_rope(z, qgain_ref[...], ones_ref[...], cos, sin_signed,
                                split_ssq=False)
            q_out.store(start, (q * Q_SCALE).astype(BF16), cols=cols)
        z = _dot(h, w_ref[:, pl.ds(qw, 2 * kw)])
        k = _head_norm_rope(z[:, :kw], kgain_ref[...], ones_ref[0:kw, 0:kw], cos, sin_signed,
                            split_ssq=True)
        kv_out.store(start, jnp.concatenate([k, z[:, kw:]], axis=1))

    @pl.when(pl.program_id(0) % seq_tiles == seq_tiles - 1)
    def _():
        kv_tail_ref[...] = kvp_ref[kvp_ref.shape[0] - WINDOW:, :]


def _qkv(x, gain, w_qkv, ones_blockdiag, q_gain, k_gain, rope_prompt, rope_sample, *,
         n_tiles, prompt_rows, seq_tiles, cast=()):
    m, d = x.shape
    cast_in, cast_out, cast_shapes = _cast_specs(cast, (n_tiles,))
    tm = m // n_tiles
    sample_rows = tm - prompt_rows
    qw = N_HEADS * HEAD_DIM
    kvw = 2 * N_KV_HEADS * HEAD_DIM
    tn = ones_blockdiag.shape[0]
    const = lambda shape: pl.BlockSpec(shape, lambda i: (0, 0), pipeline_mode=pl.Buffered(1))
    rope_p = pl.BlockSpec((prompt_rows, LANES), lambda i: (i % seq_tiles, 0))
    split = lambda width: [pl.BlockSpec((prompt_rows, width), lambda i: (i, 0)),
                           pl.BlockSpec((sample_rows, width), lambda i: (i, 0))]
    split_shape = lambda width, dtype: [
        jax.ShapeDtypeStruct((prompt_rows * n_tiles, width), dtype),
        jax.ShapeDtypeStruct((sample_rows * n_tiles, width), dtype)]
    return pl.pallas_call(
        functools.partial(_qkv_body, seq_tiles=seq_tiles, n_cast=len(cast)),
        grid=(n_tiles,),
        in_specs=[
            pl.BlockSpec((tm, d), lambda i: (i, 0)),
            const((1, d)),
            const(w_qkv.shape),
            const((tn, tn)),
            const((1, tn)),
            const((1, kvw // 2)),
            rope_p, rope_p,
            const((sample_rows, LANES)), const((sample_rows, LANES)),
        ] + cast_in,
        out_specs=split(qw) + split(kvw) + [
            pl.BlockSpec((None, WINDOW, kvw), lambda i: (i // seq_tiles, 0, 0))] + cast_out,
        out_shape=split_shape(qw, BF16) + split_shape(kvw, F32) + [
            jax.ShapeDtypeStruct((n_tiles // seq_tiles, WINDOW, kvw), F32)] + cast_shapes,
        compiler_params=_params(1, 56),
        name="qkv",
    )(x, gain, w_qkv, ones_blockdiag, q_gain, k_gain, *rope_prompt, *rope_sample,
      *[w for w, _ in cast])


MASKED = -1e30
LOG2_E = math.log2(math.e)
Q_SCALE = HEAD_DIM ** -0.5 * LOG2_E


def _attn_prompt_body(sinks_ref, q_ref, kv_prev_ref, kv_cur_ref, os_ref, x_ref, wo_ref,
                      out_ref, o_ref):
    n = pl.program_id(1)
    blk = WINDOW
    n_blocks = q_ref.shape[0] // blk
    kw = N_KV_HEADS * HEAD_DIM
    r = lax.broadcasted_iota(jnp.int32, (blk, 2 * blk), 0)
    c = lax.broadcasted_iota(jnp.int32, (blk, 2 * blk), 1)
    band = (c > r + (blk - WINDOW)) & (c <= r + blk)
    band_at_sequence_start = band & ((c >= blk) | (n > 0))
    lower = lax.broadcasted_iota(jnp.int32, (1, LANES), 1) < HEAD_DIM
    heads_per_col = LANES // HEAD_DIM

    def padded_operands(kv):
        per_head = []
        for g in range(N_KV_HEADS):
            col, half = divmod(g, heads_per_col)
            kcol = kv[:, col * LANES:(col + 1) * LANES]
            vcol = kv[:, kw + col * LANES:kw + (col + 1) * LANES]
            kswap = pltpu.roll(kcol, HEAD_DIM, axis=1)
            vswap = pltpu.roll(vcol, HEAD_DIM, axis=1)
            in_lower = (kcol, vcol) if half == 0 else (kswap, vswap)
            in_upper = (kswap, vswap) if half == 0 else (kcol, vcol)
            per_head.append(tuple(jnp.where(lower, t, 0.0).astype(BF16) for t in in_lower)
                            + tuple(jnp.where(lower, 0.0, t).astype(BF16) for t in in_upper))
        return per_head

    key_blocks = [padded_operands(kv_prev_ref[...])]
    for b in range(n_blocks):
        rows = pl.ds(b * blk, blk)
        key_blocks.append(padded_operands(kv_cur_ref[rows, :]))
        visible = band_at_sequence_start if b == 0 else band
        for g in range(N_KV_HEADS):
            k_lo, v_lo, k_hi, v_hi = (jnp.concatenate(prev_and_own, axis=0) for prev_and_own
                                      in zip(key_blocks[b][g], key_blocks[b + 1][g]))
            for pair in range(GROUP // heads_per_col):
                h0 = g * GROUP + pair * heads_per_col
                lanes = pl.ds(h0 * HEAD_DIM, LANES)
                q_pair = q_ref[rows, lanes]
                acc = None
                for h, k, v in ((h0, k_lo, v_lo), (h0 + 1, k_hi, v_hi)):
                    s = lax.dot_general(q_pair, k, (((1,), (1,)), ((), ())),
                                        preferred_element_type=F32)
                    s = jnp.where(visible, s, MASKED)
                    sink = sinks_ref[h] * LOG2_E
                    m = jnp.maximum(jnp.max(s, axis=-1, keepdims=True), sink)
                    p = jnp.exp2(s - m)
                    den = jnp.sum(p, axis=-1, keepdims=True) + jnp.exp2(sink - m)
                    o = _dot(p.astype(BF16), v) * (1.0 / den)
                    acc = o if acc is None else acc + o
                o_ref[rows, lanes] = acc.astype(BF16)
        o_blk, first, size = o_ref[rows, :], b * blk, blk
        if b == n_blocks - 1:
            o_blk = jnp.concatenate([o_blk, os_ref[...]], axis=0)
            size = blk + os_ref.shape[0]
        out_rows = pl.ds(first, size)
        out_ref[out_rows, :] = x_ref[out_rows, :] + _dot(o_blk, wo_ref[...])


def _attn_prompt_out(sinks, q, kv, o_sample, x, w_out, *, batch, seq, tq):
    m, qw = q.shape
    kvw = kv.shape[1]
    d = x.shape[1]
    blk = WINDOW
    nt = seq // tq
    per_tile = tq // blk
    n_tiles = batch * nt
    tm = x.shape[0] // n_tiles
    rows_s = o_sample.shape[0] // n_tiles
    assert tm == tq + rows_s
    tile = lambda b, n, s: (b * nt + n, 0)
    return pl.pallas_call(
        _attn_prompt_body,
        grid_spec=pltpu.PrefetchScalarGridSpec(
            num_scalar_prefetch=1,
            grid=(batch, nt),
            in_specs=[
                pl.BlockSpec((tq, qw), tile),
                pl.BlockSpec((blk, kvw), lambda b, n, s: (
                    (b * nt + n) * per_tile - jnp.minimum(n, 1), 0)),
                pl.BlockSpec((tq, kvw), tile),
                pl.BlockSpec((rows_s, qw), tile),
                pl.BlockSpec((tm, d), tile),
                pl.BlockSpec(w_out.shape, lambda b, n, s: (0, 0), pipeline_mode=pl.Buffered(1)),
            ],
            out_specs=pl.BlockSpec((tm, d), tile),
            scratch_shapes=[pltpu.VMEM((tq, qw), BF16)],
        ),
        out_shape=jax.ShapeDtypeStruct(x.shape, F32),
        compiler_params=_params(2, 56),
        name="attn_prompt_out",
    )(sinks, q, kv, kv, o_sample, x, w_out)


def _attn_sample_body(sinks_ref, q_ref, kv_ref, ck_ref, cv_ref, o_ref, knew_ref, vnew_ref, *,
                      seq_len):
    nseq, win, kw = ck_ref.shape
    rows = GROUP * seq_len
    q = q_ref[...].astype(F32)
    kv_new = kv_ref[...]
    for new_ref, cache_ref, fresh in ((knew_ref, ck_ref, kv_new[:, :kw]),
                                      (vnew_ref, cv_ref, kv_new[:, kw:])):
        new_ref[:, :win - seq_len, :] = cache_ref[:, seq_len:, :]
        new_ref[:, win - seq_len:, :] = fresh.reshape(nseq, seq_len, kw)
    pad = jnp.zeros((nseq, win - seq_len, HEAD_DIM), F32)
    step = lax.broadcasted_iota(jnp.int32, (1, rows, 2 * win), 1) % seq_len
    c = lax.broadcasted_iota(jnp.int32, (1, rows, 2 * win), 2)
    visible = ((c < win) & (c > step + (win - WINDOW))) | ((c >= win) & (c - win <= step))
    pieces = []
    for g in range(N_KV_HEADS):
        lo, hi = g * HEAD_DIM, (g + 1) * HEAD_DIM
        heads = range(g * GROUP, (g + 1) * GROUP)
        k_new = kv_new[:, lo:hi].reshape(nseq, seq_len, HEAD_DIM)
        v_new = kv_new[:, kw + lo:kw + hi].reshape(nseq, seq_len, HEAD_DIM)
        k = jnp.concatenate([ck_ref[:, :, lo:hi], k_new, pad], axis=1).astype(BF16)
        v = jnp.concatenate([cv_ref[:, :, lo:hi], v_new, pad], axis=1).astype(BF16)
        qg = jnp.concatenate(
            [q[:, h * HEAD_DIM:(h + 1) * HEAD_DIM].reshape(nseq, seq_len, HEAD_DIM)
             for h in heads], axis=1).astype(BF16)
        s = jnp.einsum("bqd,bkd->bqk", qg, k, preferred_element_type=F32)
        s = jnp.where(visible, s, MASKED)
        sink_col = jnp.concatenate(
            [jnp.full((1, seq_len, 1), sinks_ref[h] * LOG2_E, F32) for h in heads], axis=1)
        m = jnp.maximum(jnp.max(s, axis=-1, keepdims=True), sink_col)
        p = jnp.exp2(s - m)
        den = jnp.sum(p, axis=-1, keepdims=True) + jnp.exp2(sink_col - m)
        o = jnp.einsum("bqk,bkd->bqd", p.astype(BF16), v, preferred_element_type=F32) / den
        pieces += [o[:, t * seq_len:(t + 1) * seq_len].reshape(nseq * seq_len, HEAD_DIM)
                   for t in range(GROUP)]
    o_ref[...] = jnp.concatenate(pieces, axis=1).astype(BF16)


def _attn_sample(sinks, q, kv, cache_k, cache_v, *, seq_len):
    m, qw = q.shape
    body = functools.partial(_attn_sample_body, seq_len=seq_len)
    full = lambda shape: pl.BlockSpec(shape, lambda i, s: (0,) * len(shape))
    return pl.pallas_call(
        body,
        grid_spec=pltpu.PrefetchScalarGridSpec(
            num_scalar_prefetch=1,
            grid=(1,),
            in_specs=[full(q.shape), full(kv.shape), full(cache_k.shape), full(cache_v.shape)],
            out_specs=[full((m, qw)), full(cache_k.shape), full(cache_v.shape)],
        ),
        out_shape=[jax.ShapeDtypeStruct((m, qw), BF16),
                   jax.ShapeDtypeStruct(cache_k.shape, F32),
                   jax.ShapeDtypeStruct(cache_v.shape, F32)],
        compiler_params=_params(1, 48),
        name="attn_sample",
    )(sinks, q, kv, cache_k, cache_v)


def _rope_tables(pos):
    half = HEAD_DIM // 2
    inv = ROPE_THETA ** (-jnp.arange(half, dtype=F32) / half)
    ang = pos.astype(F32)[:, None] * inv[None, :]
    cos, sin = jnp.cos(ang), jnp.sin(ang)
    reps = LANES // HEAD_DIM
    cos = jnp.tile(jnp.concatenate([cos, cos], axis=1), (1, reps))
    sin_signed = jnp.tile(jnp.concatenate([-sin, sin], axis=1), (1, reps))
    return cos, sin_signed


def kernel(x_prompt, x_sample, state_conv, cache_k_win, cache_v_win, ln_mix, ln_mlp,
           w_conv_in, w_conv, w_conv_out, w_qkv, w_attn_out, q_norm, k_norm, sinks,
           w_up, w_down):
    bp, tp, d = x_prompt.shape
    bs, ts, _ = x_sample.shape
    win = cache_k_win.shape[2]
    kw = N_KV_HEADS * HEAD_DIM
    assert ts == SUBLANES and win == WINDOW and tp % WINDOW == 0

    tm_p = 512
    tm_conv = 1024
    m_s = bs * ts
    n_tiles = bp * tp // tm_p
    rows_s = m_s // n_tiles
    assert rows_s % (2 * SUBLANES) == 0 and rows_s % ts == 0

    xp = x_prompt.reshape(bp * tp, d)
    xs = x_sample.reshape(m_s, d)

    w_in = w_conv_in[0].astype(BF16)
    gain = ln_mix[0][None]
    gate_p, u_tail, wu0, wd0, w_out = _conv_in_prompt(
        xp, gain, w_in, w_conv[0], seq=tp, tm=tm_conv, tn=512,
        cast=[(w_up, 0), (w_down, 0), (w_conv_out, 0)])
    past = state_conv[0]
    zeros = lambda n: jnp.zeros((bs, n, d), F32)
    p1 = jnp.concatenate([past[:, 1:2], zeros(ts - 1)], axis=1).reshape(m_s, d)
    p2 = jnp.concatenate([past, zeros(ts - 2)], axis=1).reshape(m_s, d)
    gate_s, u_s = _conv_in_sample(xs, gain, w_in, w_conv[0], p1, p2, seq_len=ts, tn=512)
    x = _proj_residual((xp, xs), (gate_p, gate_s), w_out, n_tiles=n_tiles)
    tiles_per_seq = tp // tm_conv
    new_conv_prompt = u_tail[tiles_per_seq - 1::tiles_per_seq, SUBLANES - (CONV_WIDTH - 1):][None]
    new_conv_sample = u_s.reshape(bs, ts, d)[:, ts - (CONV_WIDTH - 1):][None]

    x, wqkv, wo = _mlp(x, ln_mlp[0][None], wu0, wd0, n_tiles=n_tiles, tf=2048,
                       cast=[(w_qkv, 0), (w_attn_out, 0)])

    gain = ln_mix[1][None]
    tn = 2 * kw
    head_id = jnp.arange(tn) // HEAD_DIM
    ones_blockdiag = (head_id[:, None] == head_id[None, :]).astype(BF16)
    q_gain = jnp.tile(q_norm[0], tn // HEAD_DIM)[None]
    k_gain = jnp.tile(k_norm[0], kw // HEAD_DIM)[None]
    rope_p = _rope_tables(jnp.arange(tp, dtype=jnp.int32))
    rope_s = _rope_tables(PAST_LEN + jnp.arange(ts, dtype=jnp.int32))
    rope_s = tuple(jnp.tile(t, (rows_s // ts, 1)) for t in rope_s)
    q_p, q_s, kv_p, kv_s, kv_tail, wu1, wd1 = _qkv(
        x, gain, wqkv, ones_blockdiag, q_gain, k_gain, rope_p, rope_s,
        n_tiles=n_tiles, prompt_rows=tm_p, seq_tiles=tp // tm_p,
        cast=[(w_up, 1), (w_down, 1)])
    ck = cache_k_win[0].reshape(bs, win, kw)
    cv = cache_v_win[0].reshape(bs, win, kw)
    o_s, new_k_sample, new_v_sample = _attn_sample(sinks[0], q_s, kv_s, ck, cv, seq_len=ts)
    x = _attn_prompt_out(sinks[0], q_p, kv_p, o_s, x, wo, batch=bp, seq=tp, tq=tm_p)

    new_k_prompt = kv_tail[:, :, :kw].reshape(1, bp, WINDOW, N_KV_HEADS, HEAD_DIM)
    new_v_prompt = kv_tail[:, :, kw:].reshape(1, bp, WINDOW, N_KV_HEADS, HEAD_DIM)
    new_k_sample = new_k_sample.reshape(1, bs, win, N_KV_HEADS, HEAD_DIM)
    new_v_sample = new_v_sample.reshape(1, bs, win, N_KV_HEADS, HEAD_DIM)

    xp, xs = _mlp(x, ln_mlp[1][None], wu1, wd1, n_tiles=n_tiles, tf=2048,
                  out_rows=(tm_p, rows_s))

    return (xp.reshape(bp, tp, d), xs.reshape(bs, ts, d), new_conv_prompt, new_conv_sample,
            new_k_prompt, new_v_prompt, new_k_sample, new_v_sample)
```

```python
import functools
import math

import jax
import jax.numpy as jnp
from jax import lax
from jax.experimental import pallas as pl
from jax.experimental.pallas import tpu as pltpu

F32 = jnp.float32
BF16 = jnp.bfloat16

HEAD_DIM = 64
N_HEADS = 32
N_KV_HEADS = 4
GROUP = N_HEADS // N_KV_HEADS
WINDOW = 128
PAST_LEN = 16384
ROPE_THETA = 10000.0
EPS = 1e-6
CONV_WIDTH = 3

SUBLANES = 8
LANES = 128
MIB = 1024 * 1024
QKV_ROW_CHUNK = 256
CONV_ROW_CHUNK = 256
MLP_NORM_ROW_CHUNK = 256
MLP_FF_CHUNK = 1024


def _params(n_grid_dims, vmem_mib):
    return pltpu.CompilerParams(
        dimension_semantics=("arbitrary",) * n_grid_dims,
        vmem_limit_bytes=vmem_mib * MIB,
    )


def _rms_norm(x, gain):
    return x * lax.rsqrt(jnp.mean(x * x, axis=-1, keepdims=True) + EPS) * gain


def _dot(a, b):
    return jnp.dot(a, b, preferred_element_type=F32)


def _cast_specs(weights, grid):
    n_steps = 1
    for g in grid:
        n_steps *= g

    def step_of(*ids):
        step = ids[0]
        for g, idx in zip(grid[1:], ids[1:len(grid)]):
            step = step * g + idx
        return step

    in_specs, out_specs, out_shapes = [], [], []
    for w, layer in weights:
        rows, cols = w.shape[1:]
        assert rows % (n_steps * 2 * SUBLANES) == 0, (w.shape, n_steps)
        rb = rows // n_steps
        in_specs.append(pl.BlockSpec((None, rb, cols),
                                     lambda *ids, layer=layer: (layer, step_of(*ids), 0)))
        out_specs.append(pl.BlockSpec((rb, cols), lambda *ids: (step_of(*ids), 0)))
        out_shapes.append(jax.ShapeDtypeStruct((rows, cols), BF16))
    return in_specs, out_specs, out_shapes


def _cast_blocks(src_refs, dst_refs):
    for src, dst in zip(src_refs, dst_refs):
        dst[...] = src[...].astype(BF16)


class _RowParts:
    def __init__(self, refs):
        self.refs = tuple(refs)
        self.rows = sum(r.shape[0] for r in self.refs)

    def _pieces(self, start, size):
        offset = 0
        for ref in self.refs:
            lo, hi = max(start, offset), min(start + size, offset + ref.shape[0])
            if lo < hi:
                yield ref, lo - offset, lo - start, hi - lo
            offset += ref.shape[0]

    def load(self, start=0, size=None):
        size = self.rows - start if size is None else size
        pieces = [ref[at:at + n, :] for ref, at, _, n in self._pieces(start, size)]
        return pieces[0] if len(pieces) == 1 else jnp.concatenate(pieces, axis=0)

    def store(self, start, value, cols=slice(None), accumulate=False):
        for ref, at, src, n in self._pieces(start, value.shape[0]):
            if accumulate:
                ref[at:at + n, cols] += value[src:src + n]
            else:
                ref[at:at + n, cols] = value[src:src + n]


def _row_chunks(rows, chunk):
    n = max(rows // chunk, 1)
    return [(c * chunk, chunk if c < n - 1 else rows - c * chunk) for c in range(n)]


def _mlp_body(x_ref, g_ref, wu_ref, wd_ref, *rest, n_cast, n_out, ff_chunk):
    cast_src, out_refs, cast_dst, (h_ref,) = (
        rest[:n_cast], rest[n_cast:n_cast + n_out], rest[n_cast + n_out:2 * n_cast + n_out],
        rest[2 * n_cast + n_out:])
    out = _RowParts(out_refs)
    tm = x_ref.shape[0]
    tf = wu_ref.shape[1]

    def add_ffn(start, h, base):
        for t in range(tf // ff_chunk):
            cols = pl.ds(t * ff_chunk, ff_chunk)
            a = jnp.maximum(_dot(h, wu_ref[:, cols]), 0.0)
            part = _dot((a * a).astype(BF16), wd_ref[cols, :])
            if base is None:
                out.store(start, part, accumulate=True)
            else:
                out.store(start, base + part)
                base = None

    @pl.when(pl.program_id(1) == 0)
    def _():
        _cast_blocks(cast_src, cast_dst)
        for start, size in _row_chunks(tm, MLP_NORM_ROW_CHUNK):
            rows = pl.ds(start, size)
            x = x_ref[rows, :]
            h = _rms_norm(x, g_ref[...]).astype(BF16)
            h_ref[rows, :] = h
            add_ffn(start, h, x)

    @pl.when(pl.program_id(1) != 0)
    def _():
        _cast_blocks(cast_src, cast_dst)
        add_ffn(0, h_ref[...], None)


def _mlp(x, gain, w_up, w_down, *, n_tiles, tf, out_rows=None, cast=()):
    m, d = x.shape
    tm = m // n_tiles
    out_rows = (tm,) if out_rows is None else out_rows
    assert sum(out_rows) == tm
    ff = w_up.shape[1]
    grid = (n_tiles, ff // tf)
    cast_in, cast_out, cast_shapes = _cast_specs(cast, grid)
    body = functools.partial(_mlp_body, n_cast=len(cast), n_out=len(out_rows),
                             ff_chunk=min(tf, MLP_FF_CHUNK))
    return pl.pallas_call(
        body,
        grid=grid,
        in_specs=[
            pl.BlockSpec((tm, d), lambda i, f: (i, 0)),
            pl.BlockSpec((1, d), lambda i, f: (0, 0)),
            pl.BlockSpec((d, tf), lambda i, f: (0, f)),
            pl.BlockSpec((tf, d), lambda i, f: (f, 0)),
        ] + cast_in,
        out_specs=[pl.BlockSpec((r, d), lambda i, f: (i, 0)) for r in out_rows] + cast_out,
        out_shape=[jax.ShapeDtypeStruct((r * n_tiles, d), F32) for r in out_rows] + cast_shapes,
        scratch_shapes=[pltpu.VMEM((tm, d), BF16)],
        compiler_params=_params(2, 58),
        name="mlp",
    )(x, gain, w_up, w_down, *[w for w, _ in cast])


def _proj_body(*refs, n_x, n_a):
    x, a = _RowParts(refs[:n_x]), _RowParts(refs[n_x:n_x + n_a])
    w_ref, o_ref = refs[n_x + n_a:]
    o_ref[...] = x.load() + _dot(a.load(), w_ref[...])


def _proj_residual(x_parts, a_parts, w, *, n_tiles):
    d = w.shape[1]
    tm = sum(p.shape[0] for p in x_parts) // n_tiles
    assert tm == sum(p.shape[0] for p in a_parts) // n_tiles
    tile_spec = lambda p: pl.BlockSpec((p.shape[0] // n_tiles, p.shape[1]), lambda i: (i, 0))
    return pl.pallas_call(
        functools.partial(_proj_body, n_x=len(x_parts), n_a=len(a_parts)),
        grid=(n_tiles,),
        in_specs=[tile_spec(p) for p in (*x_parts, *a_parts)] + [
            pl.BlockSpec(w.shape, lambda i: (0, 0), pipeline_mode=pl.Buffered(1))],
        out_specs=pl.BlockSpec((tm, d), lambda i: (i, 0)),
        out_shape=jax.ShapeDtypeStruct((tm * n_tiles, d), F32),
        compiler_params=_params(1, 48),
        name="proj_residual",
    )(*x_parts, *a_parts, w)


def _conv_taps(u, u1, u2, w):
    return w[0:1, :] * u2 + w[1:2, :] * u1 + w[2:3, :] * u


def _conv_in_prompt_body(x_ref, g_ref, wb_ref, wc_ref, wv_ref, wconv_ref, *rest,
                         tiles_per_seq, n_cast):
    cast_src, (gate_ref, utail_ref), cast_dst, (h_ref, carry_ref) = (
        rest[:n_cast], rest[n_cast:n_cast + 2], rest[n_cast + 2:2 * n_cast + 2],
        rest[2 * n_cast + 2:])
    i = pl.program_id(0)
    j = pl.program_id(1)

    @pl.when(i % tiles_per_seq == 0)
    def _():
        carry_ref[j] = jnp.zeros(carry_ref.shape[1:], F32)

    tm = h_ref.shape[0]
    chunk = min(tm, CONV_ROW_CHUNK)
    row8 = lax.broadcasted_iota(jnp.int32, (SUBLANES, 1), 0)
    w_conv = wconv_ref[...]

    def run(first_column_tile):
        _cast_blocks(cast_src, cast_dst)
        prev = carry_ref[j]
        for c in range(tm // chunk):
            rows = pl.ds(c * chunk, chunk)
            if first_column_tile:
                h = _rms_norm(x_ref[rows, :], g_ref[...]).astype(BF16)
                h_ref[rows, :] = h
            else:
                h = h_ref[rows, :]
            u = _dot(h, wc_ref[...]) * _dot(h, wv_ref[...])
            u1 = pltpu.roll(u, 1, axis=0)
            u2 = pltpu.roll(u, 2, axis=0)
            top1 = jnp.where(row8 < 1, pltpu.roll(prev, 1, axis=0), u1[0:SUBLANES])
            top2 = jnp.where(row8 < 2, pltpu.roll(prev, 2, axis=0), u2[0:SUBLANES])
            u1 = jnp.concatenate([top1, u1[SUBLANES:]], axis=0)
            u2 = jnp.concatenate([top2, u2[SUBLANES:]], axis=0)
            conv = _conv_taps(u, u1, u2, w_conv)
            gate_ref[rows, :] = (_dot(h, wb_ref[...]) * conv).astype(BF16)
            prev = u[chunk - SUBLANES:]
        carry_ref[j] = prev
        utail_ref[0] = prev

    pl.when(j == 0)(functools.partial(run, True))
    pl.when(j != 0)(functools.partial(run, False))


def _conv_in_sample_body(x_ref, g_ref, wb_ref, wc_ref, wv_ref, wconv_ref, p1_ref, p2_ref,
                         gate_ref, u_ref, wb16_ref, wc16_ref, wv16_ref, h_ref, *, seq_len):
    j = pl.program_id(1)

    @pl.when(j == 0)
    def _():
        h_ref[...] = _rms_norm(x_ref[...], g_ref[...]).astype(BF16)

    _cast_blocks((wb_ref, wc_ref, wv_ref), (wb16_ref, wc16_ref, wv16_ref))
    h = h_ref[...]
    u = _dot(h, wc16_ref[...]) * _dot(h, wv16_ref[...])
    step = lax.broadcasted_iota(jnp.int32, (u.shape[0], 1), 0) % seq_len
    u1 = jnp.where(step < 1, p1_ref[...], pltpu.roll(u, 1, axis=0))
    u2 = jnp.where(step < 2, p2_ref[...], pltpu.roll(u, 2, axis=0))
    conv = _conv_taps(u, u1, u2, wconv_ref[...])
    u_ref[...] = u
    gate_ref[...] = (_dot(h, wb16_ref[...]) * conv).astype(BF16)


def _conv_in_specs(tm, d, tn, *, fused_weight):
    third = d // tn if fused_weight else 0
    return [
        pl.BlockSpec((tm, d), lambda i, j: (i, 0)),
        pl.BlockSpec((1, d), lambda i, j: (0, 0)),
        pl.BlockSpec((d, tn), lambda i, j: (0, j)),
        pl.BlockSpec((d, tn), lambda i, j: (0, third + j)),
        pl.BlockSpec((d, tn), lambda i, j: (0, 2 * third + j)),
        pl.BlockSpec((CONV_WIDTH, tn), lambda i, j: (0, j)),
    ]


def _conv_in_prompt(x, gain, w_bcv, w_conv, *, seq, tm, tn, cast=()):
    m, d = x.shape
    grid = (m // tm, d // tn)
    cast_in, cast_out, cast_shapes = _cast_specs(cast, grid)
    body = functools.partial(_conv_in_prompt_body, tiles_per_seq=seq // tm, n_cast=len(cast))
    return pl.pallas_call(
        body,
        grid=grid,
        in_specs=_conv_in_specs(tm, d, tn, fused_weight=False) + cast_in,
        out_specs=[
            pl.BlockSpec((tm, tn), lambda i, j: (i, j)),
            pl.BlockSpec((1, SUBLANES, tn), lambda i, j: (i, 0, j)),
        ] + cast_out,
        out_shape=[
            jax.ShapeDtypeStruct((m, d), BF16),
            jax.ShapeDtypeStruct((m // tm, SUBLANES, d), F32),
        ] + cast_shapes,
        scratch_shapes=[
            pltpu.VMEM((tm, d), BF16),
            pltpu.VMEM((d // tn, SUBLANES, tn), F32),
        ],
        compiler_params=_params(2, 56),
        name="conv_in_prompt",
    )(x, gain, *w_bcv, w_conv, *[w for w, _ in cast])


def _conv_in_sample(x, gain, w_in, w_conv, p1, p2, *, seq_len, tn):
    m, d = x.shape
    body = functools.partial(_conv_in_sample_body, seq_len=seq_len)
    column_tile = lambda rows: pl.BlockSpec((rows, tn), lambda i, j: (0, j))
    return pl.pallas_call(
        body,
        grid=(1, d // tn),
        in_specs=_conv_in_specs(m, d, tn, fused_weight=True) + [column_tile(m)] * 2,
        out_specs=[column_tile(m)] * 2 + [column_tile(d)] * 3,
        out_shape=[
            jax.ShapeDtypeStruct((m, d), BF16),
            jax.ShapeDtypeStruct((m, d), F32),
        ] + [jax.ShapeDtypeStruct((d, d), BF16)] * 3,
        scratch_shapes=[pltpu.VMEM((m, d), BF16)],
        compiler_params=_params(2, 48),
        name="conv_in_sample",
    )(x, gain, w_in, w_in, w_in, w_conv, p1, p2)


def _head_norm_rope(z, gain, ones_blockdiag, cos, sin_signed, *, split_ssq):
    w = z.shape[1]
    zz = z * z
    hi = zz.astype(BF16)
    ssq = _dot(hi, ones_blockdiag)
    if split_ssq:
        ssq += _dot((zz - hi.astype(F32)).astype(BF16), ones_blockdiag)
    zn = z * lax.rsqrt(ssq * (1.0 / HEAD_DIM) + EPS) * gain
    reps = w // cos.shape[1]
    cos = jnp.concatenate([cos] * reps, axis=1)
    sin_signed = jnp.concatenate([sin_signed] * reps, axis=1)
    lane = lax.broadcasted_iota(jnp.int32, (1, w), 1)
    first_half = (lane % HEAD_DIM) < (HEAD_DIM // 2)
    partner = jnp.where(first_half,
                        pltpu.roll(zn, w - HEAD_DIM // 2, axis=1),
                        pltpu.roll(zn, HEAD_DIM // 2, axis=1))
    return zn * cos + partner * sin_signed


def _qkv_body(x_ref, g_ref, w_ref, ones_ref, qgain_ref, kgain_ref,
              rope_p_ref, rope_s_ref,
              qp_ref, qs_ref, kvp_ref, kvs_ref, kv_tail_ref, *, seq_tiles):
    q_out, kv_out = _RowParts((qp_ref, qs_ref)), _RowParts((kvp_ref, kvs_ref))
    rope_tile = _RowParts((rope_p_ref, rope_s_ref))
    tm = x_ref.shape[0]
    qw = qp_ref.shape[1]
    tn = ones_ref.shape[0]
    kw = N_KV_HEADS * HEAD_DIM
    for start, size in _row_chunks(tm, QKV_ROW_CHUNK):
        h = _rms_norm(x_ref[pl.ds(start, size), :], g_ref[...]).astype(BF16)
        rope = rope_tile.load(start, size)
        cos, sin_signed = rope[:, :LANES], rope[:, LANES:]
        for t in range(qw // tn):
            cols = pl.ds(t * tn, tn)
            z = _dot(h, w_ref[:, cols])
            q = _head_norm_rope(z, qgain_ref[...], ones_ref[...], cos, sin_signed,
                                split_ssq=False)
            q_out.store(start, (q * Q_SCALE).astype(BF16), cols=cols)
        z = _dot(h, w_ref[:, pl.ds(qw, 2 * kw)])
        k = _head_norm_rope(z[:, :kw], kgain_ref[...], ones_ref[0:kw, 0:kw], cos, sin_signed,
                            split_ssq=True)
        kv_out.store(start, jnp.concatenate([k, z[:, kw:]], axis=1))

    @pl.when(pl.program_id(0) % seq_tiles == seq_tiles - 1)
    def _():
        kv_tail_ref[...] = kvp_ref[kvp_ref.shape[0] - WINDOW:, :]


def _qkv(x, gain, w_qkv, ones_blockdiag, q_gain, k_gain, rope_prompt, rope_sample, *,
         n_tiles, prompt_rows, seq_tiles):
    m, d = x.shape
    tm = m // n_tiles
    sample_rows = tm - prompt_rows
    qw = N_HEADS * HEAD_DIM
    kvw = 2 * N_KV_HEADS * HEAD_DIM
    tn = ones_blockdiag.shape[0]
    const = lambda shape: pl.BlockSpec(shape, lambda i: (0, 0), pipeline_mode=pl.Buffered(1))
    rope_p = pl.BlockSpec((prompt_rows, 2 * LANES), lambda i: (i % seq_tiles, 0))
    split = lambda width: [pl.BlockSpec((prompt_rows, width), lambda i: (i, 0)),
                           pl.BlockSpec((sample_rows, width), lambda i: (i, 0))]
    split_shape = lambda width, dtype: [
        jax.ShapeDtypeStruct((prompt_rows * n_tiles, width), dtype),
        jax.ShapeDtypeStruct((sample_rows * n_tiles, width), dtype)]
    return pl.pallas_call(
        functools.partial(_qkv_body, seq_tiles=seq_tiles),
        grid=(n_tiles,),
        in_specs=[
            pl.BlockSpec((tm, d), lambda i: (i, 0)),
            const((1, d)),
            const(w_qkv.shape),
            const((tn, tn)),
            const((1, tn)),
            const((1, kvw // 2)),
            rope_p,
            const((sample_rows, 2 * LANES)),
        ],
        out_specs=split(qw) + split(kvw) + [
            pl.BlockSpec((None, WINDOW, kvw), lambda i: (i // seq_tiles, 0, 0))],
        out_shape=split_shape(qw, BF16) + split_shape(kvw, F32) + [
            jax.ShapeDtypeStruct((n_tiles // seq_tiles, WINDOW, kvw), F32)],
        compiler_params=_params(1, 48),
        name="qkv",
    )(x, gain, w_qkv, ones_blockdiag, q_gain, k_gain, rope_prompt, rope_sample)


MASKED = -1e30
LOG2_E = math.log2(math.e)
Q_SCALE = HEAD_DIM ** -0.5 * LOG2_E


def _attn_prompt_body(sinks_ref, q_ref, kv_prev_ref, kv_cur_ref, *rest, n_cast):
    cast_src, (o_ref,), cast_dst = rest[:n_cast], rest[n_cast:n_cast + 1], rest[n_cast + 1:]
    _cast_blocks(cast_src, cast_dst)
    n = pl.program_id(1)
    blk = WINDOW
    n_blocks = q_ref.shape[0] // blk
    kw = N_KV_HEADS * HEAD_DIM
    r = lax.broadcasted_iota(jnp.int32, (blk, blk), 0)
    c = lax.broadcasted_iota(jnp.int32, (blk, blk), 1)
    from_prev = c > r
    missing_prev = from_prev & (n == 0)
    lower = lax.broadcasted_iota(jnp.int32, (1, LANES), 1) < HEAD_DIM
    heads_per_col = LANES // HEAD_DIM

    def padded_operands(kv):
        per_head = []
        for g in range(N_KV_HEADS):
            col, half = divmod(g, heads_per_col)
            kcol = kv[:, col * LANES:(col + 1) * LANES]
            vcol = kv[:, kw + col * LANES:kw + (col + 1) * LANES]
            kswap = pltpu.roll(kcol, HEAD_DIM, axis=1)
            vswap = pltpu.roll(vcol, HEAD_DIM, axis=1)
            in_lower = (kcol, vcol) if half == 0 else (kswap, vswap)
            in_upper = (kswap, vswap) if half == 0 else (kcol, vcol)
            per_head.append(tuple(jnp.where(lower, t, 0.0).astype(BF16) for t in in_lower)
                            + tuple(jnp.where(lower, 0.0, t).astype(BF16) for t in in_upper))
        return per_head

    key_blocks = [padded_operands(kv_prev_ref[...])]
    for b in range(n_blocks):
        rows = pl.ds(b * blk, blk)
        key_blocks.append(padded_operands(kv_cur_ref[rows, :]))
        for g in range(N_KV_HEADS):
            k_lo, v_lo, k_hi, v_hi = (jnp.concatenate(prev_and_own, axis=0) for prev_and_own
                                      in zip(key_blocks[b][g], key_blocks[b + 1][g]))
            for pair in range(GROUP // heads_per_col):
                h0 = g * GROUP + pair * heads_per_col
                lanes = pl.ds(h0 * HEAD_DIM, LANES)
                q_pair = q_ref[rows, lanes]
                acc = None
                for h, k, v in ((h0, k_lo, v_lo), (h0 + 1, k_hi, v_hi)):
                    s = lax.dot_general(q_pair, k, (((1,), (1,)), ((), ())),
                                        preferred_element_type=F32)
                    s = jnp.where(from_prev, s[:, :blk], s[:, blk:])
                    if b == 0:
                        s = jnp.where(missing_prev, MASKED, s)
                    sink = sinks_ref[h] * LOG2_E
                    m = jnp.maximum(jnp.max(s, axis=-1, keepdims=True), sink)
                    p = jnp.exp2(s - m)
                    den = jnp.sum(p, axis=-1, keepdims=True) + jnp.exp2(sink - m)
                    p = jnp.concatenate([jnp.where(from_prev, p, 0.0),
                                         jnp.where(from_prev, 0.0, p)], axis=1)
                    o = _dot(p.astype(BF16), v) * (1.0 / den)
                    acc = o if acc is None else acc + o
                o_ref[rows, lanes] = acc.astype(BF16)


def _attn_prompt(sinks, q, kv, *, batch, seq, tq, cast=()):
    m, qw = q.shape
    kvw = kv.shape[1]
    blk = WINDOW
    nt = seq // tq
    per_tile = tq // blk
    grid = (batch, nt)
    cast_in, cast_out, cast_shapes = _cast_specs(cast, grid)
    return pl.pallas_call(
        functools.partial(_attn_prompt_body, n_cast=len(cast)),
        grid_spec=pltpu.PrefetchScalarGridSpec(
            num_scalar_prefetch=1,
            grid=grid,
            in_specs=[
                pl.BlockSpec((tq, qw), lambda b, n, s: (b * nt + n, 0)),
                pl.BlockSpec((blk, kvw), lambda b, n, s: (
                    (b * nt + n) * per_tile - jnp.minimum(n, 1), 0)),
                pl.BlockSpec((tq, kvw), lambda b, n, s: (b * nt + n, 0)),
            ] + cast_in,
            out_specs=[pl.BlockSpec((tq, qw), lambda b, n, s: (b * nt + n, 0))] + cast_out,
        ),
        out_shape=[jax.ShapeDtypeStruct((m, qw), BF16)] + cast_shapes,
        compiler_params=_params(2, 56),
        name="attn_prompt",
    )(sinks, q, kv, kv, *[w for w, _ in cast])


def _attn_sample_body(sinks_ref, q_ref, kv_ref, ck_ref, cv_ref, o_ref, knew_ref, vnew_ref, *,
                      seq_len):
    nseq, win, kw = ck_ref.shape
    rows = GROUP * seq_len
    q = q_ref[...].astype(F32)
    kv_new = kv_ref[...]
    for new_ref, cache_ref, fresh in ((knew_ref, ck_ref, kv_new[:, :kw]),
                                      (vnew_ref, cv_ref, kv_new[:, kw:])):
        new_ref[:, :win - seq_len, :] = cache_ref[:, seq_len:, :]
        new_ref[:, win - seq_len:, :] = fresh.reshape(nseq, seq_len, kw)
    pad = jnp.zeros((nseq, win - seq_len, HEAD_DIM), F32)
    step = lax.broadcasted_iota(jnp.int32, (1, rows, 2 * win), 1) % seq_len
    c = lax.broadcasted_iota(jnp.int32, (1, rows, 2 * win), 2)
    visible = ((c < win) & (c > step + (win - WINDOW))) | ((c >= win) & (c - win <= step))
    pieces = []
    for g in range(N_KV_HEADS):
        lo, hi = g * HEAD_DIM, (g + 1) * HEAD_DIM
        heads = range(g * GROUP, (g + 1) * GROUP)
        k_new = kv_new[:, lo:hi].reshape(nseq, seq_len, HEAD_DIM)
        v_new = kv_new[:, kw + lo:kw + hi].reshape(nseq, seq_len, HEAD_DIM)
        k = jnp.concatenate([ck_ref[:, :, lo:hi], k_new, pad], axis=1).astype(BF16)
        v = jnp.concatenate([cv_ref[:, :, lo:hi], v_new, pad], axis=1).astype(BF16)
        qg = jnp.concatenate(
            [q[:, h * HEAD_DIM:(h + 1) * HEAD_DIM].reshape(nseq, seq_len, HEAD_DIM)
             for h in heads], axis=1).astype(BF16)
        s = jnp.einsum("bqd,bkd->bqk", qg, k, preferred_element_type=F32)
        s = jnp.where(visible, s, MASKED)
        sink_col = jnp.concatenate(
            [jnp.full((1, seq_len, 1), sinks_ref[h] * LOG2_E, F32) for h in heads], axis=1)
        m = jnp.maximum(jnp.max(s, axis=-1, keepdims=True), sink_col)
        p = jnp.exp2(s - m)
        den = jnp.sum(p, axis=-1, keepdims=True) + jnp.exp2(sink_col - m)
        o = jnp.einsum("bqk,bkd->bqd", p.astype(BF16), v, preferred_element_type=F32) / den
        pieces += [o[:, t * seq_len:(t + 1) * seq_len].reshape(nseq * seq_len, HEAD_DIM)
                   for t in range(GROUP)]
    o_ref[...] = jnp.concatenate(pieces, axis=1).astype(BF16)


def _attn_sample(sinks, q, kv, cache_k, cache_v, *, seq_len):
    m, qw = q.shape
    body = functools.partial(_attn_sample_body, seq_len=seq_len)
    full = lambda shape: pl.BlockSpec(shape, lambda i, s: (0,) * len(shape))
    return pl.pallas_call(
        body,
        grid_spec=pltpu.PrefetchScalarGridSpec(
            num_scalar_prefetch=1,
            grid=(1,),
            in_specs=[full(q.shape), full(kv.shape), full(cache_k.shape), full(cache_v.shape)],
            out_specs=[full((m, qw)), full(cache_k.shape), full(cache_v.shape)],
        ),
        out_shape=[jax.ShapeDtypeStruct((m, qw), BF16),
                   jax.ShapeDtypeStruct(cache_k.shape, F32),
                   jax.ShapeDtypeStruct(cache_v.shape, F32)],
        compiler_params=_params(1, 48),
        name="attn_sample",
    )(sinks, q, kv, cache_k, cache_v)


def _rope_tables(pos):
    half = HEAD_DIM // 2
    inv = ROPE_THETA ** (-jnp.arange(half, dtype=F32) / half)
    ang = pos.astype(F32)[:, None] * inv[None, :]
    cos, sin = jnp.cos(ang), jnp.sin(ang)
    reps = LANES // HEAD_DIM
    cos = jnp.tile(jnp.concatenate([cos, cos], axis=1), (1, reps))
    sin_signed = jnp.tile(jnp.concatenate([-sin, sin], axis=1), (1, reps))
    return jnp.concatenate([cos, sin_signed], axis=1)


def kernel(x_prompt, x_sample, state_conv, cache_k_win, cache_v_win, ln_mix, ln_mlp,
           w_conv_in, w_conv, w_conv_out, w_qkv, w_attn_out, q_norm, k_norm, sinks,
           w_up, w_down):
    bp, tp, d = x_prompt.shape
    bs, ts, _ = x_sample.shape
    win = cache_k_win.shape[2]
    kw = N_KV_HEADS * HEAD_DIM
    assert ts == SUBLANES and win == WINDOW and tp % WINDOW == 0

    tm_p = 512
    tm_conv = 1024
    m_s = bs * ts
    n_tiles = bp * tp // tm_p
    rows_s = m_s // n_tiles
    assert rows_s % (2 * SUBLANES) == 0 and rows_s % ts == 0

    xp = x_prompt.reshape(bp * tp, d)
    xs = x_sample.reshape(m_s, d)

    gain = ln_mix[0][None]
    past = state_conv[0]
    zeros = lambda n: jnp.zeros((bs, n, d), F32)
    p1 = jnp.concatenate([past[:, 1:2], zeros(ts - 1)], axis=1).reshape(m_s, d)
    p2 = jnp.concatenate([past, zeros(ts - 2)], axis=1).reshape(m_s, d)
    gate_s, u_s, *w_bcv = _conv_in_sample(xs, gain, w_conv_in[0], w_conv[0], p1, p2,
                                          seq_len=ts, tn=512)
    gate_p, u_tail, wu0, wd0, w_out = _conv_in_prompt(
        xp, gain, w_bcv, w_conv[0], seq=tp, tm=tm_conv, tn=512,
        cast=[(w_up, 0), (w_down, 0), (w_conv_out, 0)])
    x = _proj_residual((xp, xs), (gate_p, gate_s), w_out, n_tiles=n_tiles)
    tiles_per_seq = tp // tm_conv
    new_conv_prompt = u_tail[tiles_per_seq - 1::tiles_per_seq, SUBLANES - (CONV_WIDTH - 1):][None]
    new_conv_sample = u_s.reshape(bs, ts, d)[:, ts - (CONV_WIDTH - 1):][None]

    x, wqkv = _mlp(x, ln_mlp[0][None], wu0, wd0, n_tiles=n_tiles, tf=2048, cast=[(w_qkv, 0)])

    gain = ln_mix[1][None]
    tn = 2 * kw
    head_id = jnp.arange(tn) // HEAD_DIM
    ones_blockdiag = (head_id[:, None] == head_id[None, :]).astype(BF16)
    q_gain = jnp.tile(q_norm[0], tn // HEAD_DIM)[None]
    k_gain = jnp.tile(k_norm[0], kw // HEAD_DIM)[None]
    rope_p = _rope_tables(jnp.arange(tp, dtype=jnp.int32))
    rope_s = _rope_tables(PAST_LEN + jnp.arange(ts, dtype=jnp.int32))
    rope_s = jnp.tile(rope_s, (rows_s // ts, 1))
    q_p, q_s, kv_p, kv_s, kv_tail = _qkv(
        x, gain, wqkv, ones_blockdiag, q_gain, k_gain, rope_p, rope_s,
        n_tiles=n_tiles, prompt_rows=tm_p, seq_tiles=tp // tm_p)
    o_p, wu1, wd1, wo = _attn_prompt(sinks[0], q_p, kv_p, batch=bp, seq=tp, tq=512,
                                     cast=[(w_up, 1), (w_down, 1), (w_attn_out, 0)])
    ck = cache_k_win[0].reshape(bs, win, kw)
    cv = cache_v_win[0].reshape(bs, win, kw)
    o_s, new_k_sample, new_v_sample = _attn_sample(sinks[0], q_s, kv_s, ck, cv, seq_len=ts)
    x = _proj_residual((x,), (o_p, o_s), wo, n_tiles=n_tiles)

    new_k_prompt = kv_tail[:, :, :kw].reshape(1, bp, WINDOW, N_KV_HEADS, HEAD_DIM)
    new_v_prompt = kv_tail[:, :, kw:].reshape(1, bp, WINDOW, N_KV_HEADS, HEAD_DIM)
    new_k_sample = new_k_sample.reshape(1, bs, win, N_KV_HEADS, HEAD_DIM)
    new_v_sample = new_v_sample.reshape(1, bs, win, N_KV_HEADS, HEAD_DIM)

    xp, xs = _mlp(x, ln_mlp[1][None], wu1, wd1, n_tiles=n_tiles, tf=2048,
                  out_rows=(tm_p, rows_s))

    return (xp.reshape(bp, tp, d), xs.reshape(bs, ts, d), new_conv_prompt, new_conv_sample,
            new_k_prompt, new_v_prompt, new_k_sample, new_v_sample)
```

```python
import functools
import math

import jax
import jax.numpy as jnp
from jax import lax
from jax.experimental import pallas as pl
from jax.experimental.pallas import tpu as pltpu

F32 = jnp.float32
BF16 = jnp.bfloat16

HEAD_DIM = 64
N_HEADS = 32
N_KV_HEADS = 4
GROUP = N_HEADS // N_KV_HEADS
WINDOW = 128
PAST_LEN = 16384
ROPE_THETA = 10000.0
EPS = 1e-6
CONV_WIDTH = 3

SUBLANES = 8
LANES = 128
MIB = 1024 * 1024
QKV_ROW_CHUNK = 256
CONV_ROW_CHUNK = 256
MLP_NORM_ROW_CHUNK = 256
MLP_FF_CHUNK = 1024


def _params(n_grid_dims, vmem_mib):
    return pltpu.CompilerParams(
        dimension_semantics=("arbitrary",) * n_grid_dims,
        vmem_limit_bytes=vmem_mib * MIB,
    )


def _rms_norm(x, gain):
    return x * lax.rsqrt(jnp.mean(x * x, axis=-1, keepdims=True) + EPS) * gain


def _dot(a, b):
    return jnp.dot(a, b, preferred_element_type=F32)


def _cast_specs(weights, grid):
    n_steps = 1
    for g in grid:
        n_steps *= g

    def step_of(*ids):
        step = ids[0]
        for g, idx in zip(grid[1:], ids[1:len(grid)]):
            step = step * g + idx
        return step

    in_specs, out_specs, out_shapes = [], [], []
    for w, layer in weights:
        rows, cols = w.shape[1:]
        assert rows % (n_steps * 2 * SUBLANES) == 0, (w.shape, n_steps)
        rb = rows // n_steps
        in_specs.append(pl.BlockSpec((None, rb, cols),
                                     lambda *ids, layer=layer: (layer, step_of(*ids), 0)))
        out_specs.append(pl.BlockSpec((rb, cols), lambda *ids: (step_of(*ids), 0)))
        out_shapes.append(jax.ShapeDtypeStruct((rows, cols), BF16))
    return in_specs, out_specs, out_shapes


def _cast_blocks(src_refs, dst_refs):
    for src, dst in zip(src_refs, dst_refs):
        dst[...] = src[...].astype(BF16)


class _RowParts:
    def __init__(self, refs):
        self.refs = tuple(refs)
        self.rows = sum(r.shape[0] for r in self.refs)

    def _pieces(self, start, size):
        offset = 0
        for ref in self.refs:
            lo, hi = max(start, offset), min(start + size, offset + ref.shape[0])
            if lo < hi:
                yield ref, lo - offset, lo - start, hi - lo
            offset += ref.shape[0]

    def load(self, start=0, size=None):
        size = self.rows - start if size is None else size
        pieces = [ref[at:at + n, :] for ref, at, _, n in self._pieces(start, size)]
        return pieces[0] if len(pieces) == 1 else jnp.concatenate(pieces, axis=0)

    def store(self, start, value, cols=slice(None), accumulate=False):
        for ref, at, src, n in self._pieces(start, value.shape[0]):
            if accumulate:
                ref[at:at + n, cols] += value[src:src + n]
            else:
                ref[at:at + n, cols] = value[src:src + n]


def _row_chunks(rows, chunk):
    n = max(rows // chunk, 1)
    return [(c * chunk, chunk if c < n - 1 else rows - c * chunk) for c in range(n)]


def _mlp_body(x_ref, g_ref, wu_ref, wd_ref, *rest, n_cast, n_out, ff_chunk):
    cast_src, out_refs, cast_dst, (h_ref,) = (
        rest[:n_cast], rest[n_cast:n_cast + n_out], rest[n_cast + n_out:2 * n_cast + n_out],
        rest[2 * n_cast + n_out:])
    out = _RowParts(out_refs)
    tm = x_ref.shape[0]
    tf = wu_ref.shape[1]

    def add_ffn(start, h, base):
        for t in range(tf // ff_chunk):
            cols = pl.ds(t * ff_chunk, ff_chunk)
            a = jnp.maximum(_dot(h, wu_ref[:, cols]), 0.0)
            part = _dot((a * a).astype(BF16), wd_ref[cols, :])
            if base is None:
                out.store(start, part, accumulate=True)
            else:
                out.store(start, base + part)
                base = None

    @pl.when(pl.program_id(1) == 0)
    def _():
        _cast_blocks(cast_src, cast_dst)
        for start, size in _row_chunks(tm, MLP_NORM_ROW_CHUNK):
            rows = pl.ds(start, size)
            x = x_ref[rows, :]
            h = _rms_norm(x, g_ref[...]).astype(BF16)
            h_ref[rows, :] = h
            add_ffn(start, h, x)

    @pl.when(pl.program_id(1) != 0)
    def _():
        _cast_blocks(cast_src, cast_dst)
        add_ffn(0, h_ref[...], None)


def _mlp(x, gain, w_up, w_down, *, n_tiles, tf, out_rows=None, cast=()):
    m, d = x.shape
    tm = m // n_tiles
    out_rows = (tm,) if out_rows is None else out_rows
    assert sum(out_rows) == tm
    ff = w_up.shape[1]
    grid = (n_tiles, ff // tf)
    cast_in, cast_out, cast_shapes = _cast_specs(cast, grid)
    body = functools.partial(_mlp_body, n_cast=len(cast), n_out=len(out_rows),
                             ff_chunk=min(tf, MLP_FF_CHUNK))
    return pl.pallas_call(
        body,
        grid=grid,
        in_specs=[
            pl.BlockSpec((tm, d), lambda i, f: (i, 0)),
            pl.BlockSpec((1, d), lambda i, f: (0, 0)),
            pl.BlockSpec((d, tf), lambda i, f: (0, f)),
            pl.BlockSpec((tf, d), lambda i, f: (f, 0)),
        ] + cast_in,
        out_specs=[pl.BlockSpec((r, d), lambda i, f: (i, 0)) for r in out_rows] + cast_out,
        out_shape=[jax.ShapeDtypeStruct((r * n_tiles, d), F32) for r in out_rows] + cast_shapes,
        scratch_shapes=[pltpu.VMEM((tm, d), BF16)],
        compiler_params=_params(2, 58),
        name="mlp",
    )(x, gain, w_up, w_down, *[w for w, _ in cast])


def _proj_body(*refs, n_x, n_a):
    x, a = _RowParts(refs[:n_x]), _RowParts(refs[n_x:n_x + n_a])
    w_ref, o_ref = refs[n_x + n_a:]
    o_ref[...] = x.load() + _dot(a.load(), w_ref[...])


def _proj_residual(x_parts, a_parts, w, *, n_tiles):
    d = w.shape[1]
    tm = sum(p.shape[0] for p in x_parts) // n_tiles
    assert tm == sum(p.shape[0] for p in a_parts) // n_tiles
    tile_spec = lambda p: pl.BlockSpec((p.shape[0] // n_tiles, p.shape[1]), lambda i: (i, 0))
    return pl.pallas_call(
        functools.partial(_proj_body, n_x=len(x_parts), n_a=len(a_parts)),
        grid=(n_tiles,),
        in_specs=[tile_spec(p) for p in (*x_parts, *a_parts)] + [
            pl.BlockSpec(w.shape, lambda i: (0, 0), pipeline_mode=pl.Buffered(1))],
        out_specs=pl.BlockSpec((tm, d), lambda i: (i, 0)),
        out_shape=jax.ShapeDtypeStruct((tm * n_tiles, d), F32),
        compiler_params=_params(1, 48),
        name="proj_residual",
    )(*x_parts, *a_parts, w)


def _conv_taps(u, u1, u2, w):
    return w[0:1, :] * u2 + w[1:2, :] * u1 + w[2:3, :] * u


def _conv_in_prompt_body(x_ref, g_ref, wb_ref, wc_ref, wv_ref, wconv_ref, *rest,
                         tiles_per_seq, n_cast):
    cast_src, (gate_ref, utail_ref), cast_dst, (h_ref, carry_ref) = (
        rest[:n_cast], rest[n_cast:n_cast + 2], rest[n_cast + 2:2 * n_cast + 2],
        rest[2 * n_cast + 2:])
    i = pl.program_id(0)
    j = pl.program_id(1)

    @pl.when(i % tiles_per_seq == 0)
    def _():
        carry_ref[j] = jnp.zeros(carry_ref.shape[1:], F32)

    tm = h_ref.shape[0]
    chunk = min(tm, CONV_ROW_CHUNK)
    row8 = lax.broadcasted_iota(jnp.int32, (SUBLANES, 1), 0)
    w_conv = wconv_ref[...]

    def run(first_column_tile):
        _cast_blocks(cast_src, cast_dst)
        prev = carry_ref[j]
        for c in range(tm // chunk):
            rows = pl.ds(c * chunk, chunk)
            if first_column_tile:
                h = _rms_norm(x_ref[rows, :], g_ref[...]).astype(BF16)
                h_ref[rows, :] = h
            else:
                h = h_ref[rows, :]
            u = _dot(h, wc_ref[...]) * _dot(h, wv_ref[...])
            u1 = pltpu.roll(u, 1, axis=0)
            u2 = pltpu.roll(u, 2, axis=0)
            top1 = jnp.where(row8 < 1, pltpu.roll(prev, 1, axis=0), u1[0:SUBLANES])
            top2 = jnp.where(row8 < 2, pltpu.roll(prev, 2, axis=0), u2[0:SUBLANES])
            u1 = jnp.concatenate([top1, u1[SUBLANES:]], axis=0)
            u2 = jnp.concatenate([top2, u2[SUBLANES:]], axis=0)
            conv = _conv_taps(u, u1, u2, w_conv)
            gate_ref[rows, :] = (_dot(h, wb_ref[...]) * conv).astype(BF16)
            prev = u[chunk - SUBLANES:]
        carry_ref[j] = prev
        utail_ref[0] = prev

    pl.when(j == 0)(functools.partial(run, True))
    pl.when(j != 0)(functools.partial(run, False))


def _conv_in_sample_body(x_ref, g_ref, wb_ref, wc_ref, wv_ref, wconv_ref, p1_ref, p2_ref,
                         gate_ref, u_ref, wb16_ref, wc16_ref, wv16_ref, h_ref, *, seq_len):
    j = pl.program_id(1)

    @pl.when(j == 0)
    def _():
        h_ref[...] = _rms_norm(x_ref[...], g_ref[...]).astype(BF16)

    _cast_blocks((wb_ref, wc_ref, wv_ref), (wb16_ref, wc16_ref, wv16_ref))
    h = h_ref[...]
    u = _dot(h, wc16_ref[...]) * _dot(h, wv16_ref[...])
    step = lax.broadcasted_iota(jnp.int32, (u.shape[0], 1), 0) % seq_len
    u1 = jnp.where(step < 1, p1_ref[...], pltpu.roll(u, 1, axis=0))
    u2 = jnp.where(step < 2, p2_ref[...], pltpu.roll(u, 2, axis=0))
    conv = _conv_taps(u, u1, u2, wconv_ref[...])
    u_ref[...] = u
    gate_ref[...] = (_dot(h, wb16_ref[...]) * conv).astype(BF16)


def _conv_in_specs(tm, d, tn, *, fused_weight):
    third = d // tn if fused_weight else 0
    return [
        pl.BlockSpec((tm, d), lambda i, j: (i, 0)),
        pl.BlockSpec((1, d), lambda i, j: (0, 0)),
        pl.BlockSpec((d, tn), lambda i, j: (0, j)),
        pl.BlockSpec((d, tn), lambda i, j: (0, third + j)),
        pl.BlockSpec((d, tn), lambda i, j: (0, 2 * third + j)),
        pl.BlockSpec((CONV_WIDTH, tn), lambda i, j: (0, j)),
    ]


def _conv_in_prompt(x, gain, w_bcv, w_conv, *, seq, tm, tn, cast=()):
    m, d = x.shape
    grid = (m // tm, d // tn)
    cast_in, cast_out, cast_shapes = _cast_specs(cast, grid)
    body = functools.partial(_conv_in_prompt_body, tiles_per_seq=seq // tm, n_cast=len(cast))
    return pl.pallas_call(
        body,
        grid=grid,
        in_specs=_conv_in_specs(tm, d, tn, fused_weight=False) + cast_in,
        out_specs=[
            pl.BlockSpec((tm, tn), lambda i, j: (i, j)),
            pl.BlockSpec((1, SUBLANES, tn), lambda i, j: (i, 0, j)),
        ] + cast_out,
        out_shape=[
            jax.ShapeDtypeStruct((m, d), BF16),
            jax.ShapeDtypeStruct((m // tm, SUBLANES, d), F32),
        ] + cast_shapes,
        scratch_shapes=[
            pltpu.VMEM((tm, d), BF16),
            pltpu.VMEM((d // tn, SUBLANES, tn), F32),
        ],
        compiler_params=_params(2, 56),
        name="conv_in_prompt",
    )(x, gain, *w_bcv, w_conv, *[w for w, _ in cast])


def _conv_in_sample(x, gain, w_in, w_conv, p1, p2, *, seq_len, tn):
    m, d = x.shape
    body = functools.partial(_conv_in_sample_body, seq_len=seq_len)
    column_tile = lambda rows: pl.BlockSpec((rows, tn), lambda i, j: (0, j))
    return pl.pallas_call(
        body,
        grid=(1, d // tn),
        in_specs=_conv_in_specs(m, d, tn, fused_weight=True) + [column_tile(m)] * 2,
        out_specs=[column_tile(m)] * 2 + [column_tile(d)] * 3,
        out_shape=[
            jax.ShapeDtypeStruct((m, d), BF16),
            jax.ShapeDtypeStruct((m, d), F32),
        ] + [jax.ShapeDtypeStruct((d, d), BF16)] * 3,
        scratch_shapes=[pltpu.VMEM((m, d), BF16)],
        compiler_params=_params(2, 48),
        name="conv_in_sample",
    )(x, gain, w_in, w_in, w_in, w_conv, p1, p2)


def _head_norm_rope(z, gain, ones_blockdiag, cos, sin_signed, *, split_ssq):
    w = z.shape[1]
    zz = z * z
    hi = zz.astype(BF16)
    ssq = _dot(hi, ones_blockdiag)
    if split_ssq:
        ssq += _dot((zz - hi.astype(F32)).astype(BF16), ones_blockdiag)
    zn = z * lax.rsqrt(ssq * (1.0 / HEAD_DIM) + EPS) * gain
    reps = w // cos.shape[1]
    cos = jnp.concatenate([cos] * reps, axis=1)
    sin_signed = jnp.concatenate([sin_signed] * reps, axis=1)
    lane = lax.broadcasted_iota(jnp.int32, (1, w), 1)
    first_half = (lane % HEAD_DIM) < (HEAD_DIM // 2)
    partner = jnp.where(first_half,
                        pltpu.roll(zn, w - HEAD_DIM // 2, axis=1),
                        pltpu.roll(zn, HEAD_DIM // 2, axis=1))
    return zn * cos + partner * sin_signed


def _qkv_body(x_ref, g_ref, w_ref, ones_ref, qgain_ref, kgain_ref,
              rope_p_ref, rope_s_ref,
              qp_ref, qs_ref, kvp_ref, kvs_ref, kv_tail_ref, *, seq_tiles):
    q_out, kv_out = _RowParts((qp_ref, qs_ref)), _RowParts((kvp_ref, kvs_ref))
    rope_tile = _RowParts((rope_p_ref, rope_s_ref))
    tm = x_ref.shape[0]
    qw = qp_ref.shape[1]
    tn = ones_ref.shape[0]
    kw = N_KV_HEADS * HEAD_DIM
    for start, size in _row_chunks(tm, QKV_ROW_CHUNK):
        h = _rms_norm(x_ref[pl.ds(start, size), :], g_ref[...]).astype(BF16)
        rope = rope_tile.load(start, size)
        cos, sin_signed = rope[:, :LANES], rope[:, LANES:]
        for t in range(qw // tn):
            cols = pl.ds(t * tn, tn)
            z = _dot(h, w_ref[:, cols])
            q = _head_norm_rope(z, qgain_ref[...], ones_ref[...], cos, sin_signed,
                                split_ssq=False)
            q_out.store(start, q.astype(BF16), cols=cols)
        z = _dot(h, w_ref[:, pl.ds(qw, 2 * kw)])
        k = _head_norm_rope(z[:, :kw], kgain_ref[...], ones_ref[0:kw, 0:kw], cos, sin_signed,
                            split_ssq=True)
        kv_out.store(start, jnp.concatenate([k, z[:, kw:]], axis=1))

    @pl.when(pl.program_id(0) % seq_tiles == seq_tiles - 1)
    def _():
        kv_tail_ref[...] = kvp_ref[kvp_ref.shape[0] - WINDOW:, :]


def _qkv(x, gain, w_qkv, ones_blockdiag, q_gain, k_gain, rope_prompt, rope_sample, *,
         n_tiles, prompt_rows, seq_tiles):
    m, d = x.shape
    tm = m // n_tiles
    sample_rows = tm - prompt_rows
    qw = N_HEADS * HEAD_DIM
    kvw = 2 * N_KV_HEADS * HEAD_DIM
    tn = ones_blockdiag.shape[0]
    const = lambda shape: pl.BlockSpec(shape, lambda i: (0, 0), pipeline_mode=pl.Buffered(1))
    rope_p = pl.BlockSpec((prompt_rows, 2 * LANES), lambda i: (i % seq_tiles, 0))
    split = lambda width: [pl.BlockSpec((prompt_rows, width), lambda i: (i, 0)),
                           pl.BlockSpec((sample_rows, width), lambda i: (i, 0))]
    split_shape = lambda width, dtype: [
        jax.ShapeDtypeStruct((prompt_rows * n_tiles, width), dtype),
        jax.ShapeDtypeStruct((sample_rows * n_tiles, width), dtype)]
    return pl.pallas_call(
        functools.partial(_qkv_body, seq_tiles=seq_tiles),
        grid=(n_tiles,),
        in_specs=[
            pl.BlockSpec((tm, d), lambda i: (i, 0)),
            const((1, d)),
            const(w_qkv.shape),
            const((tn, tn)),
            const((1, tn)),
            const((1, kvw // 2)),
            rope_p,
            const((sample_rows, 2 * LANES)),
        ],
        out_specs=split(qw) + split(kvw) + [
            pl.BlockSpec((None, WINDOW, kvw), lambda i: (i // seq_tiles, 0, 0))],
        out_shape=split_shape(qw, BF16) + split_shape(kvw, F32) + [
            jax.ShapeDtypeStruct((n_tiles // seq_tiles, WINDOW, kvw), F32)],
        compiler_params=_params(1, 48),
        name="qkv",
    )(x, gain, w_qkv, ones_blockdiag, q_gain, k_gain, rope_prompt, rope_sample)


MASKED = -1e30
LOG2_E = math.log2(math.e)
Q_SCALE = HEAD_DIM ** -0.5 * LOG2_E


def _attn_prompt_body(sinks_ref, q_ref, kv_prev_ref, kv_cur_ref, *rest, n_cast):
    cast_src, (o_ref,), cast_dst = rest[:n_cast], rest[n_cast:n_cast + 1], rest[n_cast + 1:]
    _cast_blocks(cast_src, cast_dst)
    n = pl.program_id(1)
    blk = WINDOW
    n_blocks = q_ref.shape[0] // blk
    kw = N_KV_HEADS * HEAD_DIM
    r = lax.broadcasted_iota(jnp.int32, (blk, blk), 0)
    c = lax.broadcasted_iota(jnp.int32, (blk, blk), 1)
    from_prev = c > r
    missing_prev = from_prev & (n == 0)
    lower = lax.broadcasted_iota(jnp.int32, (1, LANES), 1) < HEAD_DIM
    heads_per_col = LANES // HEAD_DIM

    def padded_operands(kv):
        per_head = []
        for g in range(N_KV_HEADS):
            col, half = divmod(g, heads_per_col)
            kcol = kv[:, col * LANES:(col + 1) * LANES]
            vcol = kv[:, kw + col * LANES:kw + (col + 1) * LANES]
            kswap = pltpu.roll(kcol, HEAD_DIM, axis=1)
            vswap = pltpu.roll(vcol, HEAD_DIM, axis=1)
            in_lower = (kcol, vcol) if half == 0 else (kswap, vswap)
            in_upper = (kswap, vswap) if half == 0 else (kcol, vcol)
            per_head.append(tuple(jnp.where(lower, t, 0.0).astype(BF16) for t in in_lower)
                            + tuple(jnp.where(lower, 0.0, t).astype(BF16) for t in in_upper))
        return per_head

    key_blocks = [padded_operands(kv_prev_ref[...])]
    for b in range(n_blocks):
        rows = pl.ds(b * blk, blk)
        key_blocks.append(padded_operands(kv_cur_ref[rows, :]))
        for g in range(N_KV_HEADS):
            k_lo, v_lo, k_hi, v_hi = (jnp.concatenate(prev_and_own, axis=0) for prev_and_own
                                      in zip(key_blocks[b][g], key_blocks[b + 1][g]))
            for pair in range(GROUP // heads_per_col):
                h0 = g * GROUP + pair * heads_per_col
                lanes = pl.ds(h0 * HEAD_DIM, LANES)
                q_pair = q_ref[rows, lanes]
                acc = None
                for h, k, v in ((h0, k_lo, v_lo), (h0 + 1, k_hi, v_hi)):
                    s = lax.dot_general(q_pair, k, (((1,), (1,)), ((), ())),
                                        preferred_element_type=F32)
                    s = jnp.where(from_prev, s[:, :blk], s[:, blk:])
                    if b == 0:
                        s = jnp.where(missing_prev, MASKED, s)
                    sink = sinks_ref[h] * LOG2_E
                    m = jnp.maximum(jnp.max(s, axis=-1, keepdims=True), sink)
                    p = jnp.exp2(s - m)
                    den = jnp.sum(p, axis=-1, keepdims=True) + jnp.exp2(sink - m)
                    p = jnp.concatenate([jnp.where(from_prev, p, 0.0),
                                         jnp.where(from_prev, 0.0, p)], axis=1)
                    o = _dot(p.astype(BF16), v) * (1.0 / den)
                    acc = o if acc is None else acc + o
                o_ref[rows, lanes] = acc.astype(BF16)


def _attn_prompt(sinks, q, kv, *, batch, seq, tq, cast=()):
    m, qw = q.shape
    kvw = kv.shape[1]
    blk = WINDOW
    nt = seq // tq
    per_tile = tq // blk
    grid = (batch, nt)
    cast_in, cast_out, cast_shapes = _cast_specs(cast, grid)
    return pl.pallas_call(
        functools.partial(_attn_prompt_body, n_cast=len(cast)),
        grid_spec=pltpu.PrefetchScalarGridSpec(
            num_scalar_prefetch=1,
            grid=grid,
            in_specs=[
                pl.BlockSpec((tq, qw), lambda b, n, s: (b * nt + n, 0)),
                pl.BlockSpec((blk, kvw), lambda b, n, s: (
                    (b * nt + n) * per_tile - jnp.minimum(n, 1), 0)),
                pl.BlockSpec((tq, kvw), lambda b, n, s: (b * nt + n, 0)),
            ] + cast_in,
            out_specs=[pl.BlockSpec((tq, qw), lambda b, n, s: (b * nt + n, 0))] + cast_out,
        ),
        out_shape=[jax.ShapeDtypeStruct((m, qw), BF16)] + cast_shapes,
        compiler_params=_params(2, 56),
        name="attn_prompt",
    )(sinks, q, kv, kv, *[w for w, _ in cast])


def _attn_sample_body(sinks_ref, q_ref, kv_ref, ck_ref, cv_ref, o_ref, knew_ref, vnew_ref, *,
                      seq_len):
    nseq, win, kw = ck_ref.shape
    rows = GROUP * seq_len
    q = q_ref[...].astype(F32)
    kv_new = kv_ref[...]
    for new_ref, cache_ref, fresh in ((knew_ref, ck_ref, kv_new[:, :kw]),
                                      (vnew_ref, cv_ref, kv_new[:, kw:])):
        new_ref[:, :win - seq_len, :] = cache_ref[:, seq_len:, :]
        new_ref[:, win - seq_len:, :] = fresh.reshape(nseq, seq_len, kw)
    pad = jnp.zeros((nseq, win - seq_len, HEAD_DIM), F32)
    step = lax.broadcasted_iota(jnp.int32, (1, rows, 2 * win), 1) % seq_len
    c = lax.broadcasted_iota(jnp.int32, (1, rows, 2 * win), 2)
    visible = ((c < win) & (c > step + (win - WINDOW))) | ((c >= win) & (c - win <= step))
    pieces = []
    for g in range(N_KV_HEADS):
        lo, hi = g * HEAD_DIM, (g + 1) * HEAD_DIM
        heads = range(g * GROUP, (g + 1) * GROUP)
        k_new = kv_new[:, lo:hi].reshape(nseq, seq_len, HEAD_DIM)
        v_new = kv_new[:, kw + lo:kw + hi].reshape(nseq, seq_len, HEAD_DIM)
        k = jnp.concatenate([ck_ref[:, :, lo:hi], k_new, pad], axis=1).astype(BF16)
        v = jnp.concatenate([cv_ref[:, :, lo:hi], v_new, pad], axis=1).astype(BF16)
        qg = jnp.concatenate(
            [q[:, h * HEAD_DIM:(h + 1) * HEAD_DIM].reshape(nseq, seq_len, HEAD_DIM)
             for h in heads], axis=1).astype(BF16)
        s = jnp.einsum("bqd,bkd->bqk", qg, k, preferred_element_type=F32)
        s = jnp.where(visible, s, MASKED)
        sink_col = jnp.concatenate(
            [jnp.full((1, seq_len, 1), sinks_ref[h] * LOG2_E, F32) for h in heads], axis=1)
        m = jnp.maximum(jnp.max(s, axis=-1, keepdims=True), sink_col)
        p = jnp.exp2(s - m)
        den = jnp.sum(p, axis=-1, keepdims=True) + jnp.exp2(sink_col - m)
        o = jnp.einsum("bqk,bkd->bqd", p.astype(BF16), v, preferred_element_type=F32) / den
        pieces += [o[:, t * seq_len:(t + 1) * seq_len].reshape(nseq * seq_len, HEAD_DIM)
                   for t in range(GROUP)]
    o_ref[...] = jnp.concatenate(pieces, axis=1).astype(BF16)


def _attn_sample(sinks, q, kv, cache_k, cache_v, *, seq_len):
    m, qw = q.shape
    body = functools.partial(_attn_sample_body, seq_len=seq_len)
    full = lambda shape: pl.BlockSpec(shape, lambda i, s: (0,) * len(shape))
    return pl.pallas_call(
        body,
        grid_spec=pltpu.PrefetchScalarGridSpec(
            num_scalar_prefetch=1,
            grid=(1,),
            in_specs=[full(q.shape), full(kv.shape), full(cache_k.shape), full(cache_v.shape)],
            out_specs=[full((m, qw)), full(cache_k.shape), full(cache_v.shape)],
        ),
        out_shape=[jax.ShapeDtypeStruct((m, qw), BF16),
                   jax.ShapeDtypeStruct(cache_k.shape, F32),
                   jax.ShapeDtypeStruct(cache_v.shape, F32)],
        compiler_params=_params(1, 48),
        name="attn_sample",
    )(sinks, q, kv, cache_k, cache_v)


def _rope_tables(pos):
    half = HEAD_DIM // 2
    inv = ROPE_THETA ** (-jnp.arange(half, dtype=F32) / half)
    ang = pos.astype(F32)[:, None] * inv[None, :]
    cos, sin = jnp.cos(ang), jnp.sin(ang)
    reps = LANES // HEAD_DIM
    cos = jnp.tile(jnp.concatenate([cos, cos], axis=1), (1, reps))
    sin_signed = jnp.tile(jnp.concatenate([-sin, sin], axis=1), (1, reps))
    return jnp.concatenate([cos, sin_signed], axis=1)


def kernel(x_prompt, x_sample, state_conv, cache_k_win, cache_v_win, ln_mix, ln_mlp,
           w_conv_in, w_conv, w_conv_out, w_qkv, w_attn_out, q_norm, k_norm, sinks,
           w_up, w_down):
    bp, tp, d = x_prompt.shape
    bs, ts, _ = x_sample.shape
    win = cache_k_win.shape[2]
    kw = N_KV_HEADS * HEAD_DIM
    assert ts == SUBLANES and win == WINDOW and tp % WINDOW == 0

    tm_p = 512
    tm_conv = 1024
    m_s = bs * ts
    n_tiles = bp * tp // tm_p
    rows_s = m_s // n_tiles
    assert rows_s % (2 * SUBLANES) == 0 and rows_s % ts == 0

    xp = x_prompt.reshape(bp * tp, d)
    xs = x_sample.reshape(m_s, d)

    gain = ln_mix[0][None]
    past = state_conv[0]
    zeros = lambda n: jnp.zeros((bs, n, d), F32)
    p1 = jnp.concatenate([past[:, 1:2], zeros(ts - 1)], axis=1).reshape(m_s, d)
    p2 = jnp.concatenate([past, zeros(ts - 2)], axis=1).reshape(m_s, d)
    gate_s, u_s, *w_bcv = _conv_in_sample(xs, gain, w_conv_in[0], w_conv[0], p1, p2,
                                          seq_len=ts, tn=512)
    gate_p, u_tail, wu0, wd0, w_out = _conv_in_prompt(
        xp, gain, w_bcv, w_conv[0], seq=tp, tm=tm_conv, tn=512,
        cast=[(w_up, 0), (w_down, 0), (w_conv_out, 0)])
    x = _proj_residual((xp, xs), (gate_p, gate_s), w_out, n_tiles=n_tiles)
    tiles_per_seq = tp // tm_conv
    new_conv_prompt = u_tail[tiles_per_seq - 1::tiles_per_seq, SUBLANES - (CONV_WIDTH - 1):][None]
    new_conv_sample = u_s.reshape(bs, ts, d)[:, ts - (CONV_WIDTH - 1):][None]

    x, wqkv = _mlp(x, ln_mlp[0][None], wu0, wd0, n_tiles=n_tiles, tf=2048, cast=[(w_qkv, 0)])

    gain = ln_mix[1][None]
    tn = 2 * kw
    head_id = jnp.arange(tn) // HEAD_DIM
    ones_blockdiag = (head_id[:, None] == head_id[None, :]).astype(BF16)
    q_gain = jnp.tile(q_norm[0] * Q_SCALE, tn // HEAD_DIM)[None]
    k_gain = jnp.tile(k_norm[0], kw // HEAD_DIM)[None]
    rope_p = _rope_tables(jnp.arange(tp, dtype=jnp.int32))
    rope_s = _rope_tables(PAST_LEN + jnp.arange(ts, dtype=jnp.int32))
    rope_s = jnp.tile(rope_s, (rows_s // ts, 1))
    q_p, q_s, kv_p, kv_s, kv_tail = _qkv(
        x, gain, wqkv, ones_blockdiag, q_gain, k_gain, rope_p, rope_s,
        n_tiles=n_tiles, prompt_rows=tm_p, seq_tiles=tp // tm_p)
    o_p, wu1, wd1, wo = _attn_prompt(sinks[0], q_p, kv_p, batch=bp, seq=tp, tq=512,
                                     cast=[(w_up, 1), (w_down, 1), (w_attn_out, 0)])
    ck = cache_k_win[0].reshape(bs, win, kw)
    cv = cache_v_win[0].reshape(bs, win, kw)
    o_s, new_k_sample, new_v_sample = _attn_sample(sinks[0], q_s, kv_s, ck, cv, seq_len=ts)
    x = _proj_residual((x,), (o_p, o_s), wo, n_tiles=n_tiles)

    new_k_prompt = kv_tail[:, :, :kw].reshape(1, bp, WINDOW, N_KV_HEADS, HEAD_DIM)
    new_v_prompt = kv_tail[:, :, kw:].reshape(1, bp, WINDOW, N_KV_HEADS, HEAD_DIM)
    new_k_sample = new_k_sample.reshape(1, bs, win, N_KV_HEADS, HEAD_DIM)
    new_v_sample = new_v_sample.reshape(1, bs, win, N_KV_HEADS, HEAD_DIM)

    xp, xs = _mlp(x, ln_mlp[1][None], wu1, wd1, n_tiles=n_tiles, tf=2048,
                  out_rows=(tm_p, rows_s))

    return (xp.reshape(bp, tp, d), xs.reshape(bs, ts, d), new_conv_prompt, new_conv_sample,
            new_k_prompt, new_v_prompt, new_k_sample, new_v_sample)
```

```python
import functools
import math

import jax
import jax.numpy as jnp
from jax import lax
from jax.experimental import pallas as pl
from jax.experimental.pallas import tpu as pltpu

F32 = jnp.float32
BF16 = jnp.bfloat16

HEAD_DIM = 64
N_HEADS = 32
N_KV_HEADS = 4
GROUP = N_HEADS // N_KV_HEADS
WINDOW = 128
PAST_LEN = 16384
ROPE_THETA = 10000.0
EPS = 1e-6
CONV_WIDTH = 3

SUBLANES = 8
LANES = 128
MIB = 1024 * 1024
QKV_ROW_CHUNK = 256
CONV_ROW_CHUNK = 256
MLP_NORM_ROW_CHUNK = 256
MLP_FF_CHUNK = 1024


def _params(n_grid_dims, vmem_mib):
    return pltpu.CompilerParams(
        dimension_semantics=("arbitrary",) * n_grid_dims,
        vmem_limit_bytes=vmem_mib * MIB,
    )


def _rms_norm(x, gain):
    return x * lax.rsqrt(jnp.mean(x * x, axis=-1, keepdims=True) + EPS) * gain


def _dot(a, b):
    return jnp.dot(a, b, preferred_element_type=F32)


def _cast_specs(weights, grid):
    n_steps = 1
    for g in grid:
        n_steps *= g

    def step_of(*ids):
        step = ids[0]
        for g, idx in zip(grid[1:], ids[1:len(grid)]):
            step = step * g + idx
        return step

    in_specs, out_specs, out_shapes = [], [], []
    for w, layer in weights:
        rows, cols = w.shape[1:]
        assert rows % (n_steps * 2 * SUBLANES) == 0, (w.shape, n_steps)
        rb = rows // n_steps
        in_specs.append(pl.BlockSpec((None, rb, cols),
                                     lambda *ids, layer=layer: (layer, step_of(*ids), 0)))
        out_specs.append(pl.BlockSpec((rb, cols), lambda *ids: (step_of(*ids), 0)))
        out_shapes.append(jax.ShapeDtypeStruct((rows, cols), BF16))
    return in_specs, out_specs, out_shapes


def _cast_blocks(src_refs, dst_refs):
    for src, dst in zip(src_refs, dst_refs):
        dst[...] = src[...].astype(BF16)


class _RowParts:
    def __init__(self, refs):
        self.refs = tuple(refs)
        self.rows = sum(r.shape[0] for r in self.refs)

    def _pieces(self, start, size):
        offset = 0
        for ref in self.refs:
            lo, hi = max(start, offset), min(start + size, offset + ref.shape[0])
            if lo < hi:
                yield ref, lo - offset, lo - start, hi - lo
            offset += ref.shape[0]

    def load(self, start=0, size=None):
        size = self.rows - start if size is None else size
        pieces = [ref[at:at + n, :] for ref, at, _, n in self._pieces(start, size)]
        return pieces[0] if len(pieces) == 1 else jnp.concatenate(pieces, axis=0)

    def store(self, start, value, cols=slice(None), accumulate=False):
        for ref, at, src, n in self._pieces(start, value.shape[0]):
            if accumulate:
                ref[at:at + n, cols] += value[src:src + n]
            else:
                ref[at:at + n, cols] = value[src:src + n]


def _row_chunks(rows, chunk):
    n = max(rows // chunk, 1)
    return [(c * chunk, chunk if c < n - 1 else rows - c * chunk) for c in range(n)]


def _mlp_body(x_ref, g_ref, wu_ref, wd_ref, *rest, n_cast, n_out, ff_chunk):
    cast_src, out_refs, cast_dst, (h_ref,) = (
        rest[:n_cast], rest[n_cast:n_cast + n_out], rest[n_cast + n_out:2 * n_cast + n_out],
        rest[2 * n_cast + n_out:])
    out = _RowParts(out_refs)
    tm = x_ref.shape[0]
    tf = wu_ref.shape[1]

    def add_ffn(start, h, base):
        for t in range(tf // ff_chunk):
            cols = pl.ds(t * ff_chunk, ff_chunk)
            a = jnp.maximum(_dot(h, wu_ref[:, cols]), 0.0)
            part = _dot((a * a).astype(BF16), wd_ref[cols, :])
            if base is None:
                out.store(start, part, accumulate=True)
            else:
                out.store(start, base + part)
                base = None

    @pl.when(pl.program_id(1) == 0)
    def _():
        _cast_blocks(cast_src, cast_dst)
        for start, size in _row_chunks(tm, MLP_NORM_ROW_CHUNK):
            rows = pl.ds(start, size)
            x = x_ref[rows, :]
            h = _rms_norm(x, g_ref[...]).astype(BF16)
            h_ref[rows, :] = h
            add_ffn(start, h, x)

    @pl.when(pl.program_id(1) != 0)
    def _():
        _cast_blocks(cast_src, cast_dst)
        add_ffn(0, h_ref[...], None)


def _mlp(x, gain, w_up, w_down, *, n_tiles, tf, out_rows=None, cast=()):
    m, d = x.shape
    tm = m // n_tiles
    out_rows = (tm,) if out_rows is None else out_rows
    assert sum(out_rows) == tm
    ff = w_up.shape[1]
    grid = (n_tiles, ff // tf)
    cast_in, cast_out, cast_shapes = _cast_specs(cast, grid)
    body = functools.partial(_mlp_body, n_cast=len(cast), n_out=len(out_rows),
                             ff_chunk=min(tf, MLP_FF_CHUNK))
    return pl.pallas_call(
        body,
        grid=grid,
        in_specs=[
            pl.BlockSpec((tm, d), lambda i, f: (i, 0)),
            pl.BlockSpec((1, d), lambda i, f: (0, 0)),
            pl.BlockSpec((d, tf), lambda i, f: (0, f)),
            pl.BlockSpec((tf, d), lambda i, f: (f, 0)),
        ] + cast_in,
        out_specs=[pl.BlockSpec((r, d), lambda i, f: (i, 0)) for r in out_rows] + cast_out,
        out_shape=[jax.ShapeDtypeStruct((r * n_tiles, d), F32) for r in out_rows] + cast_shapes,
        scratch_shapes=[pltpu.VMEM((tm, d), BF16)],
        compiler_params=_params(2, 58),
        name="mlp",
    )(x, gain, w_up, w_down, *[w for w, _ in cast])


def _proj_body(*refs, n_x, n_a):
    x, a = _RowParts(refs[:n_x]), _RowParts(refs[n_x:n_x + n_a])
    w_ref, o_ref = refs[n_x + n_a:]
    o_ref[...] = x.load() + _dot(a.load(), w_ref[...])


def _proj_residual(x_parts, a_parts, w, *, n_tiles):
    d = w.shape[1]
    tm = sum(p.shape[0] for p in x_parts) // n_tiles
    assert tm == sum(p.shape[0] for p in a_parts) // n_tiles
    tile_spec = lambda p: pl.BlockSpec((p.shape[0] // n_tiles, p.shape[1]), lambda i: (i, 0))
    return pl.pallas_call(
        functools.partial(_proj_body, n_x=len(x_parts), n_a=len(a_parts)),
        grid=(n_tiles,),
        in_specs=[tile_spec(p) for p in (*x_parts, *a_parts)] + [
            pl.BlockSpec(w.shape, lambda i: (0, 0), pipeline_mode=pl.Buffered(1))],
        out_specs=pl.BlockSpec((tm, d), lambda i: (i, 0)),
        out_shape=jax.ShapeDtypeStruct((tm * n_tiles, d), F32),
        compiler_params=_params(1, 48),
        name="proj_residual",
    )(*x_parts, *a_parts, w)


def _conv_taps(u, u1, u2, w):
    return w[0:1, :] * u2 + w[1:2, :] * u1 + w[2:3, :] * u


def _conv_in_prompt_body(x_ref, g_ref, wb_ref, wc_ref, wv_ref, wconv_ref, *rest,
                         tiles_per_seq, n_cast):
    cast_src, (gate_ref, utail_ref), cast_dst, (h_ref, carry_ref) = (
        rest[:n_cast], rest[n_cast:n_cast + 2], rest[n_cast + 2:2 * n_cast + 2],
        rest[2 * n_cast + 2:])
    i = pl.program_id(0)
    j = pl.program_id(1)

    @pl.when(i % tiles_per_seq == 0)
    def _():
        carry_ref[j] = jnp.zeros(carry_ref.shape[1:], F32)

    tm = h_ref.shape[0]
    chunk = min(tm, CONV_ROW_CHUNK)
    row8 = lax.broadcasted_iota(jnp.int32, (SUBLANES, 1), 0)
    w_conv = wconv_ref[...]

    def run(first_column_tile):
        _cast_blocks(cast_src, cast_dst)
        prev = carry_ref[j]
        for c in range(tm // chunk):
            rows = pl.ds(c * chunk, chunk)
            if first_column_tile:
                h = _rms_norm(x_ref[rows, :], g_ref[...]).astype(BF16)
                h_ref[rows, :] = h
            else:
                h = h_ref[rows, :]
            u = _dot(h, wc_ref[...]) * _dot(h, wv_ref[...])
            u1 = pltpu.roll(u, 1, axis=0)
            u2 = pltpu.roll(u, 2, axis=0)
            top1 = jnp.where(row8 < 1, pltpu.roll(prev, 1, axis=0), u1[0:SUBLANES])
            top2 = jnp.where(row8 < 2, pltpu.roll(prev, 2, axis=0), u2[0:SUBLANES])
            u1 = jnp.concatenate([top1, u1[SUBLANES:]], axis=0)
            u2 = jnp.concatenate([top2, u2[SUBLANES:]], axis=0)
            conv = _conv_taps(u, u1, u2, w_conv)
            gate_ref[rows, :] = (_dot(h, wb_ref[...]) * conv).astype(BF16)
            prev = u[chunk - SUBLANES:]
        carry_ref[j] = prev
        utail_ref[0] = prev

    pl.when(j == 0)(functools.partial(run, True))
    pl.when(j != 0)(functools.partial(run, False))


def _conv_in_sample_body(x_ref, g_ref, wb_ref, wc_ref, wv_ref, wconv_ref, p1_ref, p2_ref,
                         gate_ref, u_ref, wb16_ref, wc16_ref, wv16_ref, h_ref, *, seq_len):
    j = pl.program_id(1)

    @pl.when(j == 0)
    def _():
        h_ref[...] = _rms_norm(x_ref[...], g_ref[...]).astype(BF16)

    _cast_blocks((wb_ref, wc_ref, wv_ref), (wb16_ref, wc16_ref, wv16_ref))
    h = h_ref[...]
    u = _dot(h, wc16_ref[...]) * _dot(h, wv16_ref[...])
    step = lax.broadcasted_iota(jnp.int32, (u.shape[0], 1), 0) % seq_len
    u1 = jnp.where(step < 1, p1_ref[...], pltpu.roll(u, 1, axis=0))
    u2 = jnp.where(step < 2, p2_ref[...], pltpu.roll(u, 2, axis=0))
    conv = _conv_taps(u, u1, u2, wconv_ref[...])
    u_ref[...] = u
    gate_ref[...] = (_dot(h, wb16_ref[...]) * conv).astype(BF16)


def _conv_in_specs(tm, d, tn, *, fused_weight):
    third = d // tn if fused_weight else 0
    return [
        pl.BlockSpec((tm, d), lambda i, j: (i, 0)),
        pl.BlockSpec((1, d), lambda i, j: (0, 0)),
        pl.BlockSpec((d, tn), lambda i, j: (0, j)),
        pl.BlockSpec((d, tn), lambda i, j: (0, third + j)),
        pl.BlockSpec((d, tn), lambda i, j: (0, 2 * third + j)),
        pl.BlockSpec((CONV_WIDTH, tn), lambda i, j: (0, j)),
    ]


def _conv_in_prompt(x, gain, w_bcv, w_conv, *, seq, tm, tn, cast=()):
    m, d = x.shape
    grid = (m // tm, d // tn)
    cast_in, cast_out, cast_shapes = _cast_specs(cast, grid)
    body = functools.partial(_conv_in_prompt_body, tiles_per_seq=seq // tm, n_cast=len(cast))
    return pl.pallas_call(
        body,
        grid=grid,
        in_specs=_conv_in_specs(tm, d, tn, fused_weight=False) + cast_in,
        out_specs=[
            pl.BlockSpec((tm, tn), lambda i, j: (i, j)),
            pl.BlockSpec((1, SUBLANES, tn), lambda i, j: (i, 0, j)),
        ] + cast_out,
        out_shape=[
            jax.ShapeDtypeStruct((m, d), BF16),
            jax.ShapeDtypeStruct((m // tm, SUBLANES, d), F32),
        ] + cast_shapes,
        scratch_shapes=[
            pltpu.VMEM((tm, d), BF16),
            pltpu.VMEM((d // tn, SUBLANES, tn), F32),
        ],
        compiler_params=_params(2, 56),
        name="conv_in_prompt",
    )(x, gain, *w_bcv, w_conv, *[w for w, _ in cast])


def _conv_in_sample(x, gain, w_in, w_conv, p1, p2, *, seq_len, tn):
    m, d = x.shape
    body = functools.partial(_conv_in_sample_body, seq_len=seq_len)
    column_tile = lambda rows: pl.BlockSpec((rows, tn), lambda i, j: (0, j))
    return pl.pallas_call(
        body,
        grid=(1, d // tn),
        in_specs=_conv_in_specs(m, d, tn, fused_weight=True) + [column_tile(m)] * 2,
        out_specs=[column_tile(m)] * 2 + [column_tile(d)] * 3,
        out_shape=[
            jax.ShapeDtypeStruct((m, d), BF16),
            jax.ShapeDtypeStruct((m, d), F32),
        ] + [jax.ShapeDtypeStruct((d, d), BF16)] * 3,
        scratch_shapes=[pltpu.VMEM((m, d), BF16)],
        compiler_params=_params(2, 48),
        name="conv_in_sample",
    )(x, gain, w_in, w_in, w_in, w_conv, p1, p2)


def _head_norm_rope(z, gain, ones_blockdiag, cos, sin_signed, *, split_ssq):
    w = z.shape[1]
    zz = z * z
    hi = zz.astype(BF16)
    ssq = _dot(hi, ones_blockdiag)
    if split_ssq:
        ssq += _dot((zz - hi.astype(F32)).astype(BF16), ones_blockdiag)
    zn = z * lax.rsqrt(ssq * (1.0 / HEAD_DIM) + EPS) * gain
    reps = w // cos.shape[1]
    cos = jnp.concatenate([cos] * reps, axis=1)
    sin_signed = jnp.concatenate([sin_signed] * reps, axis=1)
    lane = lax.broadcasted_iota(jnp.int32, (1, w), 1)
    first_half = (lane % HEAD_DIM) < (HEAD_DIM // 2)
    partner = jnp.where(first_half,
                        pltpu.roll(zn, w - HEAD_DIM // 2, axis=1),
                        pltpu.roll(zn, HEAD_DIM // 2, axis=1))
    return zn * cos + partner * sin_signed


def _qkv_body(x_ref, g_ref, w_ref, ones_ref, qgain_ref, kgain_ref,
              rope_p_ref, rope_s_ref,
              qp_ref, qs_ref, kvp_ref, kvs_ref, kv_tail_ref, *, steps_per_seq, tiles_per_step):
    qw = qp_ref.shape[1]
    tn = ones_ref.shape[0]
    kw = N_KV_HEADS * HEAD_DIM

    def tile(ref, t):
        rows = ref.shape[0] // tiles_per_step
        return ref.at[pl.ds(t * rows, rows)]

    for t in range(tiles_per_step):
        x_t = tile(x_ref, t)
        q_out = _RowParts((tile(qp_ref, t), tile(qs_ref, t)))
        kv_out = _RowParts((tile(kvp_ref, t), tile(kvs_ref, t)))
        rope_tile = _RowParts((tile(rope_p_ref, t), rope_s_ref))
        for start, size in _row_chunks(x_t.shape[0], QKV_ROW_CHUNK):
            h = _rms_norm(x_t[pl.ds(start, size), :], g_ref[...]).astype(BF16)
            rope = rope_tile.load(start, size)
            cos, sin_signed = rope[:, :LANES], rope[:, LANES:]
            for c in range(qw // tn):
                cols = pl.ds(c * tn, tn)
                z = _dot(h, w_ref[:, cols])
                q = _head_norm_rope(z, qgain_ref[...], ones_ref[...], cos, sin_signed,
                                    split_ssq=False)
                q_out.store(start, q.astype(BF16), cols=cols)
            z = _dot(h, w_ref[:, pl.ds(qw, 2 * kw)])
            k = _head_norm_rope(z[:, :kw], kgain_ref[...], ones_ref[0:kw, 0:kw], cos,
                                sin_signed, split_ssq=True)
            kv_out.store(start, jnp.concatenate([k, z[:, kw:]], axis=1))

    @pl.when(pl.program_id(0) % steps_per_seq == steps_per_seq - 1)
    def _():
        kv_tail_ref[...] = kvp_ref[kvp_ref.shape[0] - WINDOW:, :]


def _qkv(x, gain, w_qkv, ones_blockdiag, q_gain, k_gain, rope_prompt, rope_sample, *,
         n_tiles, prompt_rows, seq_tiles, tiles_per_step):
    m, d = x.shape
    tm = m // n_tiles
    sample_rows = tm - prompt_rows
    qw = N_HEADS * HEAD_DIM
    kvw = 2 * N_KV_HEADS * HEAD_DIM
    tn = ones_blockdiag.shape[0]
    steps_per_seq = seq_tiles // tiles_per_step
    const = lambda shape: pl.BlockSpec(shape, lambda i: (0, 0), pipeline_mode=pl.Buffered(1))
    step_rows = lambda rows, width: pl.BlockSpec((rows * tiles_per_step, width),
                                                 lambda i: (i, 0))
    split = lambda width: [step_rows(prompt_rows, width), step_rows(sample_rows, width)]
    split_shape = lambda width, dtype: [
        jax.ShapeDtypeStruct((prompt_rows * n_tiles, width), dtype),
        jax.ShapeDtypeStruct((sample_rows * n_tiles, width), dtype)]
    return pl.pallas_call(
        functools.partial(_qkv_body, steps_per_seq=steps_per_seq,
                          tiles_per_step=tiles_per_step),
        grid=(n_tiles // tiles_per_step,),
        in_specs=[
            step_rows(tm, d),
            const((1, d)),
            const(w_qkv.shape),
            const((tn, tn)),
            const((1, tn)),
            const((1, kvw // 2)),
            pl.BlockSpec((prompt_rows * tiles_per_step, 2 * LANES),
                         lambda i: (i % steps_per_seq, 0)),
            const((sample_rows, 2 * LANES)),
        ],
        out_specs=split(qw) + split(kvw) + [
            pl.BlockSpec((None, WINDOW, kvw), lambda i: (i // steps_per_seq, 0, 0))],
        out_shape=split_shape(qw, BF16) + split_shape(kvw, F32) + [
            jax.ShapeDtypeStruct((n_tiles // seq_tiles, WINDOW, kvw), F32)],
        compiler_params=_params(1, 56),
        name="qkv",
    )(x, gain, w_qkv, ones_blockdiag, q_gain, k_gain, rope_prompt, rope_sample)


MASKED = -1e30
LOG2_E = math.log2(math.e)
Q_SCALE = HEAD_DIM ** -0.5 * LOG2_E


def _attn_prompt_body(sinks_ref, q_ref, kv_prev_ref, kv_cur_ref, *rest, n_cast):
    cast_src, (o_ref,), cast_dst = rest[:n_cast], rest[n_cast:n_cast + 1], rest[n_cast + 1:]
    _cast_blocks(cast_src, cast_dst)
    n = pl.program_id(1)
    blk = WINDOW
    n_blocks = q_ref.shape[0] // blk
    kw = N_KV_HEADS * HEAD_DIM
    r = lax.broadcasted_iota(jnp.int32, (blk, blk), 0)
    c = lax.broadcasted_iota(jnp.int32, (blk, blk), 1)
    from_prev = c > r
    missing_prev = from_prev & (n == 0)
    lower = lax.broadcasted_iota(jnp.int32, (1, LANES), 1) < HEAD_DIM
    heads_per_col = LANES // HEAD_DIM

    def padded_operands(kv):
        per_head = []
        for g in range(N_KV_HEADS):
            col, half = divmod(g, heads_per_col)
            kcol = kv[:, col * LANES:(col + 1) * LANES]
            vcol = kv[:, kw + col * LANES:kw + (col + 1) * LANES]
            kswap = pltpu.roll(kcol, HEAD_DIM, axis=1)
            vswap = pltpu.roll(vcol, HEAD_DIM, axis=1)
            in_lower = (kcol, vcol) if half == 0 else (kswap, vswap)
            in_upper = (kswap, vswap) if half == 0 else (kcol, vcol)
            per_head.append(tuple(jnp.where(lower, t, 0.0).astype(BF16) for t in in_lower)
                            + tuple(jnp.where(lower, 0.0, t).astype(BF16) for t in in_upper))
        return per_head

    key_blocks = [padded_operands(kv_prev_ref[...])]
    for b in range(n_blocks):
        rows = pl.ds(b * blk, blk)
        key_blocks.append(padded_operands(kv_cur_ref[rows, :]))
        for g in range(N_KV_HEADS):
            k_lo, v_lo, k_hi, v_hi = (jnp.concatenate(prev_and_own, axis=0) for prev_and_own
                                      in zip(key_blocks[b][g], key_blocks[b + 1][g]))
            for pair in range(GROUP // heads_per_col):
                h0 = g * GROUP + pair * heads_per_col
                lanes = pl.ds(h0 * HEAD_DIM, LANES)
                q_pair = q_ref[rows, lanes]
                acc = None
                for h, k, v in ((h0, k_lo, v_lo), (h0 + 1, k_hi, v_hi)):
                    s = lax.dot_general(q_pair, k, (((1,), (1,)), ((), ())),
                                        preferred_element_type=F32)
                    s = jnp.where(from_prev, s[:, :blk], s[:, blk:])
                    if b == 0:
                        s = jnp.where(missing_prev, MASKED, s)
                    sink = sinks_ref[h] * LOG2_E
                    m = jnp.maximum(jnp.max(s, axis=-1, keepdims=True), sink)
                    p = jnp.exp2(s - m)
                    den = jnp.sum(p, axis=-1, keepdims=True) + jnp.exp2(sink - m)
                    p = jnp.concatenate([jnp.where(from_prev, p, 0.0),
                                         jnp.where(from_prev, 0.0, p)], axis=1)
                    o = _dot(p.astype(BF16), v) * (1.0 / den)
                    acc = o if acc is None else acc + o
                o_ref[rows, lanes] = acc.astype(BF16)


def _attn_prompt(sinks, q, kv, *, batch, seq, tq, cast=()):
    m, qw = q.shape
    kvw = kv.shape[1]
    blk = WINDOW
    nt = seq // tq
    per_tile = tq // blk
    grid = (batch, nt)
    cast_in, cast_out, cast_shapes = _cast_specs(cast, grid)
    return pl.pallas_call(
        functools.partial(_attn_prompt_body, n_cast=len(cast)),
        grid_spec=pltpu.PrefetchScalarGridSpec(
            num_scalar_prefetch=1,
            grid=grid,
            in_specs=[
                pl.BlockSpec((tq, qw), lambda b, n, s: (b * nt + n, 0)),
                pl.BlockSpec((blk, kvw), lambda b, n, s: (
                    (b * nt + n) * per_tile - jnp.minimum(n, 1), 0)),
                pl.BlockSpec((tq, kvw), lambda b, n, s: (b * nt + n, 0)),
            ] + cast_in,
            out_specs=[pl.BlockSpec((tq, qw), lambda b, n, s: (b * nt + n, 0))] + cast_out,
        ),
        out_shape=[jax.ShapeDtypeStruct((m, qw), BF16)] + cast_shapes,
        compiler_params=_params(2, 56),
        name="attn_prompt",
    )(sinks, q, kv, kv, *[w for w, _ in cast])


def _attn_sample_body(sinks_ref, q_ref, kv_ref, ck_ref, cv_ref, o_ref, knew_ref, vnew_ref, *,
                      seq_len):
    nseq, win, kw = ck_ref.shape
    rows = GROUP * seq_len
    q = q_ref[...].astype(F32)
    kv_new = kv_ref[...]
    for new_ref, cache_ref, fresh in ((knew_ref, ck_ref, kv_new[:, :kw]),
                                      (vnew_ref, cv_ref, kv_new[:, kw:])):
        new_ref[:, :win - seq_len, :] = cache_ref[:, seq_len:, :]
        new_ref[:, win - seq_len:, :] = fresh.reshape(nseq, seq_len, kw)
    pad = jnp.zeros((nseq, win - seq_len, HEAD_DIM), F32)
    step = lax.broadcasted_iota(jnp.int32, (1, rows, 2 * win), 1) % seq_len
    c = lax.broadcasted_iota(jnp.int32, (1, rows, 2 * win), 2)
    visible = ((c < win) & (c > step + (win - WINDOW))) | ((c >= win) & (c - win <= step))
    pieces = []
    for g in range(N_KV_HEADS):
        lo, hi = g * HEAD_DIM, (g + 1) * HEAD_DIM
        heads = range(g * GROUP, (g + 1) * GROUP)
        k_new = kv_new[:, lo:hi].reshape(nseq, seq_len, HEAD_DIM)
        v_new = kv_new[:, kw + lo:kw + hi].reshape(nseq, seq_len, HEAD_DIM)
        k = jnp.concatenate([ck_ref[:, :, lo:hi], k_new, pad], axis=1).astype(BF16)
        v = jnp.concatenate([cv_ref[:, :, lo:hi], v_new, pad], axis=1).astype(BF16)
        qg = jnp.concatenate(
            [q[:, h * HEAD_DIM:(h + 1) * HEAD_DIM].reshape(nseq, seq_len, HEAD_DIM)
             for h in heads], axis=1).astype(BF16)
        s = jnp.einsum("bqd,bkd->bqk", qg, k, preferred_element_type=F32)
        s = jnp.where(visible, s, MASKED)
        sink_col = jnp.concatenate(
            [jnp.full((1, seq_len, 1), sinks_ref[h] * LOG2_E, F32) for h in heads], axis=1)
        m = jnp.maximum(jnp.max(s, axis=-1, keepdims=True), sink_col)
        p = jnp.exp2(s - m)
        den = jnp.sum(p, axis=-1, keepdims=True) + jnp.exp2(sink_col - m)
        o = jnp.einsum("bqk,bkd->bqd", p.astype(BF16), v, preferred_element_type=F32) / den
        pieces += [o[:, t * seq_len:(t + 1) * seq_len].reshape(nseq * seq_len, HEAD_DIM)
                   for t in range(GROUP)]
    o_ref[...] = jnp.concatenate(pieces, axis=1).astype(BF16)


def _attn_sample(sinks, q, kv, cache_k, cache_v, *, seq_len):
    m, qw = q.shape
    body = functools.partial(_attn_sample_body, seq_len=seq_len)
    full = lambda shape: pl.BlockSpec(shape, lambda i, s: (0,) * len(shape))
    return pl.pallas_call(
        body,
        grid_spec=pltpu.PrefetchScalarGridSpec(
            num_scalar_prefetch=1,
            grid=(1,),
            in_specs=[full(q.shape), full(kv.shape), full(cache_k.shape), full(cache_v.shape)],
            out_specs=[full((m, qw)), full(cache_k.shape), full(cache_v.shape)],
        ),
        out_shape=[jax.ShapeDtypeStruct((m, qw), BF16),
                   jax.ShapeDtypeStruct(cache_k.shape, F32),
                   jax.ShapeDtypeStruct(cache_v.shape, F32)],
        compiler_params=_params(1, 48),
        name="attn_sample",
    )(sinks, q, kv, cache_k, cache_v)


def _rope_tables(pos):
    half = HEAD_DIM // 2
    inv = ROPE_THETA ** (-jnp.arange(half, dtype=F32) / half)
    ang = pos.astype(F32)[:, None] * inv[None, :]
    cos, sin = jnp.cos(ang), jnp.sin(ang)
    reps = LANES // HEAD_DIM
    cos = jnp.tile(jnp.concatenate([cos, cos], axis=1), (1, reps))
    sin_signed = jnp.tile(jnp.concatenate([-sin, sin], axis=1), (1, reps))
    return jnp.concatenate([cos, sin_signed], axis=1)


def kernel(x_prompt, x_sample, state_conv, cache_k_win, cache_v_win, ln_mix, ln_mlp,
           w_conv_in, w_conv, w_conv_out, w_qkv, w_attn_out, q_norm, k_norm, sinks,
           w_up, w_down):
    bp, tp, d = x_prompt.shape
    bs, ts, _ = x_sample.shape
    win = cache_k_win.shape[2]
    kw = N_KV_HEADS * HEAD_DIM
    assert ts == SUBLANES and win == WINDOW and tp % WINDOW == 0

    tm_p = 512
    tm_conv = 1024
    m_s = bs * ts
    n_tiles = bp * tp // tm_p
    rows_s = m_s // n_tiles
    assert rows_s % (2 * SUBLANES) == 0 and rows_s % ts == 0

    xp = x_prompt.reshape(bp * tp, d)
    xs = x_sample.reshape(m_s, d)

    gain = ln_mix[0][None]
    past = state_conv[0]
    zeros = lambda n: jnp.zeros((bs, n, d), F32)
    p1 = jnp.concatenate([past[:, 1:2], zeros(ts - 1)], axis=1).reshape(m_s, d)
    p2 = jnp.concatenate([past, zeros(ts - 2)], axis=1).reshape(m_s, d)
    gate_s, u_s, *w_bcv = _conv_in_sample(xs, gain, w_conv_in[0], w_conv[0], p1, p2,
                                          seq_len=ts, tn=512)
    gate_p, u_tail, wu0, wd0, w_out = _conv_in_prompt(
        xp, gain, w_bcv, w_conv[0], seq=tp, tm=tm_conv, tn=512,
        cast=[(w_up, 0), (w_down, 0), (w_conv_out, 0)])
    x = _proj_residual((xp, xs), (gate_p, gate_s), w_out, n_tiles=n_tiles)
    tiles_per_seq = tp // tm_conv
    new_conv_prompt = u_tail[tiles_per_seq - 1::tiles_per_seq, SUBLANES - (CONV_WIDTH - 1):][None]
    new_conv_sample = u_s.reshape(bs, ts, d)[:, ts - (CONV_WIDTH - 1):][None]

    x, wqkv = _mlp(x, ln_mlp[0][None], wu0, wd0, n_tiles=n_tiles, tf=2048, cast=[(w_qkv, 0)])

    gain = ln_mix[1][None]
    tn = 2 * kw
    head_id = jnp.arange(tn) // HEAD_DIM
    ones_blockdiag = (head_id[:, None] == head_id[None, :]).astype(BF16)
    q_gain = jnp.tile(q_norm[0] * Q_SCALE, tn // HEAD_DIM)[None]
    k_gain = jnp.tile(k_norm[0], kw // HEAD_DIM)[None]
    rope_p = _rope_tables(jnp.arange(tp, dtype=jnp.int32))
    rope_s = _rope_tables(PAST_LEN + jnp.arange(ts, dtype=jnp.int32))
    rope_s = jnp.tile(rope_s, (rows_s // ts, 1))
    q_p, q_s, kv_p, kv_s, kv_tail = _qkv(
        x, gain, wqkv, ones_blockdiag, q_gain, k_gain, rope_p, rope_s,
        n_tiles=n_tiles, prompt_rows=tm_p, seq_tiles=tp // tm_p, tiles_per_step=2)
    o_p, wu1, wd1, wo = _attn_prompt(sinks[0], q_p, kv_p, batch=bp, seq=tp, tq=512,
                                     cast=[(w_up, 1), (w_down, 1), (w_attn_out, 0)])
    ck = cache_k_win[0].reshape(bs, win, kw)
    cv = cache_v_win[0].reshape(bs, win, kw)
    o_s, new_k_sample, new_v_sample = _attn_sample(sinks[0], q_s, kv_s, ck, cv, seq_len=ts)
    x = _proj_residual((x,), (o_p, o_s), wo, n_tiles=n_tiles)

    new_k_prompt = kv_tail[:, :, :kw].reshape(1, bp, WINDOW, N_KV_HEADS, HEAD_DIM)
    new_v_prompt = kv_tail[:, :, kw:].reshape(1, bp, WINDOW, N_KV_HEADS, HEAD_DIM)
    new_k_sample = new_k_sample.reshape(1, bs, win, N_KV_HEADS, HEAD_DIM)
    new_v_sample = new_v_sample.reshape(1, bs, win, N_KV_HEADS, HEAD_DIM)

    xp, xs = _mlp(x, ln_mlp[1][None], wu1, wd1, n_tiles=n_tiles, tf=2048,
                  out_rows=(tm_p, rows_s))

    return (xp.reshape(bp, tp, d), xs.reshape(bs, ts, d), new_conv_prompt, new_conv_sample,
            new_k_prompt, new_v_prompt, new_k_sample, new_v_sample)
```

```python
import functools
import math

import jax
import jax.numpy as jnp
from jax import lax
from jax.experimental import pallas as pl
from jax.experimental.pallas import tpu as pltpu

F32 = jnp.float32
BF16 = jnp.bfloat16

HEAD_DIM = 64
N_HEADS = 32
N_KV_HEADS = 4
GROUP = N_HEADS // N_KV_HEADS
WINDOW = 128
PAST_LEN = 16384
ROPE_THETA = 10000.0
EPS = 1e-6
CONV_WIDTH = 3

SUBLANES = 8
LANES = 128
MIB = 1024 * 1024
QKV_ROW_CHUNK = 256
CONV_ROW_CHUNK = 256
MLP_NORM_ROW_CHUNK = 256
MLP_FF_CHUNK = 1024


def _params(n_grid_dims, vmem_mib):
    return pltpu.CompilerParams(
        dimension_semantics=("arbitrary",) * n_grid_dims,
        vmem_limit_bytes=vmem_mib * MIB,
    )


def _rms_norm(x, gain):
    return x * lax.rsqrt(jnp.mean(x * x, axis=-1, keepdims=True) + EPS) * gain


def _dot(a, b):
    return jnp.dot(a, b, preferred_element_type=F32)


def _cast_specs(weights, grid):
    n_steps = 1
    for g in grid:
        n_steps *= g

    def step_of(*ids):
        step = ids[0]
        for g, idx in zip(grid[1:], ids[1:len(grid)]):
            step = step * g + idx
        return step

    in_specs, out_specs, out_shapes = [], [], []
    for w, layer, *column_range in weights:
        rows = w.shape[1]
        first, cols = column_range[0] if column_range else (0, w.shape[2])
        assert rows % (n_steps * 2 * SUBLANES) == 0 and first % cols == 0, (w.shape, n_steps)
        rb = rows // n_steps
        in_specs.append(pl.BlockSpec(
            (None, rb, cols),
            lambda *ids, layer=layer, col=first // cols: (layer, step_of(*ids), col)))
        out_specs.append(pl.BlockSpec((rb, cols), lambda *ids: (step_of(*ids), 0)))
        out_shapes.append(jax.ShapeDtypeStruct((rows, cols), BF16))
    return in_specs, out_specs, out_shapes


def _cast_blocks(src_refs, dst_refs):
    for src, dst in zip(src_refs, dst_refs):
        dst[...] = src[...].astype(BF16)


class _RowParts:
    def __init__(self, refs):
        self.refs = tuple(refs)
        self.rows = sum(r.shape[0] for r in self.refs)

    def _pieces(self, start, size):
        offset = 0
        for ref in self.refs:
            lo, hi = max(start, offset), min(start + size, offset + ref.shape[0])
            if lo < hi:
                yield ref, lo - offset, lo - start, hi - lo
            offset += ref.shape[0]

    def load(self, start=0, size=None):
        size = self.rows - start if size is None else size
        pieces = [ref[at:at + n, :] for ref, at, _, n in self._pieces(start, size)]
        return pieces[0] if len(pieces) == 1 else jnp.concatenate(pieces, axis=0)

    def store(self, start, value, cols=slice(None), accumulate=False):
        for ref, at, src, n in self._pieces(start, value.shape[0]):
            if accumulate:
                ref[at:at + n, cols] += value[src:src + n]
            else:
                ref[at:at + n, cols] = value[src:src + n]


def _row_chunks(rows, chunk):
    n = max(rows // chunk, 1)
    return [(c * chunk, chunk if c < n - 1 else rows - c * chunk) for c in range(n)]


def _mlp_body(x_ref, g_ref, wu_ref, wd_ref, *rest, n_cast, n_out, ff_chunk):
    cast_src, out_refs, cast_dst, (h_ref,) = (
        rest[:n_cast], rest[n_cast:n_cast + n_out], rest[n_cast + n_out:2 * n_cast + n_out],
        rest[2 * n_cast + n_out:])
    out = _RowParts(out_refs)
    tm = x_ref.shape[0]
    tf = wu_ref.shape[1]

    def add_ffn(start, h, base):
        for t in range(tf // ff_chunk):
            cols = pl.ds(t * ff_chunk, ff_chunk)
            a = jnp.maximum(_dot(h, wu_ref[:, cols]), 0.0)
            part = _dot((a * a).astype(BF16), wd_ref[cols, :])
            if base is None:
                out.store(start, part, accumulate=True)
            else:
                out.store(start, base + part)
                base = None

    @pl.when(pl.program_id(1) == 0)
    def _():
        _cast_blocks(cast_src, cast_dst)
        for start, size in _row_chunks(tm, MLP_NORM_ROW_CHUNK):
            rows = pl.ds(start, size)
            x = x_ref[rows, :]
            h = _rms_norm(x, g_ref[...]).astype(BF16)
            h_ref[rows, :] = h
            add_ffn(start, h, x)

    @pl.when(pl.program_id(1) != 0)
    def _():
        _cast_blocks(cast_src, cast_dst)
        add_ffn(0, h_ref[...], None)


def _mlp(x, gain, w_up, w_down, *, n_tiles, tf, out_rows=None, cast=()):
    m, d = x.shape
    tm = m // n_tiles
    out_rows = (tm,) if out_rows is None else out_rows
    assert sum(out_rows) == tm
    ff = w_up.shape[1]
    grid = (n_tiles, ff // tf)
    cast_in, cast_out, cast_shapes = _cast_specs(cast, grid)
    body = functools.partial(_mlp_body, n_cast=len(cast), n_out=len(out_rows),
                             ff_chunk=min(tf, MLP_FF_CHUNK))
    return pl.pallas_call(
        body,
        grid=grid,
        in_specs=[
            pl.BlockSpec((tm, d), lambda i, f: (i, 0)),
            pl.BlockSpec((1, d), lambda i, f: (0, 0)),
            pl.BlockSpec((d, tf), lambda i, f: (0, f)),
            pl.BlockSpec((tf, d), lambda i, f: (f, 0)),
        ] + cast_in,
        out_specs=[pl.BlockSpec((r, d), lambda i, f: (i, 0)) for r in out_rows] + cast_out,
        out_shape=[jax.ShapeDtypeStruct((r * n_tiles, d), F32) for r in out_rows] + cast_shapes,
        scratch_shapes=[pltpu.VMEM((tm, d), BF16)],
        compiler_params=_params(2, 58),
        name="mlp",
    )(x, gain, w_up, w_down, *[c[0] for c in cast])


def _proj_body(*refs, n_x, n_a):
    x, a = _RowParts(refs[:n_x]), _RowParts(refs[n_x:n_x + n_a])
    w_ref, o_ref = refs[n_x + n_a:]
    o_ref[...] = x.load() + _dot(a.load(), w_ref[...])


def _proj_residual(x_parts, a_parts, w, *, n_tiles):
    d = w.shape[1]
    tm = sum(p.shape[0] for p in x_parts) // n_tiles
    assert tm == sum(p.shape[0] for p in a_parts) // n_tiles
    tile_spec = lambda p: pl.BlockSpec((p.shape[0] // n_tiles, p.shape[1]), lambda i: (i, 0))
    return pl.pallas_call(
        functools.partial(_proj_body, n_x=len(x_parts), n_a=len(a_parts)),
        grid=(n_tiles,),
        in_specs=[tile_spec(p) for p in (*x_parts, *a_parts)] + [
            pl.BlockSpec(w.shape, lambda i: (0, 0), pipeline_mode=pl.Buffered(1))],
        out_specs=pl.BlockSpec((tm, d), lambda i: (i, 0)),
        out_shape=jax.ShapeDtypeStruct((tm * n_tiles, d), F32),
        compiler_params=_params(1, 48),
        name="proj_residual",
    )(*x_parts, *a_parts, w)


def _conv_taps(u, u1, u2, w):
    return w[0:1, :] * u2 + w[1:2, :] * u1 + w[2:3, :] * u


def _conv_in_prompt_body(x_ref, g_ref, wb_ref, wc_ref, wv_ref, wconv_ref, *rest,
                         tiles_per_seq, n_cast):
    cast_src, (gate_ref, utail_ref), cast_dst, (h_ref, carry_ref) = (
        rest[:n_cast], rest[n_cast:n_cast + 2], rest[n_cast + 2:2 * n_cast + 2],
        rest[2 * n_cast + 2:])
    i = pl.program_id(0)
    j = pl.program_id(1)

    @pl.when(i % tiles_per_seq == 0)
    def _():
        carry_ref[j] = jnp.zeros(carry_ref.shape[1:], F32)

    tm = h_ref.shape[0]
    chunk = min(tm, CONV_ROW_CHUNK)
    row8 = lax.broadcasted_iota(jnp.int32, (SUBLANES, 1), 0)
    w_conv = wconv_ref[...]

    def run(first_column_tile):
        _cast_blocks(cast_src, cast_dst)
        prev = carry_ref[j]
        for c in range(tm // chunk):
            rows = pl.ds(c * chunk, chunk)
            if first_column_tile:
                h = _rms_norm(x_ref[rows, :], g_ref[...]).astype(BF16)
                h_ref[rows, :] = h
            else:
                h = h_ref[rows, :]
            u = _dot(h, wc_ref[...]) * _dot(h, wv_ref[...])
            u1 = pltpu.roll(u, 1, axis=0)
            u2 = pltpu.roll(u, 2, axis=0)
            top1 = jnp.where(row8 < 1, pltpu.roll(prev, 1, axis=0), u1[0:SUBLANES])
            top2 = jnp.where(row8 < 2, pltpu.roll(prev, 2, axis=0), u2[0:SUBLANES])
            u1 = jnp.concatenate([top1, u1[SUBLANES:]], axis=0)
            u2 = jnp.concatenate([top2, u2[SUBLANES:]], axis=0)
            conv = _conv_taps(u, u1, u2, w_conv)
            gate_ref[rows, :] = (_dot(h, wb_ref[...]) * conv).astype(BF16)
            prev = u[chunk - SUBLANES:]
        carry_ref[j] = prev
        utail_ref[0] = prev

    pl.when(j == 0)(functools.partial(run, True))
    pl.when(j != 0)(functools.partial(run, False))


def _conv_in_sample_body(x_ref, g_ref, wb_ref, wc_ref, wv_ref, wconv_ref, p1_ref, p2_ref,
                         gate_ref, u_ref, wb16_ref, wc16_ref, wv16_ref, h_ref, *, seq_len):
    j = pl.program_id(1)

    @pl.when(j == 0)
    def _():
        h_ref[...] = _rms_norm(x_ref[...], g_ref[...]).astype(BF16)

    _cast_blocks((wb_ref, wc_ref, wv_ref), (wb16_ref, wc16_ref, wv16_ref))
    h = h_ref[...]
    u = _dot(h, wc16_ref[...]) * _dot(h, wv16_ref[...])
    step = lax.broadcasted_iota(jnp.int32, (u.shape[0], 1), 0) % seq_len
    u1 = jnp.where(step < 1, p1_ref[...], pltpu.roll(u, 1, axis=0))
    u2 = jnp.where(step < 2, p2_ref[...], pltpu.roll(u, 2, axis=0))
    conv = _conv_taps(u, u1, u2, wconv_ref[...])
    u_ref[...] = u
    gate_ref[...] = (_dot(h, wb16_ref[...]) * conv).astype(BF16)


def _conv_in_specs(tm, d, tn, *, fused_weight):
    third = d // tn if fused_weight else 0
    return [
        pl.BlockSpec((tm, d), lambda i, j: (i, 0)),
        pl.BlockSpec((1, d), lambda i, j: (0, 0)),
        pl.BlockSpec((d, tn), lambda i, j: (0, j)),
        pl.BlockSpec((d, tn), lambda i, j: (0, third + j)),
        pl.BlockSpec((d, tn), lambda i, j: (0, 2 * third + j)),
        pl.BlockSpec((CONV_WIDTH, tn), lambda i, j: (0, j)),
    ]


def _conv_in_prompt(x, gain, w_bcv, w_conv, *, seq, tm, tn, cast=()):
    m, d = x.shape
    grid = (m // tm, d // tn)
    cast_in, cast_out, cast_shapes = _cast_specs(cast, grid)
    body = functools.partial(_conv_in_prompt_body, tiles_per_seq=seq // tm, n_cast=len(cast))
    return pl.pallas_call(
        body,
        grid=grid,
        in_specs=_conv_in_specs(tm, d, tn, fused_weight=False) + cast_in,
        out_specs=[
            pl.BlockSpec((tm, tn), lambda i, j: (i, j)),
            pl.BlockSpec((1, SUBLANES, tn), lambda i, j: (i, 0, j)),
        ] + cast_out,
        out_shape=[
            jax.ShapeDtypeStruct((m, d), BF16),
            jax.ShapeDtypeStruct((m // tm, SUBLANES, d), F32),
        ] + cast_shapes,
        scratch_shapes=[
            pltpu.VMEM((tm, d), BF16),
            pltpu.VMEM((d // tn, SUBLANES, tn), F32),
        ],
        compiler_params=_params(2, 56),
        name="conv_in_prompt",
    )(x, gain, *w_bcv, w_conv, *[c[0] for c in cast])


def _conv_in_sample(x, gain, w_in, w_conv, p1, p2, *, seq_len, tn):
    m, d = x.shape
    body = functools.partial(_conv_in_sample_body, seq_len=seq_len)
    column_tile = lambda rows: pl.BlockSpec((rows, tn), lambda i, j: (0, j))
    return pl.pallas_call(
        body,
        grid=(1, d // tn),
        in_specs=_conv_in_specs(m, d, tn, fused_weight=True) + [column_tile(m)] * 2,
        out_specs=[column_tile(m)] * 2 + [column_tile(d)] * 3,
        out_shape=[
            jax.ShapeDtypeStruct((m, d), BF16),
            jax.ShapeDtypeStruct((m, d), F32),
        ] + [jax.ShapeDtypeStruct((d, d), BF16)] * 3,
        scratch_shapes=[pltpu.VMEM((m, d), BF16)],
        compiler_params=_params(2, 48),
        name="conv_in_sample",
    )(x, gain, w_in, w_in, w_in, w_conv, p1, p2)


def _head_norm_rope(z, gain, ones_blockdiag, cos, sin_signed, *, split_ssq):
    w = z.shape[1]
    zz = z * z
    hi = zz.astype(BF16)
    ssq = _dot(hi, ones_blockdiag)
    if split_ssq:
        ssq += _dot((zz - hi.astype(F32)).astype(BF16), ones_blockdiag)
    zn = z * lax.rsqrt(ssq * (1.0 / HEAD_DIM) + EPS) * gain
    reps = w // cos.shape[1]
    cos = jnp.concatenate([cos] * reps, axis=1)
    sin_signed = jnp.concatenate([sin_signed] * reps, axis=1)
    lane = lax.broadcasted_iota(jnp.int32, (1, w), 1)
    first_half = (lane % HEAD_DIM) < (HEAD_DIM // 2)
    partner = jnp.where(first_half,
                        pltpu.roll(zn, w - HEAD_DIM // 2, axis=1),
                        pltpu.roll(zn, HEAD_DIM // 2, axis=1))
    return zn * cos + partner * sin_signed


def _qkv_body(x_ref, g_ref, wq_ref, wkv_ref, ones_ref, qgain_ref, kgain_ref,
              rope_p_ref, rope_s_ref,
              qp_ref, qs_ref, kvp_ref, kvs_ref, kv_tail_ref, *, seq_tiles):
    q_out, kv_out = _RowParts((qp_ref, qs_ref)), _RowParts((kvp_ref, kvs_ref))
    rope_tile = _RowParts((rope_p_ref, rope_s_ref))
    tm = x_ref.shape[0]
    qw = qp_ref.shape[1]
    tn = ones_ref.shape[0]
    kw = N_KV_HEADS * HEAD_DIM
    for start, size in _row_chunks(tm, QKV_ROW_CHUNK):
        h = _rms_norm(x_ref[pl.ds(start, size), :], g_ref[...]).astype(BF16)
        rope = rope_tile.load(start, size)
        cos, sin_signed = rope[:, :LANES], rope[:, LANES:]
        for t in range(qw // tn):
            cols = pl.ds(t * tn, tn)
            z = _dot(h, wq_ref[:, cols])
            q = _head_norm_rope(z, qgain_ref[...], ones_ref[...], cos, sin_signed,
                                split_ssq=False)
            q_out.store(start, q.astype(BF16), cols=cols)
        z = _dot(h, wkv_ref[...])
        k = _head_norm_rope(z[:, :kw], kgain_ref[...], ones_ref[0:kw, 0:kw], cos, sin_signed,
                            split_ssq=True)
        kv_out.store(start, jnp.concatenate([k, z[:, kw:]], axis=1))

    @pl.when(pl.program_id(0) % seq_tiles == seq_tiles - 1)
    def _():
        kv_tail_ref[...] = kvp_ref[kvp_ref.shape[0] - WINDOW:, :]


def _qkv(x, gain, w_q, w_kv, ones_blockdiag, q_gain, k_gain, rope_prompt, rope_sample, *,
         n_tiles, prompt_rows, seq_tiles):
    m, d = x.shape
    tm = m // n_tiles
    sample_rows = tm - prompt_rows
    qw = N_HEADS * HEAD_DIM
    kvw = 2 * N_KV_HEADS * HEAD_DIM
    tn = ones_blockdiag.shape[0]
    const = lambda shape: pl.BlockSpec(shape, lambda i: (0, 0), pipeline_mode=pl.Buffered(1))
    rope_p = pl.BlockSpec((prompt_rows, 2 * LANES), lambda i: (i % seq_tiles, 0))
    split = lambda width: [pl.BlockSpec((prompt_rows, width), lambda i: (i, 0)),
                           pl.BlockSpec((sample_rows, width), lambda i: (i, 0))]
    split_shape = lambda width, dtype: [
        jax.ShapeDtypeStruct((prompt_rows * n_tiles, width), dtype),
        jax.ShapeDtypeStruct((sample_rows * n_tiles, width), dtype)]
    return pl.pallas_call(
        functools.partial(_qkv_body, seq_tiles=seq_tiles),
        grid=(n_tiles,),
        in_specs=[
            pl.BlockSpec((tm, d), lambda i: (i, 0)),
            const((1, d)),
            const(w_q.shape),
            const(w_kv.shape),
            const((tn, tn)),
            const((1, tn)),
            const((1, kvw // 2)),
            rope_p,
            const((sample_rows, 2 * LANES)),
        ],
        out_specs=split(qw) + split(kvw) + [
            pl.BlockSpec((None, WINDOW, kvw), lambda i: (i // seq_tiles, 0, 0))],
        out_shape=split_shape(qw, BF16) + split_shape(kvw, F32) + [
            jax.ShapeDtypeStruct((n_tiles // seq_tiles, WINDOW, kvw), F32)],
        compiler_params=_params(1, 48),
        name="qkv",
    )(x, gain, w_q, w_kv, ones_blockdiag, q_gain, k_gain, rope_prompt, rope_sample)


MASKED = -1e30
LOG2_E = math.log2(math.e)
Q_SCALE = HEAD_DIM ** -0.5 * LOG2_E


def _attn_prompt_body(sinks_ref, q_ref, kv_prev_ref, kv_cur_ref, *rest, n_cast):
    cast_src, (o_ref,), cast_dst = rest[:n_cast], rest[n_cast:n_cast + 1], rest[n_cast + 1:]
    _cast_blocks(cast_src, cast_dst)
    n = pl.program_id(1)
    blk = WINDOW
    n_blocks = q_ref.shape[0] // blk
    kw = N_KV_HEADS * HEAD_DIM
    r = lax.broadcasted_iota(jnp.int32, (blk, blk), 0)
    c = lax.broadcasted_iota(jnp.int32, (blk, blk), 1)
    from_prev = c > r
    missing_prev = from_prev & (n == 0)
    lower = lax.broadcasted_iota(jnp.int32, (1, LANES), 1) < HEAD_DIM
    heads_per_col = LANES // HEAD_DIM

    def padded_operands(kv):
        per_head = []
        for g in range(N_KV_HEADS):
            col, half = divmod(g, heads_per_col)
            kcol = kv[:, col * LANES:(col + 1) * LANES]
            vcol = kv[:, kw + col * LANES:kw + (col + 1) * LANES]
            kswap = pltpu.roll(kcol, HEAD_DIM, axis=1)
            vswap = pltpu.roll(vcol, HEAD_DIM, axis=1)
            in_lower = (kcol, vcol) if half == 0 else (kswap, vswap)
            in_upper = (kswap, vswap) if half == 0 else (kcol, vcol)
            per_head.append(tuple(jnp.where(lower, t, 0.0).astype(BF16) for t in in_lower)
                            + tuple(jnp.where(lower, 0.0, t).astype(BF16) for t in in_upper))
        return per_head

    key_blocks = [padded_operands(kv_prev_ref[...])]
    for b in range(n_blocks):
        rows = pl.ds(b * blk, blk)
        key_blocks.append(padded_operands(kv_cur_ref[rows, :]))
        for g in range(N_KV_HEADS):
            k_lo, v_lo, k_hi, v_hi = (jnp.concatenate(prev_and_own, axis=0) for prev_and_own
                                      in zip(key_blocks[b][g], key_blocks[b + 1][g]))
            for pair in range(GROUP // heads_per_col):
                h0 = g * GROUP + pair * heads_per_col
                lanes = pl.ds(h0 * HEAD_DIM, LANES)
                q_pair = q_ref[rows, lanes]
                acc = None
                for h, k, v in ((h0, k_lo, v_lo), (h0 + 1, k_hi, v_hi)):
                    s = lax.dot_general(q_pair, k, (((1,), (1,)), ((), ())),
                                        preferred_element_type=F32)
                    s = jnp.where(from_prev, s[:, :blk], s[:, blk:])
                    if b == 0:
                        s = jnp.where(missing_prev, MASKED, s)
                    sink = sinks_ref[h] * LOG2_E
                    m = jnp.maximum(jnp.max(s, axis=-1, keepdims=True), sink)
                    p = jnp.exp2(s - m)
                    den = jnp.sum(p, axis=-1, keepdims=True) + jnp.exp2(sink - m)
                    p = jnp.concatenate([jnp.where(from_prev, p, 0.0),
                                         jnp.where(from_prev, 0.0, p)], axis=1)
                    o = _dot(p.astype(BF16), v) * (1.0 / den)
                    acc = o if acc is None else acc + o
                o_ref[rows, lanes] = acc.astype(BF16)


def _attn_prompt(sinks, q, kv, *, batch, seq, tq, cast=()):
    m, qw = q.shape
    kvw = kv.shape[1]
    blk = WINDOW
    nt = seq // tq
    per_tile = tq // blk
    grid = (batch, nt)
    cast_in, cast_out, cast_shapes = _cast_specs(cast, grid)
    return pl.pallas_call(
        functools.partial(_attn_prompt_body, n_cast=len(cast)),
        grid_spec=pltpu.PrefetchScalarGridSpec(
            num_scalar_prefetch=1,
            grid=grid,
            in_specs=[
                pl.BlockSpec((tq, qw), lambda b, n, s: (b * nt + n, 0)),
                pl.BlockSpec((blk, kvw), lambda b, n, s: (
                    (b * nt + n) * per_tile - jnp.minimum(n, 1), 0)),
                pl.BlockSpec((tq, kvw), lambda b, n, s: (b * nt + n, 0)),
            ] + cast_in,
            out_specs=[pl.BlockSpec((tq, qw), lambda b, n, s: (b * nt + n, 0))] + cast_out,
        ),
        out_shape=[jax.ShapeDtypeStruct((m, qw), BF16)] + cast_shapes,
        compiler_params=_params(2, 56),
        name="attn_prompt",
    )(sinks, q, kv, kv, *[c[0] for c in cast])


def _attn_sample_body(sinks_ref, q_ref, kv_ref, ck_ref, cv_ref, o_ref, knew_ref, vnew_ref, *,
                      seq_len):
    nseq, win, kw = ck_ref.shape
    rows = GROUP * seq_len
    q = q_ref[...].astype(F32)
    kv_new = kv_ref[...]
    for new_ref, cache_ref, fresh in ((knew_ref, ck_ref, kv_new[:, :kw]),
                                      (vnew_ref, cv_ref, kv_new[:, kw:])):
        new_ref[:, :win - seq_len, :] = cache_ref[:, seq_len:, :]
        new_ref[:, win - seq_len:, :] = fresh.reshape(nseq, seq_len, kw)
    pad = jnp.zeros((nseq, win - seq_len, HEAD_DIM), F32)
    step = lax.broadcasted_iota(jnp.int32, (1, rows, 2 * win), 1) % seq_len
    c = lax.broadcasted_iota(jnp.int32, (1, rows, 2 * win), 2)
    visible = ((c < win) & (c > step + (win - WINDOW))) | ((c >= win) & (c - win <= step))
    pieces = []
    for g in range(N_KV_HEADS):
        lo, hi = g * HEAD_DIM, (g + 1) * HEAD_DIM
        heads = range(g * GROUP, (g + 1) * GROUP)
        k_new = kv_new[:, lo:hi].reshape(nseq, seq_len, HEAD_DIM)
        v_new = kv_new[:, kw + lo:kw + hi].reshape(nseq, seq_len, HEAD_DIM)
        k = jnp.concatenate([ck_ref[:, :, lo:hi], k_new, pad], axis=1).astype(BF16)
        v = jnp.concatenate([cv_ref[:, :, lo:hi], v_new, pad], axis=1).astype(BF16)
        qg = jnp.concatenate(
            [q[:, h * HEAD_DIM:(h + 1) * HEAD_DIM].reshape(nseq, seq_len, HEAD_DIM)
             for h in heads], axis=1).astype(BF16)
        s = jnp.einsum("bqd,bkd->bqk", qg, k, preferred_element_type=F32)
        s = jnp.where(visible, s, MASKED)
        sink_col = jnp.concatenate(
            [jnp.full((1, seq_len, 1), sinks_ref[h] * LOG2_E, F32) for h in heads], axis=1)
        m = jnp.maximum(jnp.max(s, axis=-1, keepdims=True), sink_col)
        p = jnp.exp2(s - m)
        den = jnp.sum(p, axis=-1, keepdims=True) + jnp.exp2(sink_col - m)
        o = jnp.einsum("bqk,bkd->bqd", p.astype(BF16), v, preferred_element_type=F32) / den
        pieces += [o[:, t * seq_len:(t + 1) * seq_len].reshape(nseq * seq_len, HEAD_DIM)
                   for t in range(GROUP)]
    o_ref[...] = jnp.concatenate(pieces, axis=1).astype(BF16)


def _attn_sample(sinks, q, kv, cache_k, cache_v, *, seq_len):
    m, qw = q.shape
    body = functools.partial(_attn_sample_body, seq_len=seq_len)
    full = lambda shape: pl.BlockSpec(shape, lambda i, s: (0,) * len(shape))
    return pl.pallas_call(
        body,
        grid_spec=pltpu.PrefetchScalarGridSpec(
            num_scalar_prefetch=1,
            grid=(1,),
            in_specs=[full(q.shape), full(kv.shape), full(cache_k.shape), full(cache_v.shape)],
            out_specs=[full((m, qw)), full(cache_k.shape), full(cache_v.shape)],
        ),
        out_shape=[jax.ShapeDtypeStruct((m, qw), BF16),
                   jax.ShapeDtypeStruct(cache_k.shape, F32),
                   jax.ShapeDtypeStruct(cache_v.shape, F32)],
        compiler_params=_params(1, 48),
        name="attn_sample",
    )(sinks, q, kv, cache_k, cache_v)


def _rope_tables(pos):
    half = HEAD_DIM // 2
    inv = ROPE_THETA ** (-jnp.arange(half, dtype=F32) / half)
    ang = pos.astype(F32)[:, None] * inv[None, :]
    cos, sin = jnp.cos(ang), jnp.sin(ang)
    reps = LANES // HEAD_DIM
    cos = jnp.tile(jnp.concatenate([cos, cos], axis=1), (1, reps))
    sin_signed = jnp.tile(jnp.concatenate([-sin, sin], axis=1), (1, reps))
    return jnp.concatenate([cos, sin_signed], axis=1)


def kernel(x_prompt, x_sample, state_conv, cache_k_win, cache_v_win, ln_mix, ln_mlp,
           w_conv_in, w_conv, w_conv_out, w_qkv, w_attn_out, q_norm, k_norm, sinks,
           w_up, w_down):
    bp, tp, d = x_prompt.shape
    bs, ts, _ = x_sample.shape
    win = cache_k_win.shape[2]
    kw = N_KV_HEADS * HEAD_DIM
    assert ts == SUBLANES and win == WINDOW and tp % WINDOW == 0

    tm_p = 512
    tm_conv = 1024
    m_s = bs * ts
    n_tiles = bp * tp // tm_p
    rows_s = m_s // n_tiles
    assert rows_s % (2 * SUBLANES) == 0 and rows_s % ts == 0

    xp = x_prompt.reshape(bp * tp, d)
    xs = x_sample.reshape(m_s, d)

    gain = ln_mix[0][None]
    past = state_conv[0]
    zeros = lambda n: jnp.zeros((bs, n, d), F32)
    p1 = jnp.concatenate([past[:, 1:2], zeros(ts - 1)], axis=1).reshape(m_s, d)
    p2 = jnp.concatenate([past, zeros(ts - 2)], axis=1).reshape(m_s, d)
    gate_s, u_s, *w_bcv = _conv_in_sample(xs, gain, w_conv_in[0], w_conv[0], p1, p2,
                                          seq_len=ts, tn=512)
    gate_p, u_tail, wu0, wd0, w_out = _conv_in_prompt(
        xp, gain, w_bcv, w_conv[0], seq=tp, tm=tm_conv, tn=512,
        cast=[(w_up, 0), (w_down, 0), (w_conv_out, 0)])
    x = _proj_residual((xp, xs), (gate_p, gate_s), w_out, n_tiles=n_tiles)
    tiles_per_seq = tp // tm_conv
    new_conv_prompt = u_tail[tiles_per_seq - 1::tiles_per_seq, SUBLANES - (CONV_WIDTH - 1):][None]
    new_conv_sample = u_s.reshape(bs, ts, d)[:, ts - (CONV_WIDTH - 1):][None]

    qw = N_HEADS * HEAD_DIM
    x, wq, wkv = _mlp(x, ln_mlp[0][None], wu0, wd0, n_tiles=n_tiles, tf=2048,
                      cast=[(w_qkv, 0, (0, qw)), (w_qkv, 0, (qw, 2 * kw))])

    gain = ln_mix[1][None]
    tn = 2 * kw
    head_id = jnp.arange(tn) // HEAD_DIM
    ones_blockdiag = (head_id[:, None] == head_id[None, :]).astype(BF16)
    q_gain = jnp.tile(q_norm[0] * Q_SCALE, tn // HEAD_DIM)[None]
    k_gain = jnp.tile(k_norm[0], kw // HEAD_DIM)[None]
    rope_p = _rope_tables(jnp.arange(tp, dtype=jnp.int32))
    rope_s = _rope_tables(PAST_LEN + jnp.arange(ts, dtype=jnp.int32))
    rope_s = jnp.tile(rope_s, (rows_s // ts, 1))
    q_p, q_s, kv_p, kv_s, kv_tail = _qkv(
        x, gain, wq, wkv, ones_blockdiag, q_gain, k_gain, rope_p, rope_s,
        n_tiles=n_tiles, prompt_rows=tm_p, seq_tiles=tp // tm_p)
    o_p, wu1, wd1, wo = _attn_prompt(sinks[0], q_p, kv_p, batch=bp, seq=tp, tq=512,
                                     cast=[(w_up, 1), (w_down, 1), (w_attn_out, 0)])
    ck = cache_k_win[0].reshape(bs, win, kw)
    cv = cache_v_win[0].reshape(bs, win, kw)
    o_s, new_k_sample, new_v_sample = _attn_sample(sinks[0], q_s, kv_s, ck, cv, seq_len=ts)
    x = _proj_residual((x,), (o_p, o_s), wo, n_tiles=n_tiles)

    new_k_prompt = kv_tail[:, :, :kw].reshape(1, bp, WINDOW, N_KV_HEADS, HEAD_DIM)
    new_v_prompt = kv_tail[:, :, kw:].reshape(1, bp, WINDOW, N_KV_HEADS, HEAD_DIM)
    new_k_sample = new_k_sample.reshape(1, bs, win, N_KV_HEADS, HEAD_DIM)
    new_v_sample = new_v_sample.reshape(1, bs, win, N_KV_HEADS, HEAD_DIM)

    xp, xs = _mlp(x, ln_mlp[1][None], wu1, wd1, n_tiles=n_tiles, tf=2048,
                  out_rows=(tm_p, rows_s))

    return (xp.reshape(bp, tp, d), xs.reshape(bs, ts, d), new_conv_prompt, new_conv_sample,
            new_k_prompt, new_v_prompt, new_k_sample, new_v_sample)
```

```python
import functools
import math

import jax
import jax.numpy as jnp
from jax import lax
from jax.experimental import pallas as pl
from jax.experimental.pallas import tpu as pltpu

F32 = jnp.float32
BF16 = jnp.bfloat16

HEAD_DIM = 64
N_HEADS = 32
N_KV_HEADS = 4
GROUP = N_HEADS // N_KV_HEADS
WINDOW = 128
PAST_LEN = 16384
ROPE_THETA = 10000.0
EPS = 1e-6
CONV_WIDTH = 3

SUBLANES = 8
LANES = 128
MIB = 1024 * 1024
QKV_ROW_CHUNK = 256
CONV_ROW_CHUNK = 256
MLP_NORM_ROW_CHUNK = 256
MLP_FF_CHUNK = 1024


def _params(n_grid_dims, vmem_mib):
    return pltpu.CompilerParams(
        dimension_semantics=("arbitrary",) * n_grid_dims,
        vmem_limit_bytes=vmem_mib * MIB,
    )


def _rms_norm(x, gain):
    return x * lax.rsqrt(jnp.mean(x * x, axis=-1, keepdims=True) + EPS) * gain


def _dot(a, b):
    return jnp.dot(a, b, preferred_element_type=F32)


def _cast_specs(weights, grid):
    n_steps = 1
    for g in grid:
        n_steps *= g

    def step_of(*ids):
        step = ids[0]
        for g, idx in zip(grid[1:], ids[1:len(grid)]):
            step = step * g + idx
        return step

    in_specs, out_specs, out_shapes = [], [], []
    for w, layer in weights:
        rows, cols = w.shape[1:]
        assert rows % (n_steps * 2 * SUBLANES) == 0, (w.shape, n_steps)
        rb = rows // n_steps
        in_specs.append(pl.BlockSpec((None, rb, cols),
                                     lambda *ids, layer=layer: (layer, step_of(*ids), 0)))
        out_specs.append(pl.BlockSpec((rb, cols), lambda *ids: (step_of(*ids), 0)))
        out_shapes.append(jax.ShapeDtypeStruct((rows, cols), BF16))
    return in_specs, out_specs, out_shapes


def _cast_blocks(src_refs, dst_refs):
    for src, dst in zip(src_refs, dst_refs):
        dst[...] = src[...].astype(BF16)


class _RowParts:
    def __init__(self, refs):
        self.refs = tuple(refs)
        self.rows = sum(r.shape[0] for r in self.refs)

    def _pieces(self, start, size):
        offset = 0
        for ref in self.refs:
            lo, hi = max(start, offset), min(start + size, offset + ref.shape[0])
            if lo < hi:
                yield ref, lo - offset, lo - start, hi - lo
            offset += ref.shape[0]

    def load(self, start=0, size=None):
        size = self.rows - start if size is None else size
        pieces = [ref[at:at + n, :] for ref, at, _, n in self._pieces(start, size)]
        return pieces[0] if len(pieces) == 1 else jnp.concatenate(pieces, axis=0)

    def store(self, start, value, cols=slice(None), accumulate=False):
        for ref, at, src, n in self._pieces(start, value.shape[0]):
            if accumulate:
                ref[at:at + n, cols] += value[src:src + n]
            else:
                ref[at:at + n, cols] = value[src:src + n]


def _row_chunks(rows, chunk):
    n = max(rows // chunk, 1)
    return [(c * chunk, chunk if c < n - 1 else rows - c * chunk) for c in range(n)]


def _mlp_body(x_ref, g_ref, wu_ref, wd_ref, *rest, n_cast, n_out, ff_chunk):
    cast_src, out_refs, cast_dst, (h_ref,) = (
        rest[:n_cast], rest[n_cast:n_cast + n_out], rest[n_cast + n_out:2 * n_cast + n_out],
        rest[2 * n_cast + n_out:])
    out = _RowParts(out_refs)
    tm = x_ref.shape[0]
    tf = wu_ref.shape[1]

    def add_ffn(start, h, base):
        for t in range(tf // ff_chunk):
            cols = pl.ds(t * ff_chunk, ff_chunk)
            a = jnp.maximum(_dot(h, wu_ref[:, cols]), 0.0)
            part = _dot((a * a).astype(BF16), wd_ref[cols, :])
            if base is None:
                out.store(start, part, accumulate=True)
            else:
                out.store(start, base + part)
                base = None

    @pl.when(pl.program_id(1) == 0)
    def _():
        _cast_blocks(cast_src, cast_dst)
        for start, size in _row_chunks(tm, MLP_NORM_ROW_CHUNK):
            rows = pl.ds(start, size)
            x = x_ref[rows, :]
            h = _rms_norm(x, g_ref[...]).astype(BF16)
            h_ref[rows, :] = h
            add_ffn(start, h, x)

    @pl.when(pl.program_id(1) != 0)
    def _():
        _cast_blocks(cast_src, cast_dst)
        add_ffn(0, h_ref[...], None)


def _mlp(x, gain, w_up, w_down, *, n_tiles, tf, out_rows=None, cast=()):
    m, d = x.shape
    tm = m // n_tiles
    out_rows = (tm,) if out_rows is None else out_rows
    assert sum(out_rows) == tm
    ff = w_up.shape[1]
    grid = (n_tiles, ff // tf)
    cast_in, cast_out, cast_shapes = _cast_specs(cast, grid)
    body = functools.partial(_mlp_body, n_cast=len(cast), n_out=len(out_rows),
                             ff_chunk=min(tf, MLP_FF_CHUNK))
    return pl.pallas_call(
        body,
        grid=grid,
        in_specs=[
            pl.BlockSpec((tm, d), lambda i, f: (i, 0)),
            pl.BlockSpec((1, d), lambda i, f: (0, 0)),
            pl.BlockSpec((d, tf), lambda i, f: (0, f)),
            pl.BlockSpec((tf, d), lambda i, f: (f, 0)),
        ] + cast_in,
        out_specs=[pl.BlockSpec((r, d), lambda i, f: (i, 0)) for r in out_rows] + cast_out,
        out_shape=[jax.ShapeDtypeStruct((r * n_tiles, d), F32) for r in out_rows] + cast_shapes,
        scratch_shapes=[pltpu.VMEM((tm, d), BF16)],
        compiler_params=_params(2, 58),
        name="mlp",
    )(x, gain, w_up, w_down, *[w for w, _ in cast])


def _proj_body(*refs, n_x, n_a):
    x, a = _RowParts(refs[:n_x]), _RowParts(refs[n_x:n_x + n_a])
    w_ref, o_ref = refs[n_x + n_a:]
    o_ref[...] = x.load() + _dot(a.load(), w_ref[...])


def _proj_residual(x_parts, a_parts, w, *, n_tiles):
    d = w.shape[1]
    tm = sum(p.shape[0] for p in x_parts) // n_tiles
    assert tm == sum(p.shape[0] for p in a_parts) // n_tiles
    tile_spec = lambda p: pl.BlockSpec((p.shape[0] // n_tiles, p.shape[1]), lambda i: (i, 0))
    return pl.pallas_call(
        functools.partial(_proj_body, n_x=len(x_parts), n_a=len(a_parts)),
        grid=(n_tiles,),
        in_specs=[tile_spec(p) for p in (*x_parts, *a_parts)] + [
            pl.BlockSpec(w.shape, lambda i: (0, 0), pipeline_mode=pl.Buffered(1))],
        out_specs=pl.BlockSpec((tm, d), lambda i: (i, 0)),
        out_shape=jax.ShapeDtypeStruct((tm * n_tiles, d), F32),
        compiler_params=_params(1, 48),
        name="proj_residual",
    )(*x_parts, *a_parts, w)


def _conv_taps(u, u1, u2, w):
    return w[0:1, :] * u2 + w[1:2, :] * u1 + w[2:3, :] * u


def _conv_in_prompt_body(x_ref, g_ref, wb_ref, wc_ref, wv_ref, wconv_ref, *rest,
                         tiles_per_seq, n_cast):
    cast_src, (gate_ref, utail_ref), cast_dst, (h_ref, carry_ref) = (
        rest[:n_cast], rest[n_cast:n_cast + 2], rest[n_cast + 2:2 * n_cast + 2],
        rest[2 * n_cast + 2:])
    i = pl.program_id(0)
    j = pl.program_id(1)

    @pl.when(i % tiles_per_seq == 0)
    def _():
        carry_ref[j] = jnp.zeros(carry_ref.shape[1:], F32)

    tm = h_ref.shape[0]
    chunk = min(tm, CONV_ROW_CHUNK)
    row8 = lax.broadcasted_iota(jnp.int32, (SUBLANES, 1), 0)
    w_conv = wconv_ref[...]

    def run(first_column_tile):
        _cast_blocks(cast_src, cast_dst)
        prev = carry_ref[j]
        for c in range(tm // chunk):
            rows = pl.ds(c * chunk, chunk)
            if first_column_tile:
                h = _rms_norm(x_ref[rows, :], g_ref[...]).astype(BF16)
                h_ref[rows, :] = h
            else:
                h = h_ref[rows, :]
            u = _dot(h, wc_ref[...]) * _dot(h, wv_ref[...])
            u1 = pltpu.roll(u, 1, axis=0)
            u2 = pltpu.roll(u, 2, axis=0)
            top1 = jnp.where(row8 < 1, pltpu.roll(prev, 1, axis=0), u1[0:SUBLANES])
            top2 = jnp.where(row8 < 2, pltpu.roll(prev, 2, axis=0), u2[0:SUBLANES])
            u1 = jnp.concatenate([top1, u1[SUBLANES:]], axis=0)
            u2 = jnp.concatenate([top2, u2[SUBLANES:]], axis=0)
            conv = _conv_taps(u, u1, u2, w_conv)
            gate_ref[rows, :] = (_dot(h, wb_ref[...]) * conv).astype(BF16)
            prev = u[chunk - SUBLANES:]
        carry_ref[j] = prev
        utail_ref[0] = prev

    pl.when(j == 0)(functools.partial(run, True))
    pl.when(j != 0)(functools.partial(run, False))


def _conv_in_sample_body(x_ref, g_ref, wb_ref, wc_ref, wv_ref, wconv_ref, p1_ref, p2_ref,
                         gate_ref, u_ref, wb16_ref, wc16_ref, wv16_ref, h_ref, *, seq_len):
    j = pl.program_id(1)

    @pl.when(j == 0)
    def _():
        h_ref[...] = _rms_norm(x_ref[...], g_ref[...]).astype(BF16)

    _cast_blocks((wb_ref, wc_ref, wv_ref), (wb16_ref, wc16_ref, wv16_ref))
    h = h_ref[...]
    u = _dot(h, wc16_ref[...]) * _dot(h, wv16_ref[...])
    step = lax.broadcasted_iota(jnp.int32, (u.shape[0], 1), 0) % seq_len
    u1 = jnp.where(step < 1, p1_ref[...], pltpu.roll(u, 1, axis=0))
    u2 = jnp.where(step < 2, p2_ref[...], pltpu.roll(u, 2, axis=0))
    conv = _conv_taps(u, u1, u2, wconv_ref[...])
    u_ref[...] = u
    gate_ref[...] = (_dot(h, wb16_ref[...]) * conv).astype(BF16)


def _conv_in_specs(tm, d, tn, *, fused_weight):
    third = d // tn if fused_weight else 0
    return [
        pl.BlockSpec((tm, d), lambda i, j: (i, 0)),
        pl.BlockSpec((1, d), lambda i, j: (0, 0)),
        pl.BlockSpec((d, tn), lambda i, j: (0, j)),
        pl.BlockSpec((d, tn), lambda i, j: (0, third + j)),
        pl.BlockSpec((d, tn), lambda i, j: (0, 2 * third + j)),
        pl.BlockSpec((CONV_WIDTH, tn), lambda i, j: (0, j)),
    ]


def _conv_in_prompt(x, gain, w_bcv, w_conv, *, seq, tm, tn, cast=()):
    m, d = x.shape
    grid = (m // tm, d // tn)
    cast_in, cast_out, cast_shapes = _cast_specs(cast, grid)
    body = functools.partial(_conv_in_prompt_body, tiles_per_seq=seq // tm, n_cast=len(cast))
    return pl.pallas_call(
        body,
        grid=grid,
        in_specs=_conv_in_specs(tm, d, tn, fused_weight=False) + cast_in,
        out_specs=[
            pl.BlockSpec((tm, tn), lambda i, j: (i, j)),
            pl.BlockSpec((1, SUBLANES, tn), lambda i, j: (i, 0, j)),
        ] + cast_out,
        out_shape=[
            jax.ShapeDtypeStruct((m, d), BF16),
            jax.ShapeDtypeStruct((m // tm, SUBLANES, d), F32),
        ] + cast_shapes,
        scratch_shapes=[
            pltpu.VMEM((tm, d), BF16),
            pltpu.VMEM((d // tn, SUBLANES, tn), F32),
        ],
        compiler_params=_params(2, 56),
        name="conv_in_prompt",
    )(x, gain, *w_bcv, w_conv, *[w for w, _ in cast])


def _conv_in_sample(x, gain, w_in, w_conv, p1, p2, *, seq_len, tn):
    m, d = x.shape
    body = functools.partial(_conv_in_sample_body, seq_len=seq_len)
    column_tile = lambda rows: pl.BlockSpec((rows, tn), lambda i, j: (0, j))
    return pl.pallas_call(
        body,
        grid=(1, d // tn),
        in_specs=_conv_in_specs(m, d, tn, fused_weight=True) + [column_tile(m)] * 2,
        out_specs=[column_tile(m)] * 2 + [column_tile(d)] * 3,
        out_shape=[
            jax.ShapeDtypeStruct((m, d), BF16),
            jax.ShapeDtypeStruct((m, d), F32),
        ] + [jax.ShapeDtypeStruct((d, d), BF16)] * 3,
        scratch_shapes=[pltpu.VMEM((m, d), BF16)],
        compiler_params=_params(2, 48),
        name="conv_in_sample",
    )(x, gain, w_in, w_in, w_in, w_conv, p1, p2)


def _head_norm_rope(z, gain, ones_blockdiag, cos, sin_signed, *, split_ssq):
    w = z.shape[1]
    zz = z * z
    hi = zz.astype(BF16)
    ssq = _dot(hi, ones_blockdiag)
    if split_ssq:
        ssq += _dot((zz - hi.astype(F32)).astype(BF16), ones_blockdiag)
    zn = z * lax.rsqrt(ssq * (1.0 / HEAD_DIM) + EPS) * gain
    reps = w // cos.shape[1]
    cos = jnp.concatenate([cos] * reps, axis=1)
    sin_signed = jnp.concatenate([sin_signed] * reps, axis=1)
    lane = lax.broadcasted_iota(jnp.int32, (1, w), 1)
    first_half = (lane % HEAD_DIM) < (HEAD_DIM // 2)
    partner = jnp.where(first_half,
                        pltpu.roll(zn, w - HEAD_DIM // 2, axis=1),
                        pltpu.roll(zn, HEAD_DIM // 2, axis=1))
    return zn * cos + partner * sin_signed


def _qkv_body(x_ref, g_ref, w_ref, ones_ref, qgain_ref, kgain_ref,
              rope_p_ref, rope_s_ref,
              qp_ref, qs_ref, kvp_ref, kvs_ref, kv_tail_ref, *, seq_tiles):
    q_out, kv_out = _RowParts((qp_ref, qs_ref)), _RowParts((kvp_ref, kvs_ref))
    rope_tile = _RowParts((rope_p_ref, rope_s_ref))
    tm = x_ref.shape[0]
    qw = qp_ref.shape[1]
    tn = ones_ref.shape[0]
    kw = N_KV_HEADS * HEAD_DIM
    for start, size in _row_chunks(tm, QKV_ROW_CHUNK):
        h = _rms_norm(x_ref[pl.ds(start, size), :], g_ref[...]).astype(BF16)
        rope = rope_tile.load(start, size)
        cos, sin_signed = rope[:, :LANES], rope[:, LANES:]
        for t in range(qw // tn):
            cols = pl.ds(t * tn, tn)
            z = _dot(h, w_ref[:, cols])
            q = _head_norm_rope(z, qgain_ref[...], ones_ref[...], cos, sin_signed,
                                split_ssq=False)
            q_out.store(start, q.astype(BF16), cols=cols)
        z = _dot(h, w_ref[:, pl.ds(qw, 2 * kw)])
        k = _head_norm_rope(z[:, :kw], kgain_ref[...], ones_ref[0:kw, 0:kw], cos, sin_signed,
                            split_ssq=True)
        kv_out.store(start, jnp.concatenate([k, z[:, kw:]], axis=1))

    @pl.when(pl.program_id(0) % seq_tiles == seq_tiles - 1)
    def _():
        kv_tail_ref[...] = kvp_ref[kvp_ref.shape[0] - WINDOW:, :]


def _qkv(x, gain, w_qkv, ones_blockdiag, q_gain, k_gain, rope_prompt, rope_sample, *,
         n_tiles, prompt_rows, seq_tiles):
    m, d = x.shape
    tm = m // n_tiles
    sample_rows = tm - prompt_rows
    qw = N_HEADS * HEAD_DIM
    kvw = 2 * N_KV_HEADS * HEAD_DIM
    tn = ones_blockdiag.shape[0]
    const = lambda shape: pl.BlockSpec(shape, lambda i: (0, 0), pipeline_mode=pl.Buffered(1))
    rope_p = pl.BlockSpec((prompt_rows, 2 * LANES), lambda i: (i % seq_tiles, 0))
    split = lambda width: [pl.BlockSpec((prompt_rows, width), lambda i: (i, 0)),
                           pl.BlockSpec((sample_rows, width), lambda i: (i, 0))]
    split_shape = lambda width, dtype: [
        jax.ShapeDtypeStruct((prompt_rows * n_tiles, width), dtype),
        jax.ShapeDtypeStruct((sample_rows * n_tiles, width), dtype)]
    return pl.pallas_call(
        functools.partial(_qkv_body, seq_tiles=seq_tiles),
        grid=(n_tiles,),
        in_specs=[
            pl.BlockSpec((tm, d), lambda i: (i, 0)),
            const((1, d)),
            const(w_qkv.shape),
            const((tn, tn)),
            const((1, tn)),
            const((1, kvw // 2)),
            rope_p,
            const((sample_rows, 2 * LANES)),
        ],
        out_specs=split(qw) + split(kvw) + [
            pl.BlockSpec((None, WINDOW, kvw), lambda i: (i // seq_tiles, 0, 0))],
        out_shape=split_shape(qw, BF16) + split_shape(kvw, F32) + [
            jax.ShapeDtypeStruct((n_tiles // seq_tiles, WINDOW, kvw), F32)],
        compiler_params=_params(1, 48),
        name="qkv",
    )(x, gain, w_qkv, ones_blockdiag, q_gain, k_gain, rope_prompt, rope_sample)


MASKED = -1e30
LOG2_E = math.log2(math.e)
Q_SCALE = HEAD_DIM ** -0.5 * LOG2_E


def _attn_prompt_body(sinks_ref, q_ref, kv_prev_ref, kv_cur_ref, *rest, n_cast):
    cast_src, (o_ref,), cast_dst = rest[:n_cast], rest[n_cast:n_cast + 1], rest[n_cast + 1:]
    _cast_blocks(cast_src, cast_dst)
    n = pl.program_id(1)
    blk = WINDOW
    n_blocks = q_ref.shape[0] // blk
    kw = N_KV_HEADS * HEAD_DIM
    r = lax.broadcasted_iota(jnp.int32, (blk, blk), 0)
    c = lax.broadcasted_iota(jnp.int32, (blk, blk), 1)
    from_prev = c > r
    missing_prev = from_prev & (n == 0)
    lower = lax.broadcasted_iota(jnp.int32, (1, LANES), 1) < HEAD_DIM
    heads_per_col = LANES // HEAD_DIM

    def padded_operands(kv):
        per_head = []
        for g in range(N_KV_HEADS):
            col, half = divmod(g, heads_per_col)
            kcol = kv[:, col * LANES:(col + 1) * LANES]
            vcol = kv[:, kw + col * LANES:kw + (col + 1) * LANES]
            kswap = pltpu.roll(kcol, HEAD_DIM, axis=1)
            vswap = pltpu.roll(vcol, HEAD_DIM, axis=1)
            in_lower = (kcol, vcol) if half == 0 else (kswap, vswap)
            in_upper = (kswap, vswap) if half == 0 else (kcol, vcol)
            per_head.append(tuple(jnp.where(lower, t, 0.0).astype(BF16) for t in in_lower)
                            + tuple(jnp.where(lower, 0.0, t).astype(BF16) for t in in_upper))
        return per_head

    key_blocks = [padded_operands(kv_prev_ref[...])]
    for b in range(n_blocks):
        rows = pl.ds(b * blk, blk)
        key_blocks.append(padded_operands(kv_cur_ref[rows, :]))
        for g in range(N_KV_HEADS):
            k_lo, v_lo, k_hi, v_hi = (jnp.concatenate(prev_and_own, axis=0) for prev_and_own
                                      in zip(key_blocks[b][g], key_blocks[b + 1][g]))
            for pair in range(GROUP // heads_per_col):
                h0 = g * GROUP + pair * heads_per_col
                lanes = pl.ds(h0 * HEAD_DIM, LANES)
                q_pair = q_ref[rows, lanes]
                acc = None
                for h, k, v in ((h0, k_lo, v_lo), (h0 + 1, k_hi, v_hi)):
                    s = lax.dot_general(q_pair, k, (((1,), (1,)), ((), ())),
                                        preferred_element_type=F32)
                    s = jnp.where(from_prev, s[:, :blk], s[:, blk:])
                    if b == 0:
                        s = jnp.where(missing_prev, MASKED, s)
                    sink = sinks_ref[h] * LOG2_E
                    m = jnp.maximum(jnp.max(s, axis=-1, keepdims=True), sink)
                    p = jnp.exp2(s - m)
                    den = jnp.sum(p, axis=-1, keepdims=True) + jnp.exp2(sink - m)
                    p = jnp.concatenate([jnp.where(from_prev, p, 0.0),
                                         jnp.where(from_prev, 0.0, p)], axis=1)
                    o = _dot(p.astype(BF16), v) * (1.0 / den)
                    acc = o if acc is None else acc + o
                o_ref[rows, lanes] = acc.astype(BF16)


def _attn_prompt(sinks, q, kv, *, batch, seq, tq, cast=()):
    m, qw = q.shape
    kvw = kv.shape[1]
    blk = WINDOW
    nt = seq // tq
    per_tile = tq // blk
    grid = (batch, nt)
    cast_in, cast_out, cast_shapes = _cast_specs(cast, grid)
    return pl.pallas_call(
        functools.partial(_attn_prompt_body, n_cast=len(cast)),
        grid_spec=pltpu.PrefetchScalarGridSpec(
            num_scalar_prefetch=1,
            grid=grid,
            in_specs=[
                pl.BlockSpec((tq, qw), lambda b, n, s: (b * nt + n, 0)),
                pl.BlockSpec((blk, kvw), lambda b, n, s: (
                    (b * nt + n) * per_tile - jnp.minimum(n, 1), 0)),
                pl.BlockSpec((tq, kvw), lambda b, n, s: (b * nt + n, 0)),
            ] + cast_in,
            out_specs=[pl.BlockSpec((tq, qw), lambda b, n, s: (b * nt + n, 0))] + cast_out,
        ),
        out_shape=[jax.ShapeDtypeStruct((m, qw), BF16)] + cast_shapes,
        compiler_params=_params(2, 56),
        name="attn_prompt",
    )(sinks, q, kv, kv, *[w for w, _ in cast])


def _attn_sample_body(sinks_ref, q_ref, kv_ref, ck_ref, cv_ref, o_ref, knew_ref, vnew_ref, *,
                      seq_len):
    nseq, win, kw = ck_ref.shape
    rows = GROUP * seq_len
    q = q_ref[...].astype(F32)
    kv_new = kv_ref[...]
    for new_ref, cache_ref, fresh in ((knew_ref, ck_ref, kv_new[:, :kw]),
                                      (vnew_ref, cv_ref, kv_new[:, kw:])):
        new_ref[:, :win - seq_len, :] = cache_ref[:, seq_len:, :]
        new_ref[:, win - seq_len:, :] = fresh.reshape(nseq, seq_len, kw)
    pad = jnp.zeros((nseq, win - seq_len, HEAD_DIM), F32)
    step = lax.broadcasted_iota(jnp.int32, (1, rows, 2 * win), 1) % seq_len
    c = lax.broadcasted_iota(jnp.int32, (1, rows, 2 * win), 2)
    visible = ((c < win) & (c > step + (win - WINDOW))) | ((c >= win) & (c - win <= step))
    pieces = []
    for g in range(N_KV_HEADS):
        lo, hi = g * HEAD_DIM, (g + 1) * HEAD_DIM
        heads = range(g * GROUP, (g + 1) * GROUP)
        k_new = kv_new[:, lo:hi].reshape(nseq, seq_len, HEAD_DIM)
        v_new = kv_new[:, kw + lo:kw + hi].reshape(nseq, seq_len, HEAD_DIM)
        k = jnp.concatenate([ck_ref[:, :, lo:hi], k_new, pad], axis=1).astype(BF16)
        v = jnp.concatenate([cv_ref[:, :, lo:hi], v_new, pad], axis=1).astype(BF16)
        qg = jnp.concatenate(
            [q[:, h * HEAD_DIM:(h + 1) * HEAD_DIM].reshape(nseq, seq_len, HEAD_DIM)
             for h in heads], axis=1).astype(BF16)
        s = jnp.einsum("bqd,bkd->bqk", qg, k, preferred_element_type=F32)
        s = jnp.where(visible, s, MASKED)
        sink_col = jnp.concatenate(
            [jnp.full((1, seq_len, 1), sinks_ref[h] * LOG2_E, F32) for h in heads], axis=1)
        m = jnp.maximum(jnp.max(s, axis=-1, keepdims=True), sink_col)
        p = jnp.exp2(s - m)
        den = jnp.sum(p, axis=-1, keepdims=True) + jnp.exp2(sink_col - m)
        o = jnp.einsum("bqk,bkd->bqd", p.astype(BF16), v, preferred_element_type=F32) / den
        pieces += [o[:, t * seq_len:(t + 1) * seq_len].reshape(nseq * seq_len, HEAD_DIM)
                   for t in range(GROUP)]
    o_ref[...] = jnp.concatenate(pieces, axis=1).astype(BF16)


def _attn_sample(sinks, q, kv, cache_k, cache_v, *, seq_len):
    m, qw = q.shape
    body = functools.partial(_attn_sample_body, seq_len=seq_len)
    full = lambda shape: pl.BlockSpec(shape, lambda i, s: (0,) * len(shape))
    return pl.pallas_call(
        body,
        grid_spec=pltpu.PrefetchScalarGridSpec(
            num_scalar_prefetch=1,
            grid=(1,),
            in_specs=[full(q.shape), full(kv.shape), full(cache_k.shape), full(cache_v.shape)],
            out_specs=[full((m, qw)), full(cache_k.shape), full(cache_v.shape)],
        ),
        out_shape=[jax.ShapeDtypeStruct((m, qw), BF16),
                   jax.ShapeDtypeStruct(cache_k.shape, F32),
                   jax.ShapeDtypeStruct(cache_v.shape, F32)],
        compiler_params=_params(1, 48),
        name="attn_sample",
    )(sinks, q, kv, cache_k, cache_v)


def _rope_tables(pos):
    half = HEAD_DIM // 2
    inv = ROPE_THETA ** (-jnp.arange(half, dtype=F32) / half)
    ang = pos.astype(F32)[:, None] * inv[None, :]
    cos, sin = jnp.cos(ang), jnp.sin(ang)
    reps = LANES // HEAD_DIM
    cos = jnp.tile(jnp.concatenate([cos, cos], axis=1), (1, reps))
    sin_signed = jnp.tile(jnp.concatenate([-sin, sin], axis=1), (1, reps))
    return jnp.concatenate([cos, sin_signed], axis=1)


def kernel(x_prompt, x_sample, state_conv, cache_k_win, cache_v_win, ln_mix, ln_mlp,
           w_conv_in, w_conv, w_conv_out, w_qkv, w_attn_out, q_norm, k_norm, sinks,
           w_up, w_down):
    bp, tp, d = x_prompt.shape
    bs, ts, _ = x_sample.shape
    win = cache_k_win.shape[2]
    kw = N_KV_HEADS * HEAD_DIM
    assert ts == SUBLANES and win == WINDOW and tp % WINDOW == 0

    tm_p = 512
    tm_conv = 1024
    m_s = bs * ts
    n_tiles = bp * tp // tm_p
    rows_s = m_s // n_tiles
    assert rows_s % (2 * SUBLANES) == 0 and rows_s % ts == 0

    xp = x_prompt.reshape(bp * tp, d)
    xs = x_sample.reshape(m_s, d)

    gain = ln_mix[0][None]
    past = state_conv[0]
    zeros = lambda n: jnp.zeros((bs, n, d), F32)
    p1 = jnp.concatenate([past[:, 1:2], zeros(ts - 1)], axis=1).reshape(m_s, d)
    p2 = jnp.concatenate([past, zeros(ts - 2)], axis=1).reshape(m_s, d)
    gate_s, u_s, *w_bcv = _conv_in_sample(xs, gain, w_conv_in[0], w_conv[0], p1, p2,
                                          seq_len=ts, tn=512)
    gate_p, u_tail, wu0, wd0, w_out = _conv_in_prompt(
        xp, gain, w_bcv, w_conv[0], seq=tp, tm=tm_conv, tn=512,
        cast=[(w_up, 0), (w_down, 0), (w_conv_out, 0)])
    x = _proj_residual((xp, xs), (gate_p, gate_s), w_out, n_tiles=n_tiles)
    tiles_per_seq = tp // tm_conv
    new_conv_prompt = u_tail[tiles_per_seq - 1::tiles_per_seq, SUBLANES - (CONV_WIDTH - 1):][None]
    new_conv_sample = u_s.reshape(bs, ts, d)[:, ts - (CONV_WIDTH - 1):][None]

    x, wqkv, wo = _mlp(x, ln_mlp[0][None], wu0, wd0, n_tiles=n_tiles, tf=2048,
                       cast=[(w_qkv, 0), (w_attn_out, 0)])

    gain = ln_mix[1][None]
    tn = 2 * kw
    head_id = jnp.arange(tn) // HEAD_DIM
    ones_blockdiag = (head_id[:, None] == head_id[None, :]).astype(BF16)
    q_gain = jnp.tile(q_norm[0] * Q_SCALE, tn // HEAD_DIM)[None]
    k_gain = jnp.tile(k_norm[0], kw // HEAD_DIM)[None]
    rope_p = _rope_tables(jnp.arange(tp, dtype=jnp.int32))
    rope_s = _rope_tables(PAST_LEN + jnp.arange(ts, dtype=jnp.int32))
    rope_s = jnp.tile(rope_s, (rows_s // ts, 1))
    q_p, q_s, kv_p, kv_s, kv_tail = _qkv(
        x, gain, wqkv, ones_blockdiag, q_gain, k_gain, rope_p, rope_s,
        n_tiles=n_tiles, prompt_rows=tm_p, seq_tiles=tp // tm_p)
    o_p, wu1, wd1 = _attn_prompt(sinks[0], q_p, kv_p, batch=bp, seq=tp, tq=512,
                                 cast=[(w_up, 1), (w_down, 1)])
    ck = cache_k_win[0].reshape(bs, win, kw)
    cv = cache_v_win[0].reshape(bs, win, kw)
    o_s, new_k_sample, new_v_sample = _attn_sample(sinks[0], q_s, kv_s, ck, cv, seq_len=ts)
    x = _proj_residual((x,), (o_p, o_s), wo, n_tiles=n_tiles)

    new_k_prompt = kv_tail[:, :, :kw].reshape(1, bp, WINDOW, N_KV_HEADS, HEAD_DIM)
    new_v_prompt = kv_tail[:, :, kw:].reshape(1, bp, WINDOW, N_KV_HEADS, HEAD_DIM)
    new_k_sample = new_k_sample.reshape(1, bs, win, N_KV_HEADS, HEAD_DIM)
    new_v_sample = new_v_sample.reshape(1, bs, win, N_KV_HEADS, HEAD_DIM)

    xp, xs = _mlp(x, ln_mlp[1][None], wu1, wd1, n_tiles=n_tiles, tf=2048,
                  out_rows=(tm_p, rows_s))

    return (xp.reshape(bp, tp, d), xs.reshape(bs, ts, d), new_conv_prompt, new_conv_sample,
            new_k_prompt, new_v_prompt, new_k_sample, new_v_sample)
```

```python
import functools
import math

import jax
import jax.numpy as jnp
from jax import lax
from jax.experimental import pallas as pl
from jax.experimental.pallas import tpu as pltpu

F32 = jnp.float32
BF16 = jnp.bfloat16

HEAD_DIM = 64
N_HEADS = 32
N_KV_HEADS = 4
GROUP = N_HEADS // N_KV_HEADS
WINDOW = 128
PAST_LEN = 16384
ROPE_THETA = 10000.0
EPS = 1e-6
CONV_WIDTH = 3

SUBLANES = 8
LANES = 128
MIB = 1024 * 1024
QKV_ROW_CHUNK = 256
CONV_ROW_CHUNK = 256
MLP_NORM_ROW_CHUNK = 256
MLP_FF_CHUNK = 1024


def _params(n_grid_dims, vmem_mib, independent_axes=()):
    semantics = tuple("parallel" if a in independent_axes else "arbitrary"
                      for a in range(n_grid_dims))
    return pltpu.CompilerParams(
        dimension_semantics=semantics,
        vmem_limit_bytes=vmem_mib * MIB,
    )


def _rms_norm(x, gain):
    return x * lax.rsqrt(jnp.mean(x * x, axis=-1, keepdims=True) + EPS) * gain


def _dot(a, b):
    return jnp.dot(a, b, preferred_element_type=F32)


def _cast_specs(weights, grid):
    n_steps = 1
    for g in grid:
        n_steps *= g

    def step_of(*ids):
        step = ids[0]
        for g, idx in zip(grid[1:], ids[1:len(grid)]):
            step = step * g + idx
        return step

    in_specs, out_specs, out_shapes = [], [], []
    for w, layer in weights:
        rows, cols = w.shape[1:]
        assert rows % (n_steps * 2 * SUBLANES) == 0, (w.shape, n_steps)
        rb = rows // n_steps
        in_specs.append(pl.BlockSpec((None, rb, cols),
                                     lambda *ids, layer=layer: (layer, step_of(*ids), 0)))
        out_specs.append(pl.BlockSpec((rb, cols), lambda *ids: (step_of(*ids), 0)))
        out_shapes.append(jax.ShapeDtypeStruct((rows, cols), BF16))
    return in_specs, out_specs, out_shapes


def _cast_blocks(src_refs, dst_refs):
    for src, dst in zip(src_refs, dst_refs):
        dst[...] = src[...].astype(BF16)


class _RowParts:
    def __init__(self, refs):
        self.refs = tuple(refs)
        self.rows = sum(r.shape[0] for r in self.refs)

    def _pieces(self, start, size):
        offset = 0
        for ref in self.refs:
            lo, hi = max(start, offset), min(start + size, offset + ref.shape[0])
            if lo < hi:
                yield ref, lo - offset, lo - start, hi - lo
            offset += ref.shape[0]

    def load(self, start=0, size=None):
        size = self.rows - start if size is None else size
        pieces = [ref[at:at + n, :] for ref, at, _, n in self._pieces(start, size)]
        return pieces[0] if len(pieces) == 1 else jnp.concatenate(pieces, axis=0)

    def store(self, start, value, cols=slice(None), accumulate=False):
        for ref, at, src, n in self._pieces(start, value.shape[0]):
            if accumulate:
                ref[at:at + n, cols] += value[src:src + n]
            else:
                ref[at:at + n, cols] = value[src:src + n]


def _row_chunks(rows, chunk):
    n = max(rows // chunk, 1)
    return [(c * chunk, chunk if c < n - 1 else rows - c * chunk) for c in range(n)]


def _mlp_body(x_ref, g_ref, wu_ref, wd_ref, *rest, n_cast, n_out, ff_chunk):
    cast_src, out_refs, cast_dst, (h_ref,) = (
        rest[:n_cast], rest[n_cast:n_cast + n_out], rest[n_cast + n_out:2 * n_cast + n_out],
        rest[2 * n_cast + n_out:])
    out = _RowParts(out_refs)
    tm = x_ref.shape[0]
    tf = wu_ref.shape[1]

    def add_ffn(start, h, base):
        for t in range(tf // ff_chunk):
            cols = pl.ds(t * ff_chunk, ff_chunk)
            a = jnp.maximum(_dot(h, wu_ref[:, cols]), 0.0)
            part = _dot((a * a).astype(BF16), wd_ref[cols, :])
            if base is None:
                out.store(start, part, accumulate=True)
            else:
                out.store(start, base + part)
                base = None

    @pl.when(pl.program_id(1) == 0)
    def _():
        _cast_blocks(cast_src, cast_dst)
        for start, size in _row_chunks(tm, MLP_NORM_ROW_CHUNK):
            rows = pl.ds(start, size)
            x = x_ref[rows, :]
            h = _rms_norm(x, g_ref[...]).astype(BF16)
            h_ref[rows, :] = h
            add_ffn(start, h, x)

    @pl.when(pl.program_id(1) != 0)
    def _():
        _cast_blocks(cast_src, cast_dst)
        add_ffn(0, h_ref[...], None)


def _mlp(x, gain, w_up, w_down, *, n_tiles, tf, out_rows=None, cast=()):
    m, d = x.shape
    tm = m // n_tiles
    out_rows = (tm,) if out_rows is None else out_rows
    assert sum(out_rows) == tm
    ff = w_up.shape[1]
    grid = (n_tiles, ff // tf)
    cast_in, cast_out, cast_shapes = _cast_specs(cast, grid)
    body = functools.partial(_mlp_body, n_cast=len(cast), n_out=len(out_rows),
                             ff_chunk=min(tf, MLP_FF_CHUNK))
    return pl.pallas_call(
        body,
        grid=grid,
        in_specs=[
            pl.BlockSpec((tm, d), lambda i, f: (i, 0)),
            pl.BlockSpec((1, d), lambda i, f: (0, 0)),
            pl.BlockSpec((d, tf), lambda i, f: (0, f)),
            pl.BlockSpec((tf, d), lambda i, f: (f, 0)),
        ] + cast_in,
        out_specs=[pl.BlockSpec((r, d), lambda i, f: (i, 0)) for r in out_rows] + cast_out,
        out_shape=[jax.ShapeDtypeStruct((r * n_tiles, d), F32) for r in out_rows] + cast_shapes,
        scratch_shapes=[pltpu.VMEM((tm, d), BF16)],
        compiler_params=_params(2, 58, independent_axes=(0,)),
        name="mlp",
    )(x, gain, w_up, w_down, *[w for w, _ in cast])


def _proj_body(*refs, n_x, n_a):
    x, a = _RowParts(refs[:n_x]), _RowParts(refs[n_x:n_x + n_a])
    w_ref, o_ref = refs[n_x + n_a:]
    o_ref[...] = x.load() + _dot(a.load(), w_ref[...])


def _proj_residual(x_parts, a_parts, w, *, n_tiles):
    d = w.shape[1]
    tm = sum(p.shape[0] for p in x_parts) // n_tiles
    assert tm == sum(p.shape[0] for p in a_parts) // n_tiles
    tile_spec = lambda p: pl.BlockSpec((p.shape[0] // n_tiles, p.shape[1]), lambda i: (i, 0))
    return pl.pallas_call(
        functools.partial(_proj_body, n_x=len(x_parts), n_a=len(a_parts)),
        grid=(n_tiles,),
        in_specs=[tile_spec(p) for p in (*x_parts, *a_parts)] + [
            pl.BlockSpec(w.shape, lambda i: (0, 0), pipeline_mode=pl.Buffered(1))],
        out_specs=pl.BlockSpec((tm, d), lambda i: (i, 0)),
        out_shape=jax.ShapeDtypeStruct((tm * n_tiles, d), F32),
        compiler_params=_params(1, 48, independent_axes=(0,)),
        name="proj_residual",
    )(*x_parts, *a_parts, w)


def _conv_taps(u, u1, u2, w):
    return w[0:1, :] * u2 + w[1:2, :] * u1 + w[2:3, :] * u


def _conv_in_prompt_body(x_ref, g_ref, wb_ref, wc_ref, wv_ref, wconv_ref, *rest,
                         tiles_per_seq, n_cast):
    cast_src, (gate_ref, utail_ref), cast_dst, (h_ref, carry_ref) = (
        rest[:n_cast], rest[n_cast:n_cast + 2], rest[n_cast + 2:2 * n_cast + 2],
        rest[2 * n_cast + 2:])
    i = pl.program_id(0)
    j = pl.program_id(1)

    @pl.when(i % tiles_per_seq == 0)
    def _():
        carry_ref[j] = jnp.zeros(carry_ref.shape[1:], F32)

    tm = h_ref.shape[0]
    chunk = min(tm, CONV_ROW_CHUNK)
    row8 = lax.broadcasted_iota(jnp.int32, (SUBLANES, 1), 0)
    w_conv = wconv_ref[...]

    def run(first_column_tile):
        _cast_blocks(cast_src, cast_dst)
        prev = carry_ref[j]
        for c in range(tm // chunk):
            rows = pl.ds(c * chunk, chunk)
            if first_column_tile:
                h = _rms_norm(x_ref[rows, :], g_ref[...]).astype(BF16)
                h_ref[rows, :] = h
            else:
                h = h_ref[rows, :]
            u = _dot(h, wc_ref[...]) * _dot(h, wv_ref[...])
            u1 = pltpu.roll(u, 1, axis=0)
            u2 = pltpu.roll(u, 2, axis=0)
            top1 = jnp.where(row8 < 1, pltpu.roll(prev, 1, axis=0), u1[0:SUBLANES])
            top2 = jnp.where(row8 < 2, pltpu.roll(prev, 2, axis=0), u2[0:SUBLANES])
            u1 = jnp.concatenate([top1, u1[SUBLANES:]], axis=0)
            u2 = jnp.concatenate([top2, u2[SUBLANES:]], axis=0)
            conv = _conv_taps(u, u1, u2, w_conv)
            gate_ref[rows, :] = (_dot(h, wb_ref[...]) * conv).astype(BF16)
            prev = u[chunk - SUBLANES:]
        carry_ref[j] = prev
        utail_ref[0] = prev

    pl.when(j == 0)(functools.partial(run, True))
    pl.when(j != 0)(functools.partial(run, False))


def _conv_in_sample_body(x_ref, g_ref, wb_ref, wc_ref, wv_ref, wconv_ref, p1_ref, p2_ref,
                         gate_ref, u_ref, wb16_ref, wc16_ref, wv16_ref, h_ref, *, seq_len):
    j = pl.program_id(1)

    @pl.when(j == 0)
    def _():
        h_ref[...] = _rms_norm(x_ref[...], g_ref[...]).astype(BF16)

    _cast_blocks((wb_ref, wc_ref, wv_ref), (wb16_ref, wc16_ref, wv16_ref))
    h = h_ref[...]
    u = _dot(h, wc16_ref[...]) * _dot(h, wv16_ref[...])
    step = lax.broadcasted_iota(jnp.int32, (u.shape[0], 1), 0) % seq_len
    u1 = jnp.where(step < 1, p1_ref[...], pltpu.roll(u, 1, axis=0))
    u2 = jnp.where(step < 2, p2_ref[...], pltpu.roll(u, 2, axis=0))
    conv = _conv_taps(u, u1, u2, wconv_ref[...])
    u_ref[...] = u
    gate_ref[...] = (_dot(h, wb16_ref[...]) * conv).astype(BF16)


def _conv_in_specs(tm, d, tn, *, fused_weight):
    third = d // tn if fused_weight else 0
    return [
        pl.BlockSpec((tm, d), lambda i, j: (i, 0)),
        pl.BlockSpec((1, d), lambda i, j: (0, 0)),
        pl.BlockSpec((d, tn), lambda i, j: (0, j)),
        pl.BlockSpec((d, tn), lambda i, j: (0, third + j)),
        pl.BlockSpec((d, tn), lambda i, j: (0, 2 * third + j)),
        pl.BlockSpec((CONV_WIDTH, tn), lambda i, j: (0, j)),
    ]


def _conv_in_prompt(x, gain, w_bcv, w_conv, *, seq, tm, tn, cast=()):
    m, d = x.shape
    grid = (m // tm, d // tn)
    cast_in, cast_out, cast_shapes = _cast_specs(cast, grid)
    body = functools.partial(_conv_in_prompt_body, tiles_per_seq=seq // tm, n_cast=len(cast))
    return pl.pallas_call(
        body,
        grid=grid,
        in_specs=_conv_in_specs(tm, d, tn, fused_weight=False) + cast_in,
        out_specs=[
            pl.BlockSpec((tm, tn), lambda i, j: (i, j)),
            pl.BlockSpec((1, SUBLANES, tn), lambda i, j: (i, 0, j)),
        ] + cast_out,
        out_shape=[
            jax.ShapeDtypeStruct((m, d), BF16),
            jax.ShapeDtypeStruct((m // tm, SUBLANES, d), F32),
        ] + cast_shapes,
        scratch_shapes=[
            pltpu.VMEM((tm, d), BF16),
            pltpu.VMEM((d // tn, SUBLANES, tn), F32),
        ],
        compiler_params=_params(2, 56),
        name="conv_in_prompt",
    )(x, gain, *w_bcv, w_conv, *[w for w, _ in cast])


def _conv_in_sample(x, gain, w_in, w_conv, p1, p2, *, seq_len, tn):
    m, d = x.shape
    body = functools.partial(_conv_in_sample_body, seq_len=seq_len)
    column_tile = lambda rows: pl.BlockSpec((rows, tn), lambda i, j: (0, j))
    return pl.pallas_call(
        body,
        grid=(1, d // tn),
        in_specs=_conv_in_specs(m, d, tn, fused_weight=True) + [column_tile(m)] * 2,
        out_specs=[column_tile(m)] * 2 + [column_tile(d)] * 3,
        out_shape=[
            jax.ShapeDtypeStruct((m, d), BF16),
            jax.ShapeDtypeStruct((m, d), F32),
        ] + [jax.ShapeDtypeStruct((d, d), BF16)] * 3,
        scratch_shapes=[pltpu.VMEM((m, d), BF16)],
        compiler_params=_params(2, 48),
        name="conv_in_sample",
    )(x, gain, w_in, w_in, w_in, w_conv, p1, p2)


def _head_norm_rope(z, gain, ones_blockdiag, cos, sin_signed, *, split_ssq):
    w = z.shape[1]
    zz = z * z
    hi = zz.astype(BF16)
    ssq = _dot(hi, ones_blockdiag)
    if split_ssq:
        ssq += _dot((zz - hi.astype(F32)).astype(BF16), ones_blockdiag)
    zn = z * lax.rsqrt(ssq * (1.0 / HEAD_DIM) + EPS) * gain
    reps = w // cos.shape[1]
    cos = jnp.concatenate([cos] * reps, axis=1)
    sin_signed = jnp.concatenate([sin_signed] * reps, axis=1)
    lane = lax.broadcasted_iota(jnp.int32, (1, w), 1)
    first_half = (lane % HEAD_DIM) < (HEAD_DIM // 2)
    partner = jnp.where(first_half,
                        pltpu.roll(zn, w - HEAD_DIM // 2, axis=1),
                        pltpu.roll(zn, HEAD_DIM // 2, axis=1))
    return zn * cos + partner * sin_signed


def _qkv_body(x_ref, g_ref, w_ref, ones_ref, qgain_ref, kgain_ref,
              rope_p_ref, rope_s_ref,
              qp_ref, qs_ref, kvp_ref, kvs_ref, kv_tail_ref, *, seq_tiles):
    q_out, kv_out = _RowParts((qp_ref, qs_ref)), _RowParts((kvp_ref, kvs_ref))
    rope_tile = _RowParts((rope_p_ref, rope_s_ref))
    tm = x_ref.shape[0]
    qw = qp_ref.shape[1]
    tn = ones_ref.shape[0]
    kw = N_KV_HEADS * HEAD_DIM
    for start, size in _row_chunks(tm, QKV_ROW_CHUNK):
        h = _rms_norm(x_ref[pl.ds(start, size), :], g_ref[...]).astype(BF16)
        rope = rope_tile.load(start, size)
        cos, sin_signed = rope[:, :LANES], rope[:, LANES:]
        for t in range(qw // tn):
            cols = pl.ds(t * tn, tn)
            z = _dot(h, w_ref[:, cols])
            q = _head_norm_rope(z, qgain_ref[...], ones_ref[...], cos, sin_signed,
                                split_ssq=False)
            q_out.store(start, q.astype(BF16), cols=cols)
        z = _dot(h, w_ref[:, pl.ds(qw, 2 * kw)])
        k = _head_norm_rope(z[:, :kw], kgain_ref[...], ones_ref[0:kw, 0:kw], cos, sin_signed,
                            split_ssq=True)
        kv_out.store(start, jnp.concatenate([k, z[:, kw:]], axis=1))

    @pl.when(pl.program_id(0) % seq_tiles == seq_tiles - 1)
    def _():
        kv_tail_ref[...] = kvp_ref[kvp_ref.shape[0] - WINDOW:, :]


def _qkv(x, gain, w_qkv, ones_blockdiag, q_gain, k_gain, rope_prompt, rope_sample, *,
         n_tiles, prompt_rows, seq_tiles):
    m, d = x.shape
    tm = m // n_tiles
    sample_rows = tm - prompt_rows
    qw = N_HEADS * HEAD_DIM
    kvw = 2 * N_KV_HEADS * HEAD_DIM
    tn = ones_blockdiag.shape[0]
    const = lambda shape: pl.BlockSpec(shape, lambda i: (0, 0), pipeline_mode=pl.Buffered(1))
    rope_p = pl.BlockSpec((prompt_rows, 2 * LANES), lambda i: (i % seq_tiles, 0))
    split = lambda width: [pl.BlockSpec((prompt_rows, width), lambda i: (i, 0)),
                           pl.BlockSpec((sample_rows, width), lambda i: (i, 0))]
    split_shape = lambda width, dtype: [
        jax.ShapeDtypeStruct((prompt_rows * n_tiles, width), dtype),
        jax.ShapeDtypeStruct((sample_rows * n_tiles, width), dtype)]
    return pl.pallas_call(
        functools.partial(_qkv_body, seq_tiles=seq_tiles),
        grid=(n_tiles,),
        in_specs=[
            pl.BlockSpec((tm, d), lambda i: (i, 0)),
            const((1, d)),
            const(w_qkv.shape),
            const((tn, tn)),
            const((1, tn)),
            const((1, kvw // 2)),
            rope_p,
            const((sample_rows, 2 * LANES)),
        ],
        out_specs=split(qw) + split(kvw) + [
            pl.BlockSpec((None, WINDOW, kvw), lambda i: (i // seq_tiles, 0, 0))],
        out_shape=split_shape(qw, BF16) + split_shape(kvw, F32) + [
            jax.ShapeDtypeStruct((n_tiles // seq_tiles, WINDOW, kvw), F32)],
        compiler_params=_params(1, 48),
        name="qkv",
    )(x, gain, w_qkv, ones_blockdiag, q_gain, k_gain, rope_prompt, rope_sample)


MASKED = -1e30
LOG2_E = math.log2(math.e)
Q_SCALE = HEAD_DIM ** -0.5 * LOG2_E


def _attn_prompt_body(sinks_ref, q_ref, kv_prev_ref, kv_cur_ref, *rest, n_cast):
    cast_src, (o_ref,), cast_dst = rest[:n_cast], rest[n_cast:n_cast + 1], rest[n_cast + 1:]
    _cast_blocks(cast_src, cast_dst)
    n = pl.program_id(1)
    blk = WINDOW
    n_blocks = q_ref.shape[0] // blk
    kw = N_KV_HEADS * HEAD_DIM
    r = lax.broadcasted_iota(jnp.int32, (blk, blk), 0)
    c = lax.broadcasted_iota(jnp.int32, (blk, blk), 1)
    from_prev = c > r
    missing_prev = from_prev & (n == 0)
    lower = lax.broadcasted_iota(jnp.int32, (1, LANES), 1) < HEAD_DIM
    heads_per_col = LANES // HEAD_DIM

    def padded_operands(kv):
        per_head = []
        for g in range(N_KV_HEADS):
            col, half = divmod(g, heads_per_col)
            kcol = kv[:, col * LANES:(col + 1) * LANES]
            vcol = kv[:, kw + col * LANES:kw + (col + 1) * LANES]
            kswap = pltpu.roll(kcol, HEAD_DIM, axis=1)
            vswap = pltpu.roll(vcol, HEAD_DIM, axis=1)
            in_lower = (kcol, vcol) if half == 0 else (kswap, vswap)
            in_upper = (kswap, vswap) if half == 0 else (kcol, vcol)
            per_head.append(tuple(jnp.where(lower, t, 0.0).astype(BF16) for t in in_lower)
                            + tuple(jnp.where(lower, 0.0, t).astype(BF16) for t in in_upper))
        return per_head

    key_blocks = [padded_operands(kv_prev_ref[...])]
    for b in range(n_blocks):
        rows = pl.ds(b * blk, blk)
        key_blocks.append(padded_operands(kv_cur_ref[rows, :]))
        for g in range(N_KV_HEADS):
            k_lo, v_lo, k_hi, v_hi = (jnp.concatenate(prev_and_own, axis=0) for prev_and_own
                                      in zip(key_blocks[b][g], key_blocks[b + 1][g]))
            for pair in range(GROUP // heads_per_col):
                h0 = g * GROUP + pair * heads_per_col
                lanes = pl.ds(h0 * HEAD_DIM, LANES)
                q_pair = q_ref[rows, lanes]
                acc = None
                for h, k, v in ((h0, k_lo, v_lo), (h0 + 1, k_hi, v_hi)):
                    s = lax.dot_general(q_pair, k, (((1,), (1,)), ((), ())),
                                        preferred_element_type=F32)
                    s = jnp.where(from_prev, s[:, :blk], s[:, blk:])
                    if b == 0:
                        s = jnp.where(missing_prev, MASKED, s)
                    sink = sinks_ref[h] * LOG2_E
                    m = jnp.maximum(jnp.max(s, axis=-1, keepdims=True), sink)
                    p = jnp.exp2(s - m)
                    den = jnp.sum(p, axis=-1, keepdims=True) + jnp.exp2(sink - m)
                    p = jnp.concatenate([jnp.where(from_prev, p, 0.0),
                                         jnp.where(from_prev, 0.0, p)], axis=1)
                    o = _dot(p.astype(BF16), v) * (1.0 / den)
                    acc = o if acc is None else acc + o
                o_ref[rows, lanes] = acc.astype(BF16)


def _attn_prompt(sinks, q, kv, *, batch, seq, tq, cast=()):
    m, qw = q.shape
    kvw = kv.shape[1]
    blk = WINDOW
    nt = seq // tq
    per_tile = tq // blk
    grid = (batch, nt)
    cast_in, cast_out, cast_shapes = _cast_specs(cast, grid)
    return pl.pallas_call(
        functools.partial(_attn_prompt_body, n_cast=len(cast)),
        grid_spec=pltpu.PrefetchScalarGridSpec(
            num_scalar_prefetch=1,
            grid=grid,
            in_specs=[
                pl.BlockSpec((tq, qw), lambda b, n, s: (b * nt + n, 0)),
                pl.BlockSpec((blk, kvw), lambda b, n, s: (
                    (b * nt + n) * per_tile - jnp.minimum(n, 1), 0)),
                pl.BlockSpec((tq, kvw), lambda b, n, s: (b * nt + n, 0)),
            ] + cast_in,
            out_specs=[pl.BlockSpec((tq, qw), lambda b, n, s: (b * nt + n, 0))] + cast_out,
        ),
        out_shape=[jax.ShapeDtypeStruct((m, qw), BF16)] + cast_shapes,
        compiler_params=_params(2, 56, independent_axes=(0, 1)),
        name="attn_prompt",
    )(sinks, q, kv, kv, *[w for w, _ in cast])


def _attn_sample_body(sinks_ref, q_ref, kv_ref, ck_ref, cv_ref, o_ref, knew_ref, vnew_ref, *,
                      seq_len):
    nseq, win, kw = ck_ref.shape
    rows = GROUP * seq_len
    q = q_ref[...].astype(F32)
    kv_new = kv_ref[...]
    for new_ref, cache_ref, fresh in ((knew_ref, ck_ref, kv_new[:, :kw]),
                                      (vnew_ref, cv_ref, kv_new[:, kw:])):
        new_ref[:, :win - seq_len, :] = cache_ref[:, seq_len:, :]
        new_ref[:, win - seq_len:, :] = fresh.reshape(nseq, seq_len, kw)
    pad = jnp.zeros((nseq, win - seq_len, HEAD_DIM), F32)
    step = lax.broadcasted_iota(jnp.int32, (1, rows, 2 * win), 1) % seq_len
    c = lax.broadcasted_iota(jnp.int32, (1, rows, 2 * win), 2)
    visible = ((c < win) & (c > step + (win - WINDOW))) | ((c >= win) & (c - win <= step))
    pieces = []
    for g in range(N_KV_HEADS):
        lo, hi = g * HEAD_DIM, (g + 1) * HEAD_DIM
        heads = range(g * GROUP, (g + 1) * GROUP)
        k_new = kv_new[:, lo:hi].reshape(nseq, seq_len, HEAD_DIM)
        v_new = kv_new[:, kw + lo:kw + hi].reshape(nseq, seq_len, HEAD_DIM)
        k = jnp.concatenate([ck_ref[:, :, lo:hi], k_new, pad], axis=1).astype(BF16)
        v = jnp.concatenate([cv_ref[:, :, lo:hi], v_new, pad], axis=1).astype(BF16)
        qg = jnp.concatenate(
            [q[:, h * HEAD_DIM:(h + 1) * HEAD_DIM].reshape(nseq, seq_len, HEAD_DIM)
             for h in heads], axis=1).astype(BF16)
        s = jnp.einsum("bqd,bkd->bqk", qg, k, preferred_element_type=F32)
        s = jnp.where(visible, s, MASKED)
        sink_col = jnp.concatenate(
            [jnp.full((1, seq_len, 1), sinks_ref[h] * LOG2_E, F32) for h in heads], axis=1)
        m = jnp.maximum(jnp.max(s, axis=-1, keepdims=True), sink_col)
        p = jnp.exp2(s - m)
        den = jnp.sum(p, axis=-1, keepdims=True) + jnp.exp2(sink_col - m)
        o = jnp.einsum("bqk,bkd->bqd", p.astype(BF16), v, preferred_element_type=F32) / den
        pieces += [o[:, t * seq_len:(t + 1) * seq_len].reshape(nseq * seq_len, HEAD_DIM)
                   for t in range(GROUP)]
    o_ref[...] = jnp.concatenate(pieces, axis=1).astype(BF16)


def _attn_sample(sinks, q, kv, cache_k, cache_v, *, seq_len):
    m, qw = q.shape
    body = functools.partial(_attn_sample_body, seq_len=seq_len)
    full = lambda shape: pl.BlockSpec(shape, lambda i, s: (0,) * len(shape))
    return pl.pallas_call(
        body,
        grid_spec=pltpu.PrefetchScalarGridSpec(
            num_scalar_prefetch=1,
            grid=(1,),
            in_specs=[full(q.shape), full(kv.shape), full(cache_k.shape), full(cache_v.shape)],
            out_specs=[full((m, qw)), full(cache_k.shape), full(cache_v.shape)],
        ),
        out_shape=[jax.ShapeDtypeStruct((m, qw), BF16),
                   jax.ShapeDtypeStruct(cache_k.shape, F32),
                   jax.ShapeDtypeStruct(cache_v.shape, F32)],
        compiler_params=_params(1, 48),
        name="attn_sample",
    )(sinks, q, kv, cache_k, cache_v)


def _rope_tables(pos):
    half = HEAD_DIM // 2
    inv = ROPE_THETA ** (-jnp.arange(half, dtype=F32) / half)
    ang = pos.astype(F32)[:, None] * inv[None, :]
    cos, sin = jnp.cos(ang), jnp.sin(ang)
    reps = LANES // HEAD_DIM
    cos = jnp.tile(jnp.concatenate([cos, cos], axis=1), (1, reps))
    sin_signed = jnp.tile(jnp.concatenate([-sin, sin], axis=1), (1, reps))
    return jnp.concatenate([cos, sin_signed], axis=1)


def kernel(x_prompt, x_sample, state_conv, cache_k_win, cache_v_win, ln_mix, ln_mlp,
           w_conv_in, w_conv, w_conv_out, w_qkv, w_attn_out, q_norm, k_norm, sinks,
           w_up, w_down):
    bp, tp, d = x_prompt.shape
    bs, ts, _ = x_sample.shape
    win = cache_k_win.shape[2]
    kw = N_KV_HEADS * HEAD_DIM
    assert ts == SUBLANES and win == WINDOW and tp % WINDOW == 0

    tm_p = 512
    tm_conv = 1024
    m_s = bs * ts
    n_tiles = bp * tp // tm_p
    rows_s = m_s // n_tiles
    assert rows_s % (2 * SUBLANES) == 0 and rows_s % ts == 0

    xp = x_prompt.reshape(bp * tp, d)
    xs = x_sample.reshape(m_s, d)

    gain = ln_mix[0][None]
    past = state_conv[0]
    zeros = lambda n: jnp.zeros((bs, n, d), F32)
    p1 = jnp.concatenate([past[:, 1:2], zeros(ts - 1)], axis=1).reshape(m_s, d)
    p2 = jnp.concatenate([past, zeros(ts - 2)], axis=1).reshape(m_s, d)
    gate_s, u_s, *w_bcv = _conv_in_sample(xs, gain, w_conv_in[0], w_conv[0], p1, p2,
                                          seq_len=ts, tn=512)
    gate_p, u_tail, wu0, wd0, w_out = _conv_in_prompt(
        xp, gain, w_bcv, w_conv[0], seq=tp, tm=tm_conv, tn=512,
        cast=[(w_up, 0), (w_down, 0), (w_conv_out, 0)])
    x = _proj_residual((xp, xs), (gate_p, gate_s), w_out, n_tiles=n_tiles)
    tiles_per_seq = tp // tm_conv
    new_conv_prompt = u_tail[tiles_per_seq - 1::tiles_per_seq, SUBLANES - (CONV_WIDTH - 1):][None]
    new_conv_sample = u_s.reshape(bs, ts, d)[:, ts - (CONV_WIDTH - 1):][None]

    x, wqkv = _mlp(x, ln_mlp[0][None], wu0, wd0, n_tiles=n_tiles, tf=2048, cast=[(w_qkv, 0)])

    gain = ln_mix[1][None]
    tn = 2 * kw
    head_id = jnp.arange(tn) // HEAD_DIM
    ones_blockdiag = (head_id[:, None] == head_id[None, :]).astype(BF16)
    q_gain = jnp.tile(q_norm[0] * Q_SCALE, tn // HEAD_DIM)[None]
    k_gain = jnp.tile(k_norm[0], kw // HEAD_DIM)[None]
    rope_p = _rope_tables(jnp.arange(tp, dtype=jnp.int32))
    rope_s = _rope_tables(PAST_LEN + jnp.arange(ts, dtype=jnp.int32))
    rope_s = jnp.tile(rope_s, (rows_s // ts, 1))
    q_p, q_s, kv_p, kv_s, kv_tail = _qkv(
        x, gain, wqkv, ones_blockdiag, q_gain, k_gain, rope_p, rope_s,
        n_tiles=n_tiles, prompt_rows=tm_p, seq_tiles=tp // tm_p)
    o_p, wu1, wd1, wo = _attn_prompt(sinks[0], q_p, kv_p, batch=bp, seq=tp, tq=512,
                                     cast=[(w_up, 1), (w_down, 1), (w_attn_out, 0)])
    ck = cache_k_win[0].reshape(bs, win, kw)
    cv = cache_v_win[0].reshape(bs, win, kw)
    o_s, new_k_sample, new_v_sample = _attn_sample(sinks[0], q_s, kv_s, ck, cv, seq_len=ts)
    x = _proj_residual((x,), (o_p, o_s), wo, n_tiles=n_tiles)

    new_k_prompt = kv_tail[:, :, :kw].reshape(1, bp, WINDOW, N_KV_HEADS, HEAD_DIM)
    new_v_prompt = kv_tail[:, :, kw:].reshape(1, bp, WINDOW, N_KV_HEADS, HEAD_DIM)
    new_k_sample = new_k_sample.reshape(1, bs, win, N_KV_HEADS, HEAD_DIM)
    new_v_sample = new_v_sample.reshape(1, bs, win, N_KV_HEADS, HEAD_DIM)

    xp, xs = _mlp(x, ln_mlp[1][None], wu1, wd1, n_tiles=n_tiles, tf=2048,
                  out_rows=(tm_p, rows_s))

    return (xp.reshape(bp, tp, d), xs.reshape(bs, ts, d), new_conv_prompt, new_conv_sample,
            new_k_prompt, new_v_prompt, new_k_sample, new_v_sample)
```

```python
import functools
import math

import jax
import jax.numpy as jnp
from jax import lax
from jax.experimental import pallas as pl
from jax.experimental.pallas import tpu as pltpu

F32 = jnp.float32
BF16 = jnp.bfloat16

HEAD_DIM = 64
N_HEADS = 32
N_KV_HEADS = 4
GROUP = N_HEADS // N_KV_HEADS
WINDOW = 128
PAST_LEN = 16384
ROPE_THETA = 10000.0
EPS = 1e-6
CONV_WIDTH = 3

SUBLANES = 8
LANES = 128
MIB = 1024 * 1024
QKV_ROW_CHUNK = 256
CONV_ROW_CHUNK = 256
MLP_NORM_ROW_CHUNK = 256
MLP_FF_CHUNK = 1024


def _params(n_grid_dims, vmem_mib):
    return pltpu.CompilerParams(
        dimension_semantics=("arbitrary",) * n_grid_dims,
        vmem_limit_bytes=vmem_mib * MIB,
    )


def _rms_norm(x, gain):
    return x * lax.rsqrt(jnp.mean(x * x, axis=-1, keepdims=True) + EPS) * gain


def _dot(a, b):
    return jnp.dot(a, b, preferred_element_type=F32)


def _cast_specs(weights, grid):
    n_steps = 1
    for g in grid:
        n_steps *= g

    def step_of(*ids):
        step = ids[0]
        for g, idx in zip(grid[1:], ids[1:len(grid)]):
            step = step * g + idx
        return step

    in_specs, out_specs, out_shapes = [], [], []
    for w, layer in weights:
        rows, cols = w.shape[1:]
        assert rows % (n_steps * 2 * SUBLANES) == 0, (w.shape, n_steps)
        rb = rows // n_steps
        in_specs.append(pl.BlockSpec((None, rb, cols),
                                     lambda *ids, layer=layer: (layer, step_of(*ids), 0)))
        out_specs.append(pl.BlockSpec((rb, cols), lambda *ids: (step_of(*ids), 0)))
        out_shapes.append(jax.ShapeDtypeStruct((rows, cols), BF16))
    return in_specs, out_specs, out_shapes


def _cast_blocks(src_refs, dst_refs):
    for src, dst in zip(src_refs, dst_refs):
        dst[...] = src[...].astype(BF16)


class _RowParts:
    def __init__(self, refs):
        self.refs = tuple(refs)
        self.rows = sum(r.shape[0] for r in self.refs)

    def _pieces(self, start, size):
        offset = 0
        for ref in self.refs:
            lo, hi = max(start, offset), min(start + size, offset + ref.shape[0])
            if lo < hi:
                yield ref, lo - offset, lo - start, hi - lo
            offset += ref.shape[0]

    def load(self, start=0, size=None):
        size = self.rows - start if size is None else size
        pieces = [ref[at:at + n, :] for ref, at, _, n in self._pieces(start, size)]
        return pieces[0] if len(pieces) == 1 else jnp.concatenate(pieces, axis=0)

    def store(self, start, value, cols=slice(None), accumulate=False):
        for ref, at, src, n in self._pieces(start, value.shape[0]):
            if accumulate:
                ref[at:at + n, cols] += value[src:src + n]
            else:
                ref[at:at + n, cols] = value[src:src + n]


def _row_chunks(rows, chunk):
    n = max(rows // chunk, 1)
    return [(c * chunk, chunk if c < n - 1 else rows - c * chunk) for c in range(n)]


def _mlp_body(x_ref, g_ref, wu_ref, wd_ref, *rest, n_cast, n_out, ff_chunk):
    cast_src, out_refs, cast_dst, (h_ref,) = (
        rest[:n_cast], rest[n_cast:n_cast + n_out], rest[n_cast + n_out:2 * n_cast + n_out],
        rest[2 * n_cast + n_out:])
    out = _RowParts(out_refs)
    tm = x_ref.shape[0]
    tf = wu_ref.shape[1]

    def add_ffn(start, h, base):
        for t in range(tf // ff_chunk):
            cols = pl.ds(t * ff_chunk, ff_chunk)
            a = jnp.maximum(_dot(h, wu_ref[:, cols]), 0.0)
            part = _dot((a * a).astype(BF16), wd_ref[cols, :])
            if base is None:
                out.store(start, part, accumulate=True)
            else:
                out.store(start, base + part)
                base = None

    @pl.when(pl.program_id(1) == 0)
    def _():
        _cast_blocks(cast_src, cast_dst)
        for start, size in _row_chunks(tm, MLP_NORM_ROW_CHUNK):
            rows = pl.ds(start, size)
            x = x_ref[rows, :]
            h = _rms_norm(x, g_ref[...]).astype(BF16)
            h_ref[rows, :] = h
            add_ffn(start, h, x)

    @pl.when(pl.program_id(1) != 0)
    def _():
        _cast_blocks(cast_src, cast_dst)
        add_ffn(0, h_ref[...], None)


def _mlp(x, gain, w_up, w_down, *, n_tiles, tf, out_rows=None, cast=()):
    m, d = x.shape
    tm = m // n_tiles
    out_rows = (tm,) if out_rows is None else out_rows
    assert sum(out_rows) == tm
    ff = w_up.shape[1]
    grid = (n_tiles, ff // tf)
    cast_in, cast_out, cast_shapes = _cast_specs(cast, grid)
    body = functools.partial(_mlp_body, n_cast=len(cast), n_out=len(out_rows),
                             ff_chunk=min(tf, MLP_FF_CHUNK))
    return pl.pallas_call(
        body,
        grid=grid,
        in_specs=[
            pl.BlockSpec((tm, d), lambda i, f: (i, 0)),
            pl.BlockSpec((1, d), lambda i, f: (0, 0)),
            pl.BlockSpec((d, tf), lambda i, f: (0, f)),
            pl.BlockSpec((tf, d), lambda i, f: (f, 0)),
        ] + cast_in,
        out_specs=[pl.BlockSpec((r, d), lambda i, f: (i, 0)) for r in out_rows] + cast_out,
        out_shape=[jax.ShapeDtypeStruct((r * n_tiles, d), F32) for r in out_rows] + cast_shapes,
        scratch_shapes=[pltpu.VMEM((tm, d), BF16)],
        compiler_params=_params(2, 58),
        name="mlp",
    )(x, gain, w_up, w_down, *[w for w, _ in cast])


def _proj_body(*refs, n_x, n_a):
    x, a = _RowParts(refs[:n_x]), _RowParts(refs[n_x:n_x + n_a])
    w_ref, o_ref = refs[n_x + n_a:]
    o_ref[...] = x.load() + _dot(a.load(), w_ref[...])


def _proj_residual(x_parts, a_parts, w, *, n_tiles):
    d = w.shape[1]
    tm = sum(p.shape[0] for p in x_parts) // n_tiles
    assert tm == sum(p.shape[0] for p in a_parts) // n_tiles
    tile_spec = lambda p: pl.BlockSpec((p.shape[0] // n_tiles, p.shape[1]), lambda i: (i, 0))
    return pl.pallas_call(
        functools.partial(_proj_body, n_x=len(x_parts), n_a=len(a_parts)),
        grid=(n_tiles,),
        in_specs=[tile_spec(p) for p in (*x_parts, *a_parts)] + [
            pl.BlockSpec(w.shape, lambda i: (0, 0), pipeline_mode=pl.Buffered(1))],
        out_specs=pl.BlockSpec((tm, d), lambda i: (i, 0)),
        out_shape=jax.ShapeDtypeStruct((tm * n_tiles, d), F32),
        compiler_params=_params(1, 48),
        name="proj_residual",
    )(*x_parts, *a_parts, w)


def _conv_taps(u, u1, u2, w):
    return w[0:1, :] * u2 + w[1:2, :] * u1 + w[2:3, :] * u


def _conv_in_prompt_body(x_ref, g_ref, wb_ref, wc_ref, wv_ref, wconv_ref, *rest,
                         tiles_per_seq, n_cast):
    cast_src, (gate_ref, utail_ref), cast_dst, (h_ref, carry_ref) = (
        rest[:n_cast], rest[n_cast:n_cast + 2], rest[n_cast + 2:2 * n_cast + 2],
        rest[2 * n_cast + 2:])
    i = pl.program_id(0)
    j = pl.program_id(1)

    @pl.when(i % tiles_per_seq == 0)
    def _():
        carry_ref[j] = jnp.zeros(carry_ref.shape[1:], F32)

    tm = h_ref.shape[0]
    chunk = min(tm, CONV_ROW_CHUNK)
    row8 = lax.broadcasted_iota(jnp.int32, (SUBLANES, 1), 0)
    w_conv = wconv_ref[...]

    def run(first_column_tile):
        _cast_blocks(cast_src, cast_dst)
        prev = carry_ref[j]
        for c in range(tm // chunk):
            rows = pl.ds(c * chunk, chunk)
            if first_column_tile:
                h = _rms_norm(x_ref[rows, :], g_ref[...]).astype(BF16)
                h_ref[rows, :] = h
            else:
                h = h_ref[rows, :]
            u = _dot(h, wc_ref[...]) * _dot(h, wv_ref[...])
            u1 = pltpu.roll(u, 1, axis=0)
            u2 = pltpu.roll(u, 2, axis=0)
            top1 = jnp.where(row8 < 1, pltpu.roll(prev, 1, axis=0), u1[0:SUBLANES])
            top2 = jnp.where(row8 < 2, pltpu.roll(prev, 2, axis=0), u2[0:SUBLANES])
            u1 = jnp.concatenate([top1, u1[SUBLANES:]], axis=0)
            u2 = jnp.concatenate([top2, u2[SUBLANES:]], axis=0)
            conv = _conv_taps(u, u1, u2, w_conv)
            gate_ref[rows, :] = (_dot(h, wb_ref[...]) * conv).astype(BF16)
            prev = u[chunk - SUBLANES:]
        carry_ref[j] = prev
        utail_ref[0] = prev

    pl.when(j == 0)(functools.partial(run, True))
    pl.when(j != 0)(functools.partial(run, False))


def _conv_in_sample_body(x_ref, g_ref, wb_ref, wc_ref, wv_ref, wconv_ref, p1_ref, p2_ref,
                         gate_ref, u_ref, wb16_ref, wc16_ref, wv16_ref, h_ref, *, seq_len):
    j = pl.program_id(1)

    @pl.when(j == 0)
    def _():
        h_ref[...] = _rms_norm(x_ref[...], g_ref[...]).astype(BF16)

    _cast_blocks((wb_ref, wc_ref, wv_ref), (wb16_ref, wc16_ref, wv16_ref))
    h = h_ref[...]
    u = _dot(h, wc16_ref[...]) * _dot(h, wv16_ref[...])
    step = lax.broadcasted_iota(jnp.int32, (u.shape[0], 1), 0) % seq_len
    u1 = jnp.where(step < 1, p1_ref[...], pltpu.roll(u, 1, axis=0))
    u2 = jnp.where(step < 2, p2_ref[...], pltpu.roll(u, 2, axis=0))
    conv = _conv_taps(u, u1, u2, wconv_ref[...])
    u_ref[...] = u
    gate_ref[...] = (_dot(h, wb16_ref[...]) * conv).astype(BF16)


def _conv_in_specs(tm, d, tn, *, fused_weight):
    third = d // tn if fused_weight else 0
    return [
        pl.BlockSpec((tm, d), lambda i, j: (i, 0)),
        pl.BlockSpec((1, d), lambda i, j: (0, 0)),
        pl.BlockSpec((d, tn), lambda i, j: (0, j)),
        pl.BlockSpec((d, tn), lambda i, j: (0, third + j)),
        pl.BlockSpec((d, tn), lambda i, j: (0, 2 * third + j)),
        pl.BlockSpec((CONV_WIDTH, tn), lambda i, j: (0, j)),
    ]


def _conv_in_prompt(x, gain, w_bcv, w_conv, *, seq, tm, tn, cast=()):
    m, d = x.shape
    grid = (m // tm, d // tn)
    cast_in, cast_out, cast_shapes = _cast_specs(cast, grid)
    body = functools.partial(_conv_in_prompt_body, tiles_per_seq=seq // tm, n_cast=len(cast))
    return pl.pallas_call(
        body,
        grid=grid,
        in_specs=_conv_in_specs(tm, d, tn, fused_weight=False) + cast_in,
        out_specs=[
            pl.BlockSpec((tm, tn), lambda i, j: (i, j)),
            pl.BlockSpec((1, SUBLANES, tn), lambda i, j: (i, 0, j)),
        ] + cast_out,
        out_shape=[
            jax.ShapeDtypeStruct((m, d), BF16),
            jax.ShapeDtypeStruct((m // tm, SUBLANES, d), F32),
        ] + cast_shapes,
        scratch_shapes=[
            pltpu.VMEM((tm, d), BF16),
            pltpu.VMEM((d // tn, SUBLANES, tn), F32),
        ],
        compiler_params=_params(2, 56),
        name="conv_in_prompt",
    )(x, gain, *w_bcv, w_conv, *[w for w, _ in cast])


def _conv_in_sample(x, gain, w_in, w_conv, p1, p2, *, seq_len, tn):
    m, d = x.shape
    body = functools.partial(_conv_in_sample_body, seq_len=seq_len)
    column_tile = lambda rows: pl.BlockSpec((rows, tn), lambda i, j: (0, j))
    return pl.pallas_call(
        body,
        grid=(1, d // tn),
        in_specs=_conv_in_specs(m, d, tn, fused_weight=True) + [column_tile(m)] * 2,
        out_specs=[column_tile(m)] * 2 + [column_tile(d)] * 3,
        out_shape=[
            jax.ShapeDtypeStruct((m, d), BF16),
            jax.ShapeDtypeStruct((m, d), F32),
        ] + [jax.ShapeDtypeStruct((d, d), BF16)] * 3,
        scratch_shapes=[pltpu.VMEM((m, d), BF16)],
        compiler_params=_params(2, 48),
        name="conv_in_sample",
    )(x, gain, w_in, w_in, w_in, w_conv, p1, p2)


def _head_norm_rope(z, gain, ones_blockdiag, cos, sin_signed, *, split_ssq):
    w = z.shape[1]
    zz = z * z
    hi = zz.astype(BF16)
    ssq = _dot(hi, ones_blockdiag)
    if split_ssq:
        ssq += _dot((zz - hi.astype(F32)).astype(BF16), ones_blockdiag)
    zn = z * lax.rsqrt(ssq * (1.0 / HEAD_DIM) + EPS) * gain
    reps = w // cos.shape[1]
    cos = jnp.concatenate([cos] * reps, axis=1)
    sin_signed = jnp.concatenate([sin_signed] * reps, axis=1)
    lane = lax.broadcasted_iota(jnp.int32, (1, w), 1)
    first_half = (lane % HEAD_DIM) < (HEAD_DIM // 2)
    partner = jnp.where(first_half,
                        pltpu.roll(zn, w - HEAD_DIM // 2, axis=1),
                        pltpu.roll(zn, HEAD_DIM // 2, axis=1))
    return zn * cos + partner * sin_signed


def _qkv_body(x_ref, g_ref, w_ref, ones_ref, qgain_ref, kgain_ref,
              rope_p_ref, rope_s_ref,
              qp_ref, qs_ref, kvp_ref, kvs_ref, kv_tail_ref, *, seq_tiles):
    q_out, kv_out = _RowParts((qp_ref, qs_ref)), _RowParts((kvp_ref, kvs_ref))
    rope_tile = _RowParts((rope_p_ref, rope_s_ref))
    tm = x_ref.shape[0]
    qw = qp_ref.shape[1]
    tn = ones_ref.shape[0]
    kw = N_KV_HEADS * HEAD_DIM
    for start, size in _row_chunks(tm, QKV_ROW_CHUNK):
        h = _rms_norm(x_ref[pl.ds(start, size), :], g_ref[...]).astype(BF16)
        rope = rope_tile.load(start, size)
        cos, sin_signed = rope[:, :LANES], rope[:, LANES:]
        for t in range(qw // tn):
            z = _dot(h, w_ref[:, pl.ds(t * tn, tn)])
            for half in range(tn // kw):
                lo = half * kw
                q = _head_norm_rope(z[:, lo:lo + kw], qgain_ref[:, lo:lo + kw],
                                    ones_ref[0:kw, 0:kw], cos, sin_signed, split_ssq=False)
                q_out.store(start, q.astype(BF16), cols=pl.ds(t * tn + lo, kw))
        z = _dot(h, w_ref[:, pl.ds(qw, 2 * kw)])
        k = _head_norm_rope(z[:, :kw], kgain_ref[...], ones_ref[0:kw, 0:kw], cos, sin_signed,
                            split_ssq=True)
        kv_out.store(start, jnp.concatenate([k, z[:, kw:]], axis=1))

    @pl.when(pl.program_id(0) % seq_tiles == seq_tiles - 1)
    def _():
        kv_tail_ref[...] = kvp_ref[kvp_ref.shape[0] - WINDOW:, :]


def _qkv(x, gain, w_qkv, ones_blockdiag, q_gain, k_gain, rope_prompt, rope_sample, *,
         n_tiles, prompt_rows, seq_tiles):
    m, d = x.shape
    tm = m // n_tiles
    sample_rows = tm - prompt_rows
    qw = N_HEADS * HEAD_DIM
    kvw = 2 * N_KV_HEADS * HEAD_DIM
    tn = ones_blockdiag.shape[0]
    const = lambda shape: pl.BlockSpec(shape, lambda i: (0, 0), pipeline_mode=pl.Buffered(1))
    rope_p = pl.BlockSpec((prompt_rows, 2 * LANES), lambda i: (i % seq_tiles, 0))
    split = lambda width: [pl.BlockSpec((prompt_rows, width), lambda i: (i, 0)),
                           pl.BlockSpec((sample_rows, width), lambda i: (i, 0))]
    split_shape = lambda width, dtype: [
        jax.ShapeDtypeStruct((prompt_rows * n_tiles, width), dtype),
        jax.ShapeDtypeStruct((sample_rows * n_tiles, width), dtype)]
    return pl.pallas_call(
        functools.partial(_qkv_body, seq_tiles=seq_tiles),
        grid=(n_tiles,),
        in_specs=[
            pl.BlockSpec((tm, d), lambda i: (i, 0)),
            const((1, d)),
            const(w_qkv.shape),
            const((tn, tn)),
            const((1, tn)),
            const((1, kvw // 2)),
            rope_p,
            const((sample_rows, 2 * LANES)),
        ],
        out_specs=split(qw) + split(kvw) + [
            pl.BlockSpec((None, WINDOW, kvw), lambda i: (i // seq_tiles, 0, 0))],
        out_shape=split_shape(qw, BF16) + split_shape(kvw, F32) + [
            jax.ShapeDtypeStruct((n_tiles // seq_tiles, WINDOW, kvw), F32)],
        compiler_params=_params(1, 48),
        name="qkv",
    )(x, gain, w_qkv, ones_blockdiag, q_gain, k_gain, rope_prompt, rope_sample)


MASKED = -1e30
LOG2_E = math.log2(math.e)
Q_SCALE = HEAD_DIM ** -0.5 * LOG2_E


def _attn_prompt_body(sinks_ref, q_ref, kv_prev_ref, kv_cur_ref, *rest, n_cast):
    cast_src, (o_ref,), cast_dst = rest[:n_cast], rest[n_cast:n_cast + 1], rest[n_cast + 1:]
    _cast_blocks(cast_src, cast_dst)
    n = pl.program_id(1)
    blk = WINDOW
    n_blocks = q_ref.shape[0] // blk
    kw = N_KV_HEADS * HEAD_DIM
    r = lax.broadcasted_iota(jnp.int32, (blk, blk), 0)
    c = lax.broadcasted_iota(jnp.int32, (blk, blk), 1)
    from_prev = c > r
    missing_prev = from_prev & (n == 0)
    lower = lax.broadcasted_iota(jnp.int32, (1, LANES), 1) < HEAD_DIM
    heads_per_col = LANES // HEAD_DIM

    def padded_operands(kv):
        per_head = []
        for g in range(N_KV_HEADS):
            col, half = divmod(g, heads_per_col)
            kcol = kv[:, col * LANES:(col + 1) * LANES]
            vcol = kv[:, kw + col * LANES:kw + (col + 1) * LANES]
            kswap = pltpu.roll(kcol, HEAD_DIM, axis=1)
            vswap = pltpu.roll(vcol, HEAD_DIM, axis=1)
            in_lower = (kcol, vcol) if half == 0 else (kswap, vswap)
            in_upper = (kswap, vswap) if half == 0 else (kcol, vcol)
            per_head.append(tuple(jnp.where(lower, t, 0.0).astype(BF16) for t in in_lower)
                            + tuple(jnp.where(lower, 0.0, t).astype(BF16) for t in in_upper))
        return per_head

    key_blocks = [padded_operands(kv_prev_ref[...])]
    for b in range(n_blocks):
        rows = pl.ds(b * blk, blk)
        key_blocks.append(padded_operands(kv_cur_ref[rows, :]))
        for g in range(N_KV_HEADS):
            k_lo, v_lo, k_hi, v_hi = (jnp.concatenate(prev_and_own, axis=0) for prev_and_own
                                      in zip(key_blocks[b][g], key_blocks[b + 1][g]))
            for pair in range(GROUP // heads_per_col):
                h0 = g * GROUP + pair * heads_per_col
                lanes = pl.ds(h0 * HEAD_DIM, LANES)
                q_pair = q_ref[rows, lanes]
                acc = None
                for h, k, v in ((h0, k_lo, v_lo), (h0 + 1, k_hi, v_hi)):
                    s = lax.dot_general(q_pair, k, (((1,), (1,)), ((), ())),
                                        preferred_element_type=F32)
                    s = jnp.where(from_prev, s[:, :blk], s[:, blk:])
                    if b == 0:
                        s = jnp.where(missing_prev, MASKED, s)
                    sink = sinks_ref[h] * LOG2_E
                    m = jnp.maximum(jnp.max(s, axis=-1, keepdims=True), sink)
                    p = jnp.exp2(s - m)
                    den = jnp.sum(p, axis=-1, keepdims=True) + jnp.exp2(sink - m)
                    p = jnp.concatenate([jnp.where(from_prev, p, 0.0),
                                         jnp.where(from_prev, 0.0, p)], axis=1)
                    o = _dot(p.astype(BF16), v) * (1.0 / den)
                    acc = o if acc is None else acc + o
                o_ref[rows, lanes] = acc.astype(BF16)


def _attn_prompt(sinks, q, kv, *, batch, seq, tq, cast=()):
    m, qw = q.shape
    kvw = kv.shape[1]
    blk = WINDOW
    nt = seq // tq
    per_tile = tq // blk
    grid = (batch, nt)
    cast_in, cast_out, cast_shapes = _cast_specs(cast, grid)
    return pl.pallas_call(
        functools.partial(_attn_prompt_body, n_cast=len(cast)),
        grid_spec=pltpu.PrefetchScalarGridSpec(
            num_scalar_prefetch=1,
            grid=grid,
            in_specs=[
                pl.BlockSpec((tq, qw), lambda b, n, s: (b * nt + n, 0)),
                pl.BlockSpec((blk, kvw), lambda b, n, s: (
                    (b * nt + n) * per_tile - jnp.minimum(n, 1), 0)),
                pl.BlockSpec((tq, kvw), lambda b, n, s: (b * nt + n, 0)),
            ] + cast_in,
            out_specs=[pl.BlockSpec((tq, qw), lambda b, n, s: (b * nt + n, 0))] + cast_out,
        ),
        out_shape=[jax.ShapeDtypeStruct((m, qw), BF16)] + cast_shapes,
        compiler_params=_params(2, 56),
        name="attn_prompt",
    )(sinks, q, kv, kv, *[w for w, _ in cast])


def _attn_sample_body(sinks_ref, q_ref, kv_ref, ck_ref, cv_ref, o_ref, knew_ref, vnew_ref, *,
                      seq_len):
    nseq, win, kw = ck_ref.shape
    rows = GROUP * seq_len
    q = q_ref[...].astype(F32)
    kv_new = kv_ref[...]
    for new_ref, cache_ref, fresh in ((knew_ref, ck_ref, kv_new[:, :kw]),
                                      (vnew_ref, cv_ref, kv_new[:, kw:])):
        new_ref[:, :win - seq_len, :] = cache_ref[:, seq_len:, :]
        new_ref[:, win - seq_len:, :] = fresh.reshape(nseq, seq_len, kw)
    pad = jnp.zeros((nseq, win - seq_len, HEAD_DIM), F32)
    step = lax.broadcasted_iota(jnp.int32, (1, rows, 2 * win), 1) % seq_len
    c = lax.broadcasted_iota(jnp.int32, (1, rows, 2 * win), 2)
    visible = ((c < win) & (c > step + (win - WINDOW))) | ((c >= win) & (c - win <= step))
    pieces = []
    for g in range(N_KV_HEADS):
        lo, hi = g * HEAD_DIM, (g + 1) * HEAD_DIM
        heads = range(g * GROUP, (g + 1) * GROUP)
        k_new = kv_new[:, lo:hi].reshape(nseq, seq_len, HEAD_DIM)
        v_new = kv_new[:, kw + lo:kw + hi].reshape(nseq, seq_len, HEAD_DIM)
        k = jnp.concatenate([ck_ref[:, :, lo:hi], k_new, pad], axis=1).astype(BF16)
        v = jnp.concatenate([cv_ref[:, :, lo:hi], v_new, pad], axis=1).astype(BF16)
        qg = jnp.concatenate(
            [q[:, h * HEAD_DIM:(h + 1) * HEAD_DIM].reshape(nseq, seq_len, HEAD_DIM)
             for h in heads], axis=1).astype(BF16)
        s = jnp.einsum("bqd,bkd->bqk", qg, k, preferred_element_type=F32)
        s = jnp.where(visible, s, MASKED)
        sink_col = jnp.concatenate(
            [jnp.full((1, seq_len, 1), sinks_ref[h] * LOG2_E, F32) for h in heads], axis=1)
        m = jnp.maximum(jnp.max(s, axis=-1, keepdims=True), sink_col)
        p = jnp.exp2(s - m)
        den = jnp.sum(p, axis=-1, keepdims=True) + jnp.exp2(sink_col - m)
        o = jnp.einsum("bqk,bkd->bqd", p.astype(BF16), v, preferred_element_type=F32) / den
        pieces += [o[:, t * seq_len:(t + 1) * seq_len].reshape(nseq * seq_len, HEAD_DIM)
                   for t in range(GROUP)]
    o_ref[...] = jnp.concatenate(pieces, axis=1).astype(BF16)


def _attn_sample(sinks, q, kv, cache_k, cache_v, *, seq_len):
    m, qw = q.shape
    body = functools.partial(_attn_sample_body, seq_len=seq_len)
    full = lambda shape: pl.BlockSpec(shape, lambda i, s: (0,) * len(shape))
    return pl.pallas_call(
        body,
        grid_spec=pltpu.PrefetchScalarGridSpec(
            num_scalar_prefetch=1,
            grid=(1,),
            in_specs=[full(q.shape), full(kv.shape), full(cache_k.shape), full(cache_v.shape)],
            out_specs=[full((m, qw)), full(cache_k.shape), full(cache_v.shape)],
        ),
        out_shape=[jax.ShapeDtypeStruct((m, qw), BF16),
                   jax.ShapeDtypeStruct(cache_k.shape, F32),
                   jax.ShapeDtypeStruct(cache_v.shape, F32)],
        compiler_params=_params(1, 48),
        name="attn_sample",
    )(sinks, q, kv, cache_k, cache_v)


def _rope_tables(pos):
    half = HEAD_DIM // 2
    inv = ROPE_THETA ** (-jnp.arange(half, dtype=F32) / half)
    ang = pos.astype(F32)[:, None] * inv[None, :]
    cos, sin = jnp.cos(ang), jnp.sin(ang)
    reps = LANES // HEAD_DIM
    cos = jnp.tile(jnp.concatenate([cos, cos], axis=1), (1, reps))
    sin_signed = jnp.tile(jnp.concatenate([-sin, sin], axis=1), (1, reps))
    return jnp.concatenate([cos, sin_signed], axis=1)


def kernel(x_prompt, x_sample, state_conv, cache_k_win, cache_v_win, ln_mix, ln_mlp,
           w_conv_in, w_conv, w_conv_out, w_qkv, w_attn_out, q_norm, k_norm, sinks,
           w_up, w_down):
    bp, tp, d = x_prompt.shape
    bs, ts, _ = x_sample.shape
    win = cache_k_win.shape[2]
    kw = N_KV_HEADS * HEAD_DIM
    assert ts == SUBLANES and win == WINDOW and tp % WINDOW == 0

    tm_p = 512
    tm_conv = 1024
    m_s = bs * ts
    n_tiles = bp * tp // tm_p
    rows_s = m_s // n_tiles
    assert rows_s % (2 * SUBLANES) == 0 and rows_s % ts == 0

    xp = x_prompt.reshape(bp * tp, d)
    xs = x_sample.reshape(m_s, d)

    gain = ln_mix[0][None]
    past = state_conv[0]
    zeros = lambda n: jnp.zeros((bs, n, d), F32)
    p1 = jnp.concatenate([past[:, 1:2], zeros(ts - 1)], axis=1).reshape(m_s, d)
    p2 = jnp.concatenate([past, zeros(ts - 2)], axis=1).reshape(m_s, d)
    gate_s, u_s, *w_bcv = _conv_in_sample(xs, gain, w_conv_in[0], w_conv[0], p1, p2,
                                          seq_len=ts, tn=512)
    gate_p, u_tail, wu0, wd0, w_out = _conv_in_prompt(
        xp, gain, w_bcv, w_conv[0], seq=tp, tm=tm_conv, tn=512,
        cast=[(w_up, 0), (w_down, 0), (w_conv_out, 0)])
    x = _proj_residual((xp, xs), (gate_p, gate_s), w_out, n_tiles=n_tiles)
    tiles_per_seq = tp // tm_conv
    new_conv_prompt = u_tail[tiles_per_seq - 1::tiles_per_seq, SUBLANES - (CONV_WIDTH - 1):][None]
    new_conv_sample = u_s.reshape(bs, ts, d)[:, ts - (CONV_WIDTH - 1):][None]

    x, wqkv = _mlp(x, ln_mlp[0][None], wu0, wd0, n_tiles=n_tiles, tf=2048, cast=[(w_qkv, 0)])

    gain = ln_mix[1][None]
    tn = 2 * kw
    head_id = jnp.arange(tn) // HEAD_DIM
    ones_blockdiag = (head_id[:, None] == head_id[None, :]).astype(BF16)
    q_gain = jnp.tile(q_norm[0] * Q_SCALE, tn // HEAD_DIM)[None]
    k_gain = jnp.tile(k_norm[0], kw // HEAD_DIM)[None]
    rope_p = _rope_tables(jnp.arange(tp, dtype=jnp.int32))
    rope_s = _rope_tables(PAST_LEN + jnp.arange(ts, dtype=jnp.int32))
    rope_s = jnp.tile(rope_s, (rows_s // ts, 1))
    q_p, q_s, kv_p, kv_s, kv_tail = _qkv(
        x, gain, wqkv, ones_blockdiag, q_gain, k_gain, rope_p, rope_s,
        n_tiles=n_tiles, prompt_rows=tm_p, seq_tiles=tp // tm_p)
    o_p, wu1, wd1, wo = _attn_prompt(sinks[0], q_p, kv_p, batch=bp, seq=tp, tq=512,
                                     cast=[(w_up, 1), (w_down, 1), (w_attn_out, 0)])
    ck = cache_k_win[0].reshape(bs, win, kw)
    cv = cache_v_win[0].reshape(bs, win, kw)
    o_s, new_k_sample, new_v_sample = _attn_sample(sinks[0], q_s, kv_s, ck, cv, seq_len=ts)
    x = _proj_residual((x,), (o_p, o_s), wo, n_tiles=n_tiles)

    new_k_prompt = kv_tail[:, :, :kw].reshape(1, bp, WINDOW, N_KV_HEADS, HEAD_DIM)
    new_v_prompt = kv_tail[:, :, kw:].reshape(1, bp, WINDOW, N_KV_HEADS, HEAD_DIM)
    new_k_sample = new_k_sample.reshape(1, bs, win, N_KV_HEADS, HEAD_DIM)
    new_v_sample = new_v_sample.reshape(1, bs, win, N_KV_HEADS, HEAD_DIM)

    xp, xs = _mlp(x, ln_mlp[1][None], wu1, wd1, n_tiles=n_tiles, tf=2048,
                  out_rows=(tm_p, rows_s))

    return (xp.reshape(bp, tp, d), xs.reshape(bs, ts, d), new_conv_prompt, new_conv_sample,
            new_k_prompt, new_v_prompt, new_k_sample, new_v_sample)
```
